```python
import math
import jax, jax.numpy as jnp
from jax import lax
import numpy as np

D_MODEL = 1024
BATCH = 32
SEQ = 256
DEPTH = 2
DEC_BATCH = 8
DEC_SEQ = 1024
PAST_LEN = 512

GRID_W = 64
HEAD_DIM = 64
N_HEADS = 8
N_KV_HEADS = 2
Q_PER_KV = N_HEADS // N_KV_HEADS
ATTN_WIDTH = N_HEADS * HEAD_DIM
KV_WIDTH = N_KV_HEADS * HEAD_DIM
FOURIER_GROUP = 64
N_FOURIER_GROUPS = 8
FOURIER_WIDTH = N_FOURIER_GROUPS * FOURIER_GROUP
EVEN_IN = FOURIER_WIDTH + ATTN_WIDTH + 2 * KV_WIDTH
EVEN_OUT = FOURIER_WIDTH + ATTN_WIDTH
CONV_MIX_WIDTH = D_MODEL
CONV_K = 3
D_FF = 2816
Q_BLOCK = 128
ROPE_THETA = 10000.0
EPS = 1e-6
N_ATTN_LAYERS = (DEPTH + 1) // 2
N_CONV_LAYERS = DEPTH // 2
DEEPNORM_ALPHA = (2 * DEPTH) ** 0.25
DEEPNORM_BETA = (8 * DEPTH) ** -0.25

kernel_name = "hybrid_fourier_gqa_shortconv_dit_step"


def layer_norm(x, g, b):
    xf = x.astype(jnp.float32)
    mu = jnp.mean(xf, axis=-1, keepdims=True)
    xc = xf - mu
    var = jnp.mean(xc * xc, axis=-1, keepdims=True)
    y = xc * lax.rsqrt(var + EPS) * g.astype(jnp.float32) + b.astype(jnp.float32)
    return y.astype(x.dtype)


def rms_norm_head(x, g):
    xf = x.astype(jnp.float32)
    y = xf * lax.rsqrt(jnp.mean(xf * xf, axis=-1, keepdims=True) + EPS) * g.astype(jnp.float32)
    return y.astype(x.dtype)


def conv3_centred(x, w):
    xp = jnp.pad(x, ((0, 0), (1, 1), (0, 0)))
    return xp[:, :-2] * w[0] + xp[:, 1:-1] * w[1] + xp[:, 2:] * w[2]


def axial_angles(n):
    rows = n // GRID_W
    row = jnp.repeat(jnp.arange(rows), GRID_W).astype(jnp.float32)
    col = jnp.tile(jnp.arange(GRID_W), rows).astype(jnp.float32)
    half = HEAD_DIM // 2
    inv = 1.0 / (ROPE_THETA ** (jnp.arange(0, half, 2, dtype=jnp.float32) / half))
    return row[:, None] * inv, col[:, None] * inv


def rotate_half_rope(x, ang):
    cos = jnp.cos(ang)[None, :, None, :].astype(x.dtype)
    sin = jnp.sin(ang)[None, :, None, :].astype(x.dtype)
    x1, x2 = jnp.split(x, 2, axis=-1)
    return jnp.concatenate([x1 * cos - x2 * sin, x1 * sin + x2 * cos], axis=-1)


def apply_axial_rope(x, ang_r, ang_c):
    xr, xc = jnp.split(x, 2, axis=-1)
    return jnp.concatenate([rotate_half_rope(xr, ang_r), rotate_half_rope(xc, ang_c)], axis=-1)


def block_attention(q, k, v):
    b, sq = q.shape[0], q.shape[1]
    nb = sq // Q_BLOCK
    qb = q.reshape(b, nb, Q_BLOCK, N_KV_HEADS, Q_PER_KV, HEAD_DIM).transpose(1, 0, 2, 3, 4, 5)
    kf = k.astype(jnp.float32)
    scale = HEAD_DIM ** -0.5

    def one_block(qblk):
        s = jnp.einsum('bqkgd,bskd->bkgqs', qblk.astype(jnp.float32), kf) * scale
        p = jax.nn.softmax(s, axis=-1).astype(v.dtype)
        return jnp.einsum('bkgqs,bskd->bqkgd', p, v)

    o = lax.map(one_block, qb)
    return o.transpose(1, 0, 2, 3, 4, 5).reshape(b, sq, ATTN_WIDTH)


def fourier_mix(f):
    b, s = f.shape[0], f.shape[1]
    fg = f.reshape(b, s, N_FOURIER_GROUPS, FOURIER_GROUP).astype(jnp.float32)
    out = jnp.fft.fft2(fg, axes=(1, 3), norm='ortho').real
    return out.reshape(b, s, FOURIER_WIDTH).astype(f.dtype)


def even_mixer(h, w_in, q_g, k_g, w_out, ctx_kv):
    b, s = h.shape[0], h.shape[1]
    proj = h @ w_in
    f, q, k, v = jnp.split(proj, [FOURIER_WIDTH, FOURIER_WIDTH + ATTN_WIDTH,
                                  FOURIER_WIDTH + ATTN_WIDTH + KV_WIDTH], axis=-1)
    q = rms_norm_head(q.reshape(b, s, N_HEADS, HEAD_DIM), q_g)
    k = rms_norm_head(k.reshape(b, s, N_KV_HEADS, HEAD_DIM), k_g)
    v = v.reshape(b, s, N_KV_HEADS, HEAD_DIM)
    if ctx_kv is None:
        attn = block_attention(q, k, v)
    else:
        ang_r, ang_c = axial_angles(s)
        q = apply_axial_rope(q, ang_r, ang_c)
        k_lat = apply_axial_rope(k, ang_r, ang_c)
        keys = jnp.concatenate([k_lat, ctx_kv[0].astype(k.dtype)], axis=1)
        vals = jnp.concatenate([v, ctx_kv[1].astype(v.dtype)], axis=1)
        attn = block_attention(q, keys, vals)
    out = jnp.concatenate([fourier_mix(f), attn], axis=-1) @ w_out
    return out, k, v


def conv_mixer(h, w_in, conv_w, w_out):
    bg, cg, xin = jnp.split(h @ w_in, 3, axis=-1)
    return (bg * conv3_centred(cg * xin, conv_w)) @ w_out


def conv_ffn(h, w_up, conv_w, w_down):
    u = conv3_centred(h @ w_up, conv_w)
    a, g = jnp.split(u, 2, axis=-1)
    return (jax.nn.silu(g) * a) @ w_down


def modulation(cond, w, b):
    return jnp.split(jax.nn.silu(cond) @ w + b, 6, axis=-1)


def modulate(x, shift, scale):
    return x * (1.0 + scale[:, None, :]) + shift[:, None, :]


def post_norm_residual(x, out, gate, g, b):
    return layer_norm(DEEPNORM_ALPHA * x + gate[:, None, :] * out, g, b)


def setup_inputs(seed: int = 0) -> dict:
    key = jax.random.key(seed)
    ks = jax.random.split(key, 24)
    f32 = jnp.float32
    nrm = lambda k, shp, s: jax.random.normal(k, shp, f32) * s
    d = D_MODEL
    return {
        "x_prompt": nrm(ks[0], (BATCH, SEQ, d), 1.0),
        "x_sample": nrm(ks[1], (DEC_BATCH, DEC_SEQ, d), 1.0),
        "cache_k": nrm(ks[2], (DEC_BATCH, N_ATTN_LAYERS, PAST_LEN, N_KV_HEADS, HEAD_DIM), 1.0),
        "cache_v": nrm(ks[3], (DEC_BATCH, N_ATTN_LAYERS, PAST_LEN, N_KV_HEADS, HEAD_DIM), 1.0),
        "c": nrm(ks[4], (DEC_BATCH, d), 1.0),
        "c_ctx": nrm(ks[5], (d,), 1.0),
        "w_ada": nrm(ks[6], (DEPTH, d, 6 * d), 0.5 * d ** -0.5),
        "b_ada": nrm(ks[7], (DEPTH, 6 * d), 0.02),
        "ln_g": 1.0 + nrm(ks[8], (DEPTH, 2, d), 0.02),
        "ln_b": nrm(ks[9], (DEPTH, 2, d), 0.02),
        "w_in_a": nrm(ks[10], (N_ATTN_LAYERS, d, EVEN_IN), d ** -0.5),
        "q_norm_g": 1.0 + nrm(ks[11], (N_ATTN_LAYERS, HEAD_DIM), 0.02),
        "k_norm_g": 1.0 + nrm(ks[12], (N_ATTN_LAYERS, HEAD_DIM), 0.02),
        "w_out_a": nrm(ks[13], (N_ATTN_LAYERS, EVEN_OUT, d), DEEPNORM_BETA * EVEN_OUT ** -0.5),
        "w_in_c": nrm(ks[14], (N_CONV_LAYERS, d, 3 * CONV_MIX_WIDTH), d ** -0.5),
        "conv_c": nrm(ks[15], (N_CONV_LAYERS, CONV_K, CONV_MIX_WIDTH), CONV_K ** -0.5),
        "w_out_c": nrm(ks[16], (N_CONV_LAYERS, CONV_MIX_WIDTH, d), DEEPNORM_BETA * CONV_MIX_WIDTH ** -0.5),
        "w_up": nrm(ks[17], (DEPTH, d, 2 * D_FF), d ** -0.5),
        "conv_f": nrm(ks[18], (DEPTH, CONV_K, 2 * D_FF), CONV_K ** -0.5),
        "w_down": nrm(ks[19], (DEPTH, D_FF, d), DEEPNORM_BETA * D_FF ** -0.5),
    }


def reference(x_prompt, x_sample, cache_k, cache_v, c, c_ctx, w_ada, b_ada, ln_g, ln_b,
              w_in_a, q_norm_g, k_norm_g, w_out_a, w_in_c, conv_c, w_out_c,
              w_up, conv_f, w_down):
    xp = x_prompt
    xs = x_sample
    new_k = []
    new_v = []
    for l in range(DEPTH):
        i = l // 2
        sh1_p, sc1_p, g1_p, sh2_p, sc2_p, g2_p = modulation(c_ctx[None, :], w_ada[l], b_ada[l])
        sh1_s, sc1_s, g1_s, sh2_s, sc2_s, g2_s = modulation(c, w_ada[l], b_ada[l])
        hp = modulate(xp, sh1_p, sc1_p)
        hs = modulate(xs, sh1_s, sc1_s)
        if l % 2 == 0:
            op, kp, vp = even_mixer(hp, w_in_a[i], q_norm_g[i], k_norm_g[i], w_out_a[i], None)
            os_, _, _ = even_mixer(hs, w_in_a[i], q_norm_g[i], k_norm_g[i], w_out_a[i],
                                   (cache_k[:, i], cache_v[:, i]))
            new_k.append(kp)
            new_v.append(vp)
        else:
            op = conv_mixer(hp, w_in_c[i], conv_c[i], w_out_c[i])
            os_ = conv_mixer(hs, w_in_c[i], conv_c[i], w_out_c[i])
        xp = post_norm_residual(xp, op, g1_p, ln_g[l, 0], ln_b[l, 0])
        xs = post_norm_residual(xs, os_, g1_s, ln_g[l, 0], ln_b[l, 0])
        fp = conv_ffn(modulate(xp, sh2_p, sc2_p), w_up[l], conv_f[l], w_down[l])
        fs = conv_ffn(modulate(xs, sh2_s, sc2_s), w_up[l], conv_f[l], w_down[l])
        xp = post_norm_residual(xp, fp, g2_p, ln_g[l, 1], ln_b[l, 1])
        xs = post_norm_residual(xs, fs, g2_s, ln_g[l, 1], ln_b[l, 1])
    new_cache_k = jnp.stack(new_k, axis=1)
    new_cache_v = jnp.stack(new_v, axis=1)
    return (xp, xs, new_cache_k, new_cache_v)
```

```python
import functools

import numpy as np
import jax
import jax.numpy as jnp
from jax import lax
from jax.experimental import pallas as pl
from jax.experimental.pallas import tpu as pltpu

D_MODEL = 1024
DEPTH = 2
GRID_W = 64
HEAD_DIM = 64
N_HEADS = 8
N_KV_HEADS = 2
Q_PER_KV = N_HEADS // N_KV_HEADS
ATTN_WIDTH = N_HEADS * HEAD_DIM
KV_WIDTH = N_KV_HEADS * HEAD_DIM
QK_WIDTH = ATTN_WIDTH + KV_WIDTH
FOURIER_GROUP = 64
FOURIER_WIDTH = 512
D_FF = 2816
ROPE_THETA = 10000.0
EPS = 1e-6
DEEPNORM_ALPHA = (2 * DEPTH) ** 0.25

F32 = jnp.float32
BF16 = jnp.bfloat16

SUBLANES = 8
LANES = 128
MXU_WIDTH = 256

ROW_TILE = 256
HALO = SUBLANES
EXT_ROWS = ROW_TILE + 2 * HALO
FF_CHUNK = MXU_WIDTH
N_FF_CHUNKS = D_FF // FF_CHUNK
N_MIX_CHUNKS = D_MODEL // FF_CHUNK
PROJ_WIDTH = 2 * FOURIER_WIDTH + QK_WIDTH + KV_WIDTH
N_COND_ROWS = 16
VMEM_LIMIT_BYTES = 48 * 1024 * 1024


def _dot(a, b):
    return jnp.dot(a, b, preferred_element_type=F32)


def _split_bf16(a):
    hi = a.astype(BF16)
    lo = (a - hi.astype(F32)).astype(BF16)
    return hi, lo


def _sigmoid(x):
    return 1.0 / (1.0 + jnp.exp(-x))


def _layer_norm(y, g, b):
    mu = jnp.mean(y, axis=-1, keepdims=True)
    yc = y - mu
    var = jnp.mean(yc * yc, axis=-1, keepdims=True)
    return yc * lax.rsqrt(var + EPS) * g + b


def _params(n_axes):
    return pltpu.CompilerParams(dimension_semantics=("arbitrary",) * n_axes,
                                vmem_limit_bytes=VMEM_LIMIT_BYTES)


def _resident(block, index_map):
    return pl.BlockSpec(block, index_map, pipeline_mode=pl.Buffered(1))


def _mod_kernel(cond_ref, w_ref, b_ref, o_ref):
    c = cond_ref[...]
    s = (c * _sigmoid(c)).astype(BF16)
    o_ref[...] = _dot(s, w_ref[...].astype(BF16)) + b_ref[...]


def _modulation(cond, w_ada, b_ada):
    tn = 1536
    n_out = 6 * D_MODEL
    return pl.pallas_call(
        _mod_kernel,
        grid=(DEPTH, n_out // tn),
        in_specs=[
            pl.BlockSpec((N_COND_ROWS, D_MODEL), lambda l, n: (0, 0)),
            pl.BlockSpec((None, D_MODEL, tn), lambda l, n: (l, 0, n)),
            pl.BlockSpec((None, 1, tn), lambda l, n: (l, 0, n)),
        ],
        out_specs=pl.BlockSpec((None, N_COND_ROWS, tn), lambda l, n: (l, 0, n)),
        out_shape=jax.ShapeDtypeStruct((DEPTH, N_COND_ROWS, n_out), F32),
        compiler_params=_params(2),
        name="modulation",
    )(cond, w_ada, b_ada.reshape(DEPTH, 1, n_out))


def _fold_kernel(w_ref, c_ref, s_ref, o_ref):
    j = pl.program_id(0)
    n_fourier_blocks = FOURIER_WIDTH // MXU_WIDTH

    def dft(m_ref):
        w_hi, w_lo = _split_bf16(w_ref[...])
        m_hi, m_lo = _split_bf16(m_ref[...])
        return _dot(w_hi, m_hi) + _dot(w_hi, m_lo) + _dot(w_lo, m_hi)

    @pl.when(j < n_fourier_blocks)
    def _():
        o_ref[...] = dft(c_ref).astype(BF16)

    @pl.when(jnp.logical_and(j >= n_fourier_blocks, j < 2 * n_fourier_blocks))
    def _():
        o_ref[...] = dft(s_ref).astype(BF16)

    @pl.when(j >= 2 * n_fourier_blocks)
    def _():
        o_ref[...] = w_ref[...].astype(BF16)


def _fold_in_proj(w_in_a, dft_c, dft_s):
    nb = FOURIER_WIDTH // MXU_WIDTH
    return pl.pallas_call(
        _fold_kernel,
        grid=(PROJ_WIDTH // MXU_WIDTH,),
        in_specs=[
            pl.BlockSpec((None, D_MODEL, MXU_WIDTH), lambda j: (0, 0, jnp.where(j < nb, j, j - nb))),
            pl.BlockSpec((MXU_WIDTH, MXU_WIDTH), lambda j: (0, 0)),
            pl.BlockSpec((MXU_WIDTH, MXU_WIDTH), lambda j: (0, 0)),
        ],
        out_specs=pl.BlockSpec((D_MODEL, MXU_WIDTH), lambda j: (0, j)),
        out_shape=jax.ShapeDtypeStruct((D_MODEL, PROJ_WIDTH), BF16),
        compiler_params=_params(1),
        name="fold_in_proj",
    )(w_in_a, dft_c, dft_s)


def _in_proj_kernel(*refs, rope):
    if rope:
        (x_ref, mod_ref, w_ref, hs_ref, g_ref, cos_ref, sa_ref, sb_ref,
         fcs_ref, q_ref, k_ref, v_ref) = refs
    else:
        x_ref, mod_ref, w_ref, hs_ref, g_ref, fcs_ref, q_ref, k_ref, v_ref = refs
    shift = mod_ref[0:1, :]
    scale = mod_ref[1:2, :]
    h = (x_ref[...] * (1.0 + scale) + shift).astype(BF16)
    proj = _dot(h, w_ref[...])
    fcs_ref[...] = proj[:, :2 * FOURIER_WIDTH].astype(BF16)
    qk = proj[:, 2 * FOURIER_WIDTH:2 * FOURIER_WIDTH + QK_WIDTH]
    v = proj[:, 2 * FOURIER_WIDTH + QK_WIDTH:]
    sq_hi, sq_lo = _split_bf16(qk * qk)
    ssq = _dot(sq_hi, hs_ref[...]) + _dot(sq_lo, hs_ref[...])
    y = qk * lax.rsqrt(ssq * (1.0 / HEAD_DIM) + EPS) * g_ref[...]
    if rope:
        parts = []
        for t in range(QK_WIDTH // LANES):
            yt = y[:, t * LANES:(t + 1) * LANES]
            lo = slice(t * LANES, (t + 1) * LANES)
            parts.append(yt * cos_ref[:, lo]
                         + pltpu.roll(yt, LANES - 16, 1) * sa_ref[:, lo]
                         + pltpu.roll(yt, 16, 1) * sb_ref[:, lo])
        y = jnp.concatenate(parts, axis=1)
    q_ref[...] = (y[:, :ATTN_WIDTH] * (HEAD_DIM ** -0.5)).astype(q_ref.dtype)
    k_ref[...] = y[:, ATTN_WIDTH:].astype(k_ref.dtype)
    v_ref[...] = v.astype(v_ref.dtype)


def _in_proj(x, mod4, mod_row, w_cat, head_sum, gains, rope_tabs):
    b, s, _ = x.shape
    rope = rope_tabs is not None
    kv_dtype = BF16 if rope else F32
    row_spec = lambda w: pl.BlockSpec((None, ROW_TILE, w), lambda bi, i: (bi, i, 0))
    in_specs = [
        row_spec(D_MODEL),
        pl.BlockSpec((None, None, 6, D_MODEL), lambda bi, i: (0, mod_row(bi), 0, 0)),
        _resident((D_MODEL, PROJ_WIDTH), lambda bi, i: (0, 0)),
        _resident((QK_WIDTH, QK_WIDTH), lambda bi, i: (0, 0)),
        _resident((1, QK_WIDTH), lambda bi, i: (0, 0)),
    ]
    args = [x, mod4, w_cat, head_sum, gains]
    if rope:
        in_specs += [pl.BlockSpec((ROW_TILE, QK_WIDTH), lambda bi, i: (i, 0))] * 3
        args += list(rope_tabs)
    return pl.pallas_call(
        functools.partial(_in_proj_kernel, rope=rope),
        grid=(b, s // ROW_TILE),
        in_specs=in_specs,
        out_specs=[row_spec(2 * FOURIER_WIDTH), row_spec(ATTN_WIDTH), row_spec(KV_WIDTH), row_spec(KV_WIDTH)],
        out_shape=[
            jax.ShapeDtypeStruct((b, s, 2 * FOURIER_WIDTH), BF16),
            jax.ShapeDtypeStruct((b, s, ATTN_WIDTH), BF16),
            jax.ShapeDtypeStruct((b, s, KV_WIDTH), kv_dtype),
            jax.ShapeDtypeStruct((b, s, KV_WIDTH), kv_dtype),
        ],
        compiler_params=_params(2),
        name="in_proj_rope" if rope else "in_proj",
    )(*args)


def _attn_kernel(*refs, has_cache):
    if has_cache:
        q_ref, k_ref, v_ref, ck_ref, cv_ref, o_ref = refs
    else:
        q_ref, k_ref, v_ref, o_ref = refs
    tq = q_ref.shape[0]
    q = q_ref[...]
    k = k_ref[...].astype(BF16)
    v = v_ref[...].astype(BF16)
    if has_cache:
        ck = ck_ref[...].astype(BF16)
        cv = cv_ref[...].astype(BF16)
    nt = (((1,), (1,)), ((), ()))
    outs = []
    for j in range(N_KV_HEADS):
        kv_cols = slice(j * HEAD_DIM, (j + 1) * HEAD_DIM)
        q4 = jnp.concatenate(
            [q[:, (j * Q_PER_KV + g) * HEAD_DIM:(j * Q_PER_KV + g + 1) * HEAD_DIM] for g in range(Q_PER_KV)],
            axis=0)
        s = lax.dot_general(q4, k[:, kv_cols], nt, preferred_element_type=F32)
        m = jnp.max(s, axis=-1, keepdims=True)
        if has_cache:
            s2 = lax.dot_general(q4, ck[:, kv_cols], nt, preferred_element_type=F32)
            m = jnp.maximum(m, jnp.max(s2, axis=-1, keepdims=True))
        p = jnp.exp(s - m)
        denom = jnp.sum(p, axis=-1, keepdims=True)
        o = _dot(p.astype(BF16), v[:, kv_cols])
        if has_cache:
            p2 = jnp.exp(s2 - m)
            denom = denom + jnp.sum(p2, axis=-1, keepdims=True)
            o = o + _dot(p2.astype(BF16), cv[:, kv_cols])
        o = o / denom
        outs += [o[g * tq:(g + 1) * tq] for g in range(Q_PER_KV)]
    o_ref[...] = jnp.concatenate(outs, axis=1).astype(o_ref.dtype)


def _attention(q, k, v, cache, tq):
    b, s, _ = q.shape
    has_cache = cache is not None
    seq_spec = lambda n: pl.BlockSpec((None, n, KV_WIDTH), lambda bi, i: (bi, 0, 0))
    in_specs = [pl.BlockSpec((None, tq, ATTN_WIDTH), lambda bi, i: (bi, i, 0)), seq_spec(s), seq_spec(s)]
    args = [q, k, v]
    if has_cache:
        past = cache[0].shape[1]
        in_specs += [seq_spec(past), seq_spec(past)]
        args += list(cache)
    return pl.pallas_call(
        functools.partial(_attn_kernel, has_cache=has_cache),
        grid=(b, s // tq),
        in_specs=in_specs,
        out_specs=pl.BlockSpec((None, tq, ATTN_WIDTH), lambda bi, i: (bi, i, 0)),
        out_shape=jax.ShapeDtypeStruct((b, s, ATTN_WIDTH), BF16),
        compiler_params=_params(2),
        name="attention_cached" if has_cache else "attention",
    )(*args)


def _out_proj_kernel(x_ref, fcs_ref, dc_ref, ds_ref, attn_ref, w_ref, mod_ref, lng_ref, lnb_ref, o_ref):
    four = (_dot(dc_ref[...], fcs_ref[:, :FOURIER_WIDTH])
            - _dot(ds_ref[...], fcs_ref[:, FOURIER_WIDTH:]))
    out = (_dot(four.astype(BF16), w_ref[:FOURIER_WIDTH, :])
           + _dot(attn_ref[...], w_ref[FOURIER_WIDTH:, :]))
    gate = mod_ref[2:3, :]
    y = DEEPNORM_ALPHA * x_ref[...] + gate * out
    o_ref[...] = _layer_norm(y, lng_ref[0:1, :], lnb_ref[0:1, :])


def _out_proj(x, fcs, attn, dft_c, dft_s, w_out, mod4, mod_row, ln_g, ln_b, layer):
    b, s, _ = x.shape
    row_spec = lambda w: pl.BlockSpec((None, ROW_TILE, w), lambda bi, i: (bi, i, 0))
    ln_spec = pl.BlockSpec((None, 2, D_MODEL), lambda bi, i: (layer, 0, 0))
    return pl.pallas_call(
        _out_proj_kernel,
        grid=(b, s // ROW_TILE),
        in_specs=[
            row_spec(D_MODEL),
            pl.BlockSpec((None, s, 2 * FOURIER_WIDTH), lambda bi, i: (bi, 0, 0)),
            pl.BlockSpec((ROW_TILE, s), lambda bi, i: (i, 0)),
            pl.BlockSpec((ROW_TILE, s), lambda bi, i: (i, 0)),
            row_spec(ATTN_WIDTH),
            _resident((2 * FOURIER_WIDTH, D_MODEL), lambda bi, i: (0, 0)),
            pl.BlockSpec((None, None, 6, D_MODEL), lambda bi, i: (layer, mod_row(bi), 0, 0)),
            ln_spec, ln_spec,
        ],
        out_specs=row_spec(D_MODEL),
        out_shape=jax.ShapeDtypeStruct((b, s, D_MODEL), F32),
        compiler_params=_params(2),
        name="out_proj",
    )(x, fcs, dft_c, dft_s, attn, w_out, mod4, ln_g, ln_b)


def _load_modulated_rows(x_ref, xp_ref, xn_ref, shift, scale, h_ref):
    i = pl.program_id(1)
    has_prev = jnp.where(i > 0, 1.0, 0.0)
    has_next = jnp.where(i < pl.num_programs(1) - 1, 1.0, 0.0)
    mod = lambda t: t * (1.0 + scale) + shift
    h_ref[0:HALO, :] = (mod(xp_ref[...]) * has_prev).astype(BF16)
    h_ref[HALO:HALO + ROW_TILE, :] = mod(x_ref[...]).astype(BF16)
    h_ref[HALO + ROW_TILE:, :] = (mod(xn_ref[...]) * has_next).astype(BF16)


def _conv3_rows(t_ref, cw):
    return (cw[0:1, :] * t_ref[HALO - 1:HALO - 1 + ROW_TILE, :]
            + cw[1:2, :] * t_ref[HALO:HALO + ROW_TILE, :]
            + cw[2:3, :] * t_ref[HALO + 1:HALO + 1 + ROW_TILE, :])


def _residual_norm(x_ref, acc_ref, gate, lng, lnb, o_ref):
    y = DEEPNORM_ALPHA * x_ref[...] + gate * acc_ref[...]
    o_ref[...] = _layer_norm(y, lng, lnb)


def _ffn_kernel(x_ref, xp_ref, xn_ref, mod_ref, wup_ref, cw_ref, wdn_ref, lng_ref, lnb_ref, o_ref,
                h_ref, u_ref, acc_ref):
    _load_modulated_rows(x_ref, xp_ref, xn_ref, mod_ref[3:4, :], mod_ref[4:5, :], h_ref)
    acc_ref[...] = jnp.zeros_like(acc_ref)

    def body(j, carry):
        u_ref[...] = _dot(h_ref[...], wup_ref[j])
        c = _conv3_rows(u_ref, cw_ref[j])
        a = c[:, :FF_CHUNK]
        g = c[:, FF_CHUNK:]
        act = (g * _sigmoid(g) * a).astype(BF16)
        acc_ref[...] += _dot(act, wdn_ref[j])
        return carry

    lax.fori_loop(0, N_FF_CHUNKS, body, 0)
    _residual_norm(x_ref, acc_ref, mod_ref[5:6, :], lng_ref[1:2, :], lnb_ref[1:2, :], o_ref)


def _mixer_kernel(x_ref, xp_ref, xn_ref, mod_ref, win_ref, cw_ref, wout_ref, lng_ref, lnb_ref, o_ref,
                  h_ref, t_ref, acc_ref):
    _load_modulated_rows(x_ref, xp_ref, xn_ref, mod_ref[0:1, :], mod_ref[1:2, :], h_ref)
    acc_ref[...] = jnp.zeros_like(acc_ref)

    def body(j, carry):
        p = _dot(h_ref[...], win_ref[j])
        t_ref[...] = p[:, FF_CHUNK:2 * FF_CHUNK] * p[:, 2 * FF_CHUNK:]
        bg = p[HALO:HALO + ROW_TILE, :FF_CHUNK]
        y = (bg * _conv3_rows(t_ref, cw_ref[j])).astype(BF16)
        acc_ref[...] += _dot(y, wout_ref[j])
        return carry

    lax.fori_loop(0, N_MIX_CHUNKS, body, 0)
    _residual_norm(x_ref, acc_ref, mod_ref[2:3, :], lng_ref[0:1, :], lnb_ref[0:1, :], o_ref)


def _gated_conv_block(kernel_fn, name, x, mod4, mod_row, w1, cw, w2, ln_g, ln_b, layer, w_layer):
    b, s, _ = x.shape
    n_chunks, _, w1_cols = w1.shape[1:]
    conv_cols = cw.shape[-1]
    halo_blocks_per_tile = ROW_TILE // HALO
    last_halo_block = s // HALO - 1
    ln_spec = pl.BlockSpec((None, 2, D_MODEL), lambda bi, i: (layer, 0, 0))
    return pl.pallas_call(
        kernel_fn,
        grid=(b, s // ROW_TILE),
        in_specs=[
            pl.BlockSpec((None, ROW_TILE, D_MODEL), lambda bi, i: (bi, i, 0)),
            pl.BlockSpec((None, HALO, D_MODEL),
                         lambda bi, i: (bi, jnp.maximum(i * halo_blocks_per_tile - 1, 0), 0)),
            pl.BlockSpec((None, HALO, D_MODEL),
                         lambda bi, i: (bi, jnp.minimum((i + 1) * halo_blocks_per_tile, last_halo_block), 0)),
            pl.BlockSpec((None, None, 6, D_MODEL), lambda bi, i: (layer, mod_row(bi), 0, 0)),
            _resident((None, n_chunks, D_MODEL, w1_cols), lambda bi, i: (w_layer, 0, 0, 0)),
            _resident((None, n_chunks, 3, conv_cols), lambda bi, i: (w_layer, 0, 0, 0)),
            _resident((None, n_chunks, FF_CHUNK, D_MODEL), lambda bi, i: (w_layer, 0, 0, 0)),
            ln_spec, ln_spec,
        ],
        out_specs=pl.BlockSpec((None, ROW_TILE, D_MODEL), lambda bi, i: (bi, i, 0)),
        out_shape=jax.ShapeDtypeStruct((b, s, D_MODEL), F32),
        scratch_shapes=[
            pltpu.VMEM((EXT_ROWS, D_MODEL), BF16),
            pltpu.VMEM((EXT_ROWS, conv_cols), F32),
            pltpu.VMEM((ROW_TILE, D_MODEL), F32),
        ],
        compiler_params=_params(2),
        name=name,
    )(x, x, x, mod4, w1, cw, w2, ln_g, ln_b)


def _dft_tables(n, scale):
    jk = np.outer(np.arange(n), np.arange(n)) % n
    ang = 2.0 * np.pi * jk / n
    return (np.cos(ang) * scale).astype(np.float32), (np.sin(ang) * scale).astype(np.float32)


def _channel_dft_blocks():
    c, s = _dft_tables(FOURIER_GROUP, 1.0)
    eye = np.eye(MXU_WIDTH // FOURIER_GROUP, dtype=np.float32)
    return np.kron(eye, c), np.kron(eye, s)


def _rope_tables(n):
    half = HEAD_DIM // 2
    inv = 1.0 / (ROPE_THETA ** (np.arange(0, half, 2, dtype=np.float64) / half))
    pos = np.arange(n)
    d = np.arange(QK_WIDTH) % HEAD_DIM
    coord = np.where(d < half, (pos // GRID_W)[:, None], (pos % GRID_W)[:, None])
    ang = coord * inv[d % (half // 2)][None, :]
    first = (d % half) < (half // 2)
    cos = np.cos(ang)
    sa = np.where(first[None, :], -np.sin(ang), 0.0)
    sb = np.where(first[None, :], 0.0, np.sin(ang))
    return tuple(jnp.asarray(t.astype(np.float32)) for t in (cos, sa, sb))


def _head_sum_matrix():
    return np.kron(np.eye(QK_WIDTH // HEAD_DIM, dtype=np.float32), np.ones((HEAD_DIM, HEAD_DIM), np.float32))


def kernel(x_prompt, x_sample, cache_k, cache_v, c, c_ctx, w_ada, b_ada, ln_g, ln_b, w_in_a, q_norm_g,
           k_norm_g, w_out_a, w_in_c, conv_c, w_out_c, w_up, conv_f, w_down):
    n_prompt, s_prompt, _ = x_prompt.shape
    n_sample, s_sample, _ = x_sample.shape
    past = cache_k.shape[2]

    cond = jnp.concatenate(
        [c_ctx[None, :], c, jnp.zeros((N_COND_ROWS - 1 - n_sample, D_MODEL), F32)], axis=0)
    mod4 = _modulation(cond, w_ada, b_ada).reshape(DEPTH, N_COND_ROWS, 6, D_MODEL)

    chan_c, chan_s = _channel_dft_blocks()
    w_cat = _fold_in_proj(w_in_a, jnp.asarray(chan_c), jnp.asarray(chan_s))
    head_sum = jnp.asarray(_head_sum_matrix()).astype(BF16)
    gains = jnp.concatenate([jnp.tile(q_norm_g[0], N_HEADS), jnp.tile(k_norm_g[0], N_KV_HEADS)])[None, :]
    w_out_bf = w_out_a[0].astype(BF16)

    w_up_r = (w_up.astype(BF16).reshape(DEPTH, D_MODEL, 2, N_FF_CHUNKS, FF_CHUNK)
              .transpose(0, 3, 1, 2, 4).reshape(DEPTH, N_FF_CHUNKS, D_MODEL, 2 * FF_CHUNK))
    conv_f_r = (conv_f.reshape(DEPTH, 3, 2, N_FF_CHUNKS, FF_CHUNK)
                .transpose(0, 3, 1, 2, 4).reshape(DEPTH, N_FF_CHUNKS, 3, 2 * FF_CHUNK))
    w_down_r = w_down.astype(BF16).reshape(DEPTH, N_FF_CHUNKS, FF_CHUNK, D_MODEL)
    n_conv_layers = w_in_c.shape[0]
    w_in_c_r = (w_in_c.astype(BF16).reshape(n_conv_layers, D_MODEL, 3, N_MIX_CHUNKS, FF_CHUNK)
                .transpose(0, 3, 1, 2, 4).reshape(n_conv_layers, N_MIX_CHUNKS, D_MODEL, 3 * FF_CHUNK))
    conv_c_r = conv_c.reshape(n_conv_layers, 3, N_MIX_CHUNKS, FF_CHUNK).transpose(0, 2, 1, 3)
    w_out_c_r = w_out_c.astype(BF16).reshape(n_conv_layers, N_MIX_CHUNKS, FF_CHUNK, D_MODEL)

    def run_stream(x, mod_row, cache, tq):
        s = x.shape[1]
        scale = (FOURIER_GROUP * s) ** -0.5
        dft_c, dft_s = (jnp.asarray(t).astype(BF16) for t in _dft_tables(s, scale))
        rope_tabs = _rope_tables(s) if cache is not None else None
        fcs, q, k, v = _in_proj(x, mod4, mod_row, w_cat, head_sum, gains, rope_tabs)
        attn = _attention(q, k, v, cache, tq)
        x = _out_proj(x, fcs, attn, dft_c, dft_s, w_out_bf, mod4, mod_row, ln_g, ln_b, 0)
        x = _gated_conv_block(_ffn_kernel, "conv_ffn", x, mod4, mod_row, w_up_r, conv_f_r, w_down_r,
                              ln_g, ln_b, 0, 0)
        x = _gated_conv_block(_mixer_kernel, "conv_mixer", x, mod4, mod_row, w_in_c_r, conv_c_r, w_out_c_r,
                              ln_g, ln_b, 1, 0)
        x = _gated_conv_block(_ffn_kernel, "conv_ffn", x, mod4, mod_row, w_up_r, conv_f_r, w_down_r,
                              ln_g, ln_b, 1, 1)
        return x, k, v

    y_prompt, k_new, v_new = run_stream(x_prompt, lambda bi: 0, None, ROW_TILE)
    cache = (cache_k[:, 0].reshape(n_sample, past, KV_WIDTH), cache_v[:, 0].reshape(n_sample, past, KV_WIDTH))
    y_sample, _, _ = run_stream(x_sample, lambda bi: bi + 1, cache, 128)

    new_shape = (n_prompt, 1, s_prompt, N_KV_HEADS, HEAD_DIM)
    return y_prompt, y_sample, k_new.reshape(new_shape), v_new.reshape(new_shape)
```

```python
import functools

import numpy as np
import jax
import jax.numpy as jnp
from jax import lax
from jax.experimental import pallas as pl
from jax.experimental.pallas import tpu as pltpu

D_MODEL = 1024
DEPTH = 2
GRID_W = 64
HEAD_DIM = 64
N_HEADS = 8
N_KV_HEADS = 2
Q_PER_KV = N_HEADS // N_KV_HEADS
ATTN_WIDTH = N_HEADS * HEAD_DIM
KV_WIDTH = N_KV_HEADS * HEAD_DIM
QK_WIDTH = ATTN_WIDTH + KV_WIDTH
FOURIER_GROUP = 64
FOURIER_WIDTH = 512
D_FF = 2816
ROPE_THETA = 10000.0
EPS = 1e-6
DEEPNORM_ALPHA = (2 * DEPTH) ** 0.25

F32 = jnp.float32
BF16 = jnp.bfloat16

SUBLANES = 8
LANES = 128
MXU_WIDTH = 256

ROW_TILE = 256
HALO = SUBLANES
PERM_ROWS = HALO + ROW_TILE
PERM_STRIDE = PERM_ROWS // SUBLANES
EXT_ROWS = PERM_ROWS + HALO
N_SLABS = D_MODEL // LANES
FF_CHUNK = MXU_WIDTH
N_FF_CHUNKS = D_FF // FF_CHUNK
N_MIX_CHUNKS = D_MODEL // FF_CHUNK
PROJ_WIDTH = 2 * FOURIER_WIDTH + QK_WIDTH + KV_WIDTH
N_COND_ROWS = 16
VMEM_LIMIT_BYTES = 48 * 1024 * 1024


def _dot(a, b):
    return jnp.dot(a, b, preferred_element_type=F32)


def _split_bf16(a):
    hi = a.astype(BF16)
    lo = (a - hi.astype(F32)).astype(BF16)
    return hi, lo


def _sigmoid(x):
    return 1.0 / (1.0 + jnp.exp(-x))


def _layer_norm(y, g, b):
    mu = jnp.mean(y, axis=-1, keepdims=True)
    yc = y - mu
    var = jnp.mean(yc * yc, axis=-1, keepdims=True)
    return yc * lax.rsqrt(var + EPS) * g + b


def _params(n_axes):
    return pltpu.CompilerParams(dimension_semantics=("arbitrary",) * n_axes,
                                vmem_limit_bytes=VMEM_LIMIT_BYTES)


def _resident(block, index_map):
    return pl.BlockSpec(block, index_map, pipeline_mode=pl.Buffered(1))


def _mod_kernel(cond_ref, w_ref, b_ref, o_ref):
    c = cond_ref[...]
    s = (c * _sigmoid(c)).astype(BF16)
    o_ref[...] = _dot(s, w_ref[...].astype(BF16)) + b_ref[...]


def _modulation(cond, w_ada, b_ada):
    tn = 1536
    n_out = 6 * D_MODEL
    return pl.pallas_call(
        _mod_kernel,
        grid=(DEPTH, n_out // tn),
        in_specs=[
            pl.BlockSpec((N_COND_ROWS, D_MODEL), lambda l, n: (0, 0)),
            pl.BlockSpec((None, D_MODEL, tn), lambda l, n: (l, 0, n)),
            pl.BlockSpec((None, 1, tn), lambda l, n: (l, 0, n)),
        ],
        out_specs=pl.BlockSpec((None, N_COND_ROWS, tn), lambda l, n: (l, 0, n)),
        out_shape=jax.ShapeDtypeStruct((DEPTH, N_COND_ROWS, n_out), F32),
        compiler_params=_params(2),
        name="modulation",
    )(cond, w_ada, b_ada.reshape(DEPTH, 1, n_out))


def _fold_kernel(w_ref, c_ref, s_ref, o_ref):
    j = pl.program_id(0)
    n_fourier_blocks = FOURIER_WIDTH // MXU_WIDTH

    def dft(m_ref):
        w_hi, w_lo = _split_bf16(w_ref[...])
        m_hi, m_lo = _split_bf16(m_ref[...])
        return _dot(w_hi, m_hi) + _dot(w_hi, m_lo) + _dot(w_lo, m_hi)

    @pl.when(j < n_fourier_blocks)
    def _():
        o_ref[...] = dft(c_ref).astype(BF16)

    @pl.when(jnp.logical_and(j >= n_fourier_blocks, j < 2 * n_fourier_blocks))
    def _():
        o_ref[...] = dft(s_ref).astype(BF16)

    @pl.when(j >= 2 * n_fourier_blocks)
    def _():
        o_ref[...] = w_ref[...].astype(BF16)


def _fold_in_proj(w_in_a, dft_c, dft_s):
    nb = FOURIER_WIDTH // MXU_WIDTH
    return pl.pallas_call(
        _fold_kernel,
        grid=(PROJ_WIDTH // MXU_WIDTH,),
        in_specs=[
            pl.BlockSpec((None, D_MODEL, MXU_WIDTH), lambda j: (0, 0, jnp.where(j < nb, j, j - nb))),
            pl.BlockSpec((MXU_WIDTH, MXU_WIDTH), lambda j: (0, 0)),
            pl.BlockSpec((MXU_WIDTH, MXU_WIDTH), lambda j: (0, 0)),
        ],
        out_specs=pl.BlockSpec((D_MODEL, MXU_WIDTH), lambda j: (0, j)),
        out_shape=jax.ShapeDtypeStruct((D_MODEL, PROJ_WIDTH), BF16),
        compiler_params=_params(1),
        name="fold_in_proj",
    )(w_in_a, dft_c, dft_s)


def _in_proj_kernel(*refs, rope):
    if rope:
        (x_ref, mod_ref, w_ref, hs_ref, g_ref, cos_ref, sa_ref, sb_ref,
         fcs_ref, q_ref, k_ref, v_ref) = refs
    else:
        x_ref, mod_ref, w_ref, hs_ref, g_ref, fcs_ref, q_ref, k_ref, v_ref = refs
    shift = mod_ref[0:1, :]
    scale = mod_ref[1:2, :]
    h = (x_ref[...] * (1.0 + scale) + shift).astype(BF16)
    proj = _dot(h, w_ref[...])
    fcs_ref[...] = proj[:, :2 * FOURIER_WIDTH].astype(BF16)
    qk = proj[:, 2 * FOURIER_WIDTH:2 * FOURIER_WIDTH + QK_WIDTH]
    v = proj[:, 2 * FOURIER_WIDTH + QK_WIDTH:]
    sq_hi, sq_lo = _split_bf16(qk * qk)
    ssq = _dot(sq_hi, hs_ref[...]) + _dot(sq_lo, hs_ref[...])
    y = qk * lax.rsqrt(ssq * (1.0 / HEAD_DIM) + EPS) * g_ref[...]
    if rope:
        parts = []
        for t in range(QK_WIDTH // LANES):
            yt = y[:, t * LANES:(t + 1) * LANES]
            lo = slice(t * LANES, (t + 1) * LANES)
            parts.append(yt * cos_ref[:, lo]
                         + pltpu.roll(yt, LANES - 16, 1) * sa_ref[:, lo]
                         + pltpu.roll(yt, 16, 1) * sb_ref[:, lo])
        y = jnp.concatenate(parts, axis=1)
    q_ref[...] = (y[:, :ATTN_WIDTH] * (HEAD_DIM ** -0.5)).astype(q_ref.dtype)
    k_ref[...] = y[:, ATTN_WIDTH:].astype(k_ref.dtype)
    v_ref[...] = v.astype(v_ref.dtype)


def _in_proj(x, mod4, mod_row, w_cat, head_sum, gains, rope_tabs):
    b, s, _ = x.shape
    rope = rope_tabs is not None
    kv_dtype = BF16 if rope else F32
    row_spec = lambda w: pl.BlockSpec((None, ROW_TILE, w), lambda bi, i: (bi, i, 0))
    in_specs = [
        row_spec(D_MODEL),
        pl.BlockSpec((None, None, 6, D_MODEL), lambda bi, i: (0, mod_row(bi), 0, 0)),
        _resident((D_MODEL, PROJ_WIDTH), lambda bi, i: (0, 0)),
        _resident((QK_WIDTH, QK_WIDTH), lambda bi, i: (0, 0)),
        _resident((1, QK_WIDTH), lambda bi, i: (0, 0)),
    ]
    args = [x, mod4, w_cat, head_sum, gains]
    if rope:
        in_specs += [pl.BlockSpec((ROW_TILE, QK_WIDTH), lambda bi, i: (i, 0))] * 3
        args += list(rope_tabs)
    return pl.pallas_call(
        functools.partial(_in_proj_kernel, rope=rope),
        grid=(b, s // ROW_TILE),
        in_specs=in_specs,
        out_specs=[row_spec(2 * FOURIER_WIDTH), row_spec(ATTN_WIDTH), row_spec(KV_WIDTH), row_spec(KV_WIDTH)],
        out_shape=[
            jax.ShapeDtypeStruct((b, s, 2 * FOURIER_WIDTH), BF16),
            jax.ShapeDtypeStruct((b, s, ATTN_WIDTH), BF16),
            jax.ShapeDtypeStruct((b, s, KV_WIDTH), kv_dtype),
            jax.ShapeDtypeStruct((b, s, KV_WIDTH), kv_dtype),
        ],
        compiler_params=_params(2),
        name="in_proj_rope" if rope else "in_proj",
    )(*args)


def _attn_kernel(*refs, has_cache):
    if has_cache:
        q_ref, k_ref, v_ref, ck_ref, cv_ref, o_ref = refs
    else:
        q_ref, k_ref, v_ref, o_ref = refs
    tq = q_ref.shape[0]
    q = q_ref[...]
    k = k_ref[...].astype(BF16)
    v = v_ref[...].astype(BF16)
    if has_cache:
        ck = ck_ref[...].astype(BF16)
        cv = cv_ref[...].astype(BF16)
    nt = (((1,), (1,)), ((), ()))
    outs = []
    for j in range(N_KV_HEADS):
        kv_cols = slice(j * HEAD_DIM, (j + 1) * HEAD_DIM)
        q4 = jnp.concatenate(
            [q[:, (j * Q_PER_KV + g) * HEAD_DIM:(j * Q_PER_KV + g + 1) * HEAD_DIM] for g in range(Q_PER_KV)],
            axis=0)
        s = lax.dot_general(q4, k[:, kv_cols], nt, preferred_element_type=F32)
        m = jnp.max(s, axis=-1, keepdims=True)
        if has_cache:
            s2 = lax.dot_general(q4, ck[:, kv_cols], nt, preferred_element_type=F32)
            m = jnp.maximum(m, jnp.max(s2, axis=-1, keepdims=True))
        p = jnp.exp(s - m)
        denom = jnp.sum(p, axis=-1, keepdims=True)
        o = _dot(p.astype(BF16), v[:, kv_cols])
        if has_cache:
            p2 = jnp.exp(s2 - m)
            denom = denom + jnp.sum(p2, axis=-1, keepdims=True)
            o = o + _dot(p2.astype(BF16), cv[:, kv_cols])
        o = o / denom
        outs += [o[g * tq:(g + 1) * tq] for g in range(Q_PER_KV)]
    o_ref[...] = jnp.concatenate(outs, axis=1).astype(o_ref.dtype)


def _attention(q, k, v, cache, tq):
    b, s, _ = q.shape
    has_cache = cache is not None
    seq_spec = lambda n: pl.BlockSpec((None, n, KV_WIDTH), lambda bi, i: (bi, 0, 0))
    in_specs = [pl.BlockSpec((None, tq, ATTN_WIDTH), lambda bi, i: (bi, i, 0)), seq_spec(s), seq_spec(s)]
    args = [q, k, v]
    if has_cache:
        past = cache[0].shape[1]
        in_specs += [seq_spec(past), seq_spec(past)]
        args += list(cache)
    return pl.pallas_call(
        functools.partial(_attn_kernel, has_cache=has_cache),
        grid=(b, s // tq),
        in_specs=in_specs,
        out_specs=pl.BlockSpec((None, tq, ATTN_WIDTH), lambda bi, i: (bi, i, 0)),
        out_shape=jax.ShapeDtypeStruct((b, s, ATTN_WIDTH), BF16),
        compiler_params=_params(2),
        name="attention_cached" if has_cache else "attention",
    )(*args)


def _out_proj_kernel(x_ref, fcs_ref, dc_ref, ds_ref, attn_ref, w_ref, mod_ref, lng_ref, lnb_ref, o_ref):
    four = (_dot(dc_ref[...], fcs_ref[:, :FOURIER_WIDTH])
            - _dot(ds_ref[...], fcs_ref[:, FOURIER_WIDTH:]))
    out = (_dot(four.astype(BF16), w_ref[:FOURIER_WIDTH, :])
           + _dot(attn_ref[...], w_ref[FOURIER_WIDTH:, :]))
    gate = mod_ref[2:3, :]
    y = DEEPNORM_ALPHA * x_ref[...] + gate * out
    o_ref[...] = _layer_norm(y, lng_ref[0:1, :], lnb_ref[0:1, :])


def _out_proj(x, fcs, attn, dft_c, dft_s, w_out, mod4, mod_row, ln_g, ln_b, layer):
    b, s, _ = x.shape
    row_spec = lambda w: pl.BlockSpec((None, ROW_TILE, w), lambda bi, i: (bi, i, 0))
    ln_spec = pl.BlockSpec((None, 2, D_MODEL), lambda bi, i: (layer, 0, 0))
    return pl.pallas_call(
        _out_proj_kernel,
        grid=(b, s // ROW_TILE),
        in_specs=[
            row_spec(D_MODEL),
            pl.BlockSpec((None, s, 2 * FOURIER_WIDTH), lambda bi, i: (bi, 0, 0)),
            pl.BlockSpec((ROW_TILE, s), lambda bi, i: (i, 0)),
            pl.BlockSpec((ROW_TILE, s), lambda bi, i: (i, 0)),
            row_spec(ATTN_WIDTH),
            _resident((2 * FOURIER_WIDTH, D_MODEL), lambda bi, i: (0, 0)),
            pl.BlockSpec((None, None, 6, D_MODEL), lambda bi, i: (layer, mod_row(bi), 0, 0)),
            ln_spec, ln_spec,
        ],
        out_specs=row_spec(D_MODEL),
        out_shape=jax.ShapeDtypeStruct((b, s, D_MODEL), F32),
        compiler_params=_params(2),
        name="out_proj",
    )(x, fcs, dft_c, dft_s, attn, w_out, mod4, ln_g, ln_b)


def _load_permuted(slab_ref):
    slabs = [jnp.concatenate([slab_ref[c, pl.ds(v, SUBLANES, stride=PERM_STRIDE), :] for v in range(PERM_STRIDE)],
                             axis=0) for c in range(N_SLABS)]
    return jnp.concatenate(slabs, axis=1)


def _store_unpermuted(y, slab_ref):
    for c in range(N_SLABS):
        for v in range(PERM_STRIDE):
            slab_ref[c, pl.ds(v, SUBLANES, stride=PERM_STRIDE), :] = (
                y[v * SUBLANES:(v + 1) * SUBLANES, c * LANES:(c + 1) * LANES])


def _stage_rows(x_ref, xp_ref, xn_ref, shift, scale, xe_ref, h_ref):
    i = pl.program_id(1)
    for c in range(N_SLABS):
        cols = slice(c * LANES, (c + 1) * LANES)
        xe_ref[c, 0:HALO, :] = xp_ref[:, cols]
        xe_ref[c, HALO:PERM_ROWS, :] = x_ref[:, cols]
    rows = lax.broadcasted_iota(jnp.int32, (PERM_ROWS, 1), 0)
    is_halo = jnp.logical_and(rows % SUBLANES == 0, rows < SUBLANES * HALO)
    keep = jnp.where(jnp.logical_and(is_halo, i == 0), 0.0, 1.0)
    has_next = jnp.where(i < pl.num_programs(1) - 1, 1.0, 0.0)
    mod = lambda t: t * (1.0 + scale) + shift
    h = jnp.concatenate([mod(_load_permuted(xe_ref)) * keep, mod(xn_ref[...]) * has_next], axis=0)
    h_ref[...] = h.astype(BF16)


def _conv3_permuted(t, cw):
    body = t[0:PERM_ROWS]
    first_prev = pltpu.roll(t[PERM_ROWS - SUBLANES:PERM_ROWS], 1, 0)
    prev = jnp.concatenate([first_prev, t[0:PERM_ROWS - SUBLANES]], axis=0)
    sublane = lax.broadcasted_iota(jnp.int32, (SUBLANES, 1), 0)
    last_next = jnp.where(sublane == SUBLANES - 1, t[PERM_ROWS:PERM_ROWS + 1],
                          pltpu.roll(t[0:SUBLANES], SUBLANES - 1, 0))
    nxt = jnp.concatenate([t[SUBLANES:PERM_ROWS], last_next], axis=0)
    return cw[0:1, :] * prev + cw[1:2, :] * body + cw[2:3, :] * nxt


def _finish_rows(xe_ref, acc, gate, lng, lnb, oe_ref, o_ref):
    y = DEEPNORM_ALPHA * _load_permuted(xe_ref) + gate * acc
    _store_unpermuted(_layer_norm(y, lng, lnb), oe_ref)
    for c in range(N_SLABS):
        o_ref[:, c * LANES:(c + 1) * LANES] = oe_ref[c, HALO:PERM_ROWS, :]


def _ffn_kernel(x_ref, xp_ref, xn_ref, mod_ref, wup_ref, cw_ref, wdn_ref, lng_ref, lnb_ref, o_ref,
                xe_ref, h_ref, act_ref, oe_ref):
    _stage_rows(x_ref, xp_ref, xn_ref, mod_ref[3:4, :], mod_ref[4:5, :], xe_ref, h_ref)
    for j in range(N_FF_CHUNKS):
        a_cols = slice(j * FF_CHUNK, (j + 1) * FF_CHUNK)
        g_cols = slice(D_FF + j * FF_CHUNK, D_FF + (j + 1) * FF_CHUNK)
        a = _conv3_permuted(_dot(h_ref[...], wup_ref[:, a_cols]), cw_ref[:, a_cols])
        g = _conv3_permuted(_dot(h_ref[...], wup_ref[:, g_cols]), cw_ref[:, g_cols])
        act_ref[:, a_cols] = (g * _sigmoid(g) * a).astype(BF16)
    acc = _dot(act_ref[...], wdn_ref[...])
    _finish_rows(xe_ref, acc, mod_ref[5:6, :], lng_ref[1:2, :], lnb_ref[1:2, :], oe_ref, o_ref)


def _mixer_kernel(x_ref, xp_ref, xn_ref, mod_ref, win_ref, cw_ref, wout_ref, lng_ref, lnb_ref, o_ref,
                  xe_ref, h_ref, act_ref, oe_ref):
    _stage_rows(x_ref, xp_ref, xn_ref, mod_ref[0:1, :], mod_ref[1:2, :], xe_ref, h_ref)
    for j in range(N_MIX_CHUNKS):
        cols = slice(j * FF_CHUNK, (j + 1) * FF_CHUNK)
        bg, cg, xin = (_dot(h_ref[...], win_ref[:, k * D_MODEL + j * FF_CHUNK:k * D_MODEL + (j + 1) * FF_CHUNK])
                       for k in range(3))
        y = bg[0:PERM_ROWS] * _conv3_permuted(cg * xin, cw_ref[:, cols])
        act_ref[:, cols] = y.astype(BF16)
    acc = _dot(act_ref[...], wout_ref[...])
    _finish_rows(xe_ref, acc, mod_ref[2:3, :], lng_ref[0:1, :], lnb_ref[0:1, :], oe_ref, o_ref)


def _gated_conv_block(kernel_fn, name, x, mod4, mod_row, w1, cw, w2, ln_g, ln_b, layer, w_layer):
    b, s, _ = x.shape
    hidden = w2.shape[1]
    halo_blocks_per_tile = ROW_TILE // HALO
    last_halo_block = s // HALO - 1
    ln_spec = pl.BlockSpec((None, 2, D_MODEL), lambda bi, i: (layer, 0, 0))
    return pl.pallas_call(
        kernel_fn,
        grid=(b, s // ROW_TILE),
        in_specs=[
            pl.BlockSpec((None, ROW_TILE, D_MODEL), lambda bi, i: (bi, i, 0)),
            pl.BlockSpec((None, HALO, D_MODEL),
                         lambda bi, i: (bi, jnp.maximum(i * halo_blocks_per_tile - 1, 0), 0)),
            pl.BlockSpec((None, HALO, D_MODEL),
                         lambda bi, i: (bi, jnp.minimum((i + 1) * halo_blocks_per_tile, last_halo_block), 0)),
            pl.BlockSpec((None, None, 6, D_MODEL), lambda bi, i: (layer, mod_row(bi), 0, 0)),
            _resident((None,) + w1.shape[1:], lambda bi, i: (w_layer, 0, 0)),
            _resident((None,) + cw.shape[1:], lambda bi, i: (w_layer, 0, 0)),
            _resident((None,) + w2.shape[1:], lambda bi, i: (w_layer, 0, 0)),
            ln_spec, ln_spec,
        ],
        out_specs=pl.BlockSpec((None, ROW_TILE, D_MODEL), lambda bi, i: (bi, i, 0)),
        out_shape=jax.ShapeDtypeStruct((b, s, D_MODEL), F32),
        scratch_shapes=[
            pltpu.VMEM((N_SLABS, PERM_ROWS, LANES), F32),
            pltpu.VMEM((EXT_ROWS, D_MODEL), BF16),
            pltpu.VMEM((PERM_ROWS, hidden), BF16),
            pltpu.VMEM((N_SLABS, PERM_ROWS, LANES), F32),
        ],
        compiler_params=_params(2),
        name=name,
    )(x, x, x, mod4, w1, cw, w2, ln_g, ln_b)


def _dft_tables(n, scale):
    jk = np.outer(np.arange(n), np.arange(n)) % n
    ang = 2.0 * np.pi * jk / n
    return (np.cos(ang) * scale).astype(np.float32), (np.sin(ang) * scale).astype(np.float32)


def _channel_dft_blocks():
    c, s = _dft_tables(FOURIER_GROUP, 1.0)
    eye = np.eye(MXU_WIDTH // FOURIER_GROUP, dtype=np.float32)
    return np.kron(eye, c), np.kron(eye, s)


def _rope_tables(n):
    half = HEAD_DIM // 2
    inv = 1.0 / (ROPE_THETA ** (np.arange(0, half, 2, dtype=np.float64) / half))
    pos = np.arange(n)
    d = np.arange(QK_WIDTH) % HEAD_DIM
    coord = np.where(d < half, (pos // GRID_W)[:, None], (pos % GRID_W)[:, None])
    ang = coord * inv[d % (half // 2)][None, :]
    first = (d % half) < (half // 2)
    cos = np.cos(ang)
    sa = np.where(first[None, :], -np.sin(ang), 0.0)
    sb = np.where(first[None, :], 0.0, np.sin(ang))
    return tuple(jnp.asarray(t.astype(np.float32)) for t in (cos, sa, sb))


def _head_sum_matrix():
    return np.kron(np.eye(QK_WIDTH // HEAD_DIM, dtype=np.float32), np.ones((HEAD_DIM, HEAD_DIM), np.float32))


def kernel(x_prompt, x_sample, cache_k, cache_v, c, c_ctx, w_ada, b_ada, ln_g, ln_b, w_in_a, q_norm_g,
           k_norm_g, w_out_a, w_in_c, conv_c, w_out_c, w_up, conv_f, w_down):
    n_prompt, s_prompt, _ = x_prompt.shape
    n_sample, s_sample, _ = x_sample.shape
    past = cache_k.shape[2]

    cond = jnp.concatenate(
        [c_ctx[None, :], c, jnp.zeros((N_COND_ROWS - 1 - n_sample, D_MODEL), F32)], axis=0)
    mod4 = _modulation(cond, w_ada, b_ada).reshape(DEPTH, N_COND_ROWS, 6, D_MODEL)

    chan_c, chan_s = _channel_dft_blocks()
    w_cat = _fold_in_proj(w_in_a, jnp.asarray(chan_c), jnp.asarray(chan_s))
    head_sum = jnp.asarray(_head_sum_matrix()).astype(BF16)
    gains = jnp.concatenate([jnp.tile(q_norm_g[0], N_HEADS), jnp.tile(k_norm_g[0], N_KV_HEADS)])[None, :]
    w_out_bf = w_out_a[0].astype(BF16)

    w_up_bf, w_down_bf = w_up.astype(BF16), w_down.astype(BF16)
    w_in_c_bf, w_out_c_bf = w_in_c.astype(BF16), w_out_c.astype(BF16)

    def run_stream(x, mod_row, cache, tq):
        s = x.shape[1]
        scale = (FOURIER_GROUP * s) ** -0.5
        dft_c, dft_s = (jnp.asarray(t).astype(BF16) for t in _dft_tables(s, scale))
        rope_tabs = _rope_tables(s) if cache is not None else None
        fcs, q, k, v = _in_proj(x, mod4, mod_row, w_cat, head_sum, gains, rope_tabs)
        attn = _attention(q, k, v, cache, tq)
        x = _out_proj(x, fcs, attn, dft_c, dft_s, w_out_bf, mod4, mod_row, ln_g, ln_b, 0)
        x = _gated_conv_block(_ffn_kernel, "conv_ffn", x, mod4, mod_row, w_up_bf, conv_f, w_down_bf,
                              ln_g, ln_b, 0, 0)
        x = _gated_conv_block(_mixer_kernel, "conv_mixer", x, mod4, mod_row, w_in_c_bf, conv_c, w_out_c_bf,
                              ln_g, ln_b, 1, 0)
        x = _gated_conv_block(_ffn_kernel, "conv_ffn", x, mod4, mod_row, w_up_bf, conv_f, w_down_bf,
                              ln_g, ln_b, 1, 1)
        return x, k, v

    y_prompt, k_new, v_new = run_stream(x_prompt, lambda bi: 0, None, ROW_TILE)
    cache = (cache_k[:, 0].reshape(n_sample, past, KV_WIDTH), cache_v[:, 0].reshape(n_sample, past, KV_WIDTH))
    y_sample, _, _ = run_stream(x_sample, lambda bi: bi + 1, cache, 128)

    new_shape = (n_prompt, 1, s_prompt, N_KV_HEADS, HEAD_DIM)
    return y_prompt, y_sample, k_new.reshape(new_shape), v_new.reshape(new_shape)
```

```python
import functools

import numpy as np
import jax
import jax.numpy as jnp
from jax import lax
from jax.experimental import pallas as pl
from jax.experimental.pallas import tpu as pltpu

D_MODEL = 1024
DEPTH = 2
GRID_W = 64
HEAD_DIM = 64
N_HEADS = 8
N_KV_HEADS = 2
Q_PER_KV = N_HEADS // N_KV_HEADS
ATTN_WIDTH = N_HEADS * HEAD_DIM
KV_WIDTH = N_KV_HEADS * HEAD_DIM
QK_WIDTH = ATTN_WIDTH + KV_WIDTH
FOURIER_GROUP = 64
FOURIER_WIDTH = 512
D_FF = 2816
ROPE_THETA = 10000.0
EPS = 1e-6
DEEPNORM_ALPHA = (2 * DEPTH) ** 0.25
LOG2_E = 1.4426950408889634

F32 = jnp.float32
BF16 = jnp.bfloat16

SUBLANES = 8
BF16_SUBLANES = 16
LANES = 128
MXU_WIDTH = 256

ROW_TILE = 256
ATTN_Q_TILE = 256
ATTN_SUB_TILE = 128
HALO = SUBLANES
PERM_ROWS = HALO + ROW_TILE
PERM_STRIDE = PERM_ROWS // SUBLANES
EXT_ROWS = PERM_ROWS + HALO
N_SLABS = D_MODEL // LANES
FF_CHUNK = MXU_WIDTH
N_FF_CHUNKS = D_FF // FF_CHUNK
N_MIX_CHUNKS = D_MODEL // FF_CHUNK
PROJ_WIDTH = 2 * FOURIER_WIDTH + QK_WIDTH + KV_WIDTH
N_COND_ROWS = 16
VMEM_LIMIT_BYTES = 48 * 1024 * 1024


def _dot(a, b):
    return jnp.dot(a, b, preferred_element_type=F32)


def _split_bf16(a):
    hi = a.astype(BF16)
    lo = (a - hi.astype(F32)).astype(BF16)
    return hi, lo


def _sigmoid(x):
    return 1.0 / (1.0 + jnp.exp(-x))


def _layer_norm(y, g, b):
    mu = jnp.mean(y, axis=-1, keepdims=True)
    yc = y - mu
    var = jnp.mean(yc * yc, axis=-1, keepdims=True)
    return yc * lax.rsqrt(var + EPS) * g + b


def _params(n_axes):
    return pltpu.CompilerParams(dimension_semantics=("arbitrary",) * n_axes,
                                vmem_limit_bytes=VMEM_LIMIT_BYTES)


def _resident(block, index_map):
    return pl.BlockSpec(block, index_map, pipeline_mode=pl.Buffered(1))


def _mod_kernel(cond_ref, w_ref, b_ref, o_ref):
    c = cond_ref[...]
    s = (c * _sigmoid(c)).astype(BF16)
    o_ref[...] = _dot(s, w_ref[...].astype(BF16)) + b_ref[...]


def _modulation(cond, w_ada, b_ada):
    tn = 1536
    n_out = 6 * D_MODEL
    return pl.pallas_call(
        _mod_kernel,
        grid=(DEPTH, n_out // tn),
        in_specs=[
            pl.BlockSpec((N_COND_ROWS, D_MODEL), lambda l, n: (0, 0)),
            pl.BlockSpec((None, D_MODEL, tn), lambda l, n: (l, 0, n)),
            pl.BlockSpec((None, 1, tn), lambda l, n: (l, 0, n)),
        ],
        out_specs=pl.BlockSpec((None, N_COND_ROWS, tn), lambda l, n: (l, 0, n)),
        out_shape=jax.ShapeDtypeStruct((DEPTH, N_COND_ROWS, n_out), F32),
        compiler_params=_params(2),
        name="modulation",
    )(cond, w_ada, b_ada.reshape(DEPTH, 1, n_out))


def _fold_kernel(w_ref, c_ref, s_ref, o_ref):
    j = pl.program_id(0)
    n_fourier_blocks = FOURIER_WIDTH // MXU_WIDTH

    def dft(m_ref):
        w_hi, w_lo = _split_bf16(w_ref[...])
        m_hi, m_lo = _split_bf16(m_ref[...])
        return _dot(w_hi, m_hi) + _dot(w_hi, m_lo) + _dot(w_lo, m_hi)

    @pl.when(j < n_fourier_blocks)
    def _():
        o_ref[...] = dft(c_ref).astype(BF16)

    @pl.when(jnp.logical_and(j >= n_fourier_blocks, j < 2 * n_fourier_blocks))
    def _():
        o_ref[...] = dft(s_ref).astype(BF16)

    @pl.when(j >= 2 * n_fourier_blocks)
    def _():
        o_ref[...] = w_ref[...].astype(BF16)


def _fold_in_proj(w_in_a, dft_c, dft_s):
    nb = FOURIER_WIDTH // MXU_WIDTH
    return pl.pallas_call(
        _fold_kernel,
        grid=(PROJ_WIDTH // MXU_WIDTH,),
        in_specs=[
            pl.BlockSpec((None, D_MODEL, MXU_WIDTH), lambda j: (0, 0, jnp.where(j < nb, j, j - nb))),
            pl.BlockSpec((MXU_WIDTH, MXU_WIDTH), lambda j: (0, 0)),
            pl.BlockSpec((MXU_WIDTH, MXU_WIDTH), lambda j: (0, 0)),
        ],
        out_specs=pl.BlockSpec((D_MODEL, MXU_WIDTH), lambda j: (0, j)),
        out_shape=jax.ShapeDtypeStruct((D_MODEL, PROJ_WIDTH), BF16),
        compiler_params=_params(1),
        name="fold_in_proj",
    )(w_in_a, dft_c, dft_s)


def _in_proj_kernel(*refs, rope):
    if rope:
        (x_ref, mod_ref, w_ref, hs_ref, g_ref, cos_ref, sa_ref, sb_ref,
         fcs_ref, q_ref, k_ref, vt_ref) = refs
    else:
        x_ref, mod_ref, w_ref, hs_ref, g_ref, fcs_ref, q_ref, k_ref, vt_ref, v_ref = refs
    shift = mod_ref[0:1, :]
    scale = mod_ref[1:2, :]
    h = (x_ref[...] * (1.0 + scale) + shift).astype(BF16)
    proj = _dot(h, w_ref[...])
    fcs_ref[...] = proj[:, :2 * FOURIER_WIDTH].astype(BF16)
    qk = proj[:, 2 * FOURIER_WIDTH:2 * FOURIER_WIDTH + QK_WIDTH]
    v = proj[:, 2 * FOURIER_WIDTH + QK_WIDTH:]
    sq_hi, sq_lo = _split_bf16(qk * qk)
    ssq = _dot(sq_hi, hs_ref[...]) + _dot(sq_lo, hs_ref[...])
    y = qk * lax.rsqrt(ssq * (1.0 / HEAD_DIM) + EPS) * g_ref[...]
    if rope:
        parts = []
        for t in range(QK_WIDTH // LANES):
            yt = y[:, t * LANES:(t + 1) * LANES]
            lo = slice(t * LANES, (t + 1) * LANES)
            parts.append(yt * cos_ref[:, lo]
                         + pltpu.roll(yt, LANES - 16, 1) * sa_ref[:, lo]
                         + pltpu.roll(yt, 16, 1) * sb_ref[:, lo])
        y = jnp.concatenate(parts, axis=1)
    q_ref[...] = (y[:, :ATTN_WIDTH] * (HEAD_DIM ** -0.5 * LOG2_E)).astype(q_ref.dtype)
    k_ref[...] = y[:, ATTN_WIDTH:].astype(k_ref.dtype)
    vt_ref[...] = v.T.astype(vt_ref.dtype)
    if not rope:
        v_ref[...] = v


def _in_proj(x, mod4, mod_row, w_cat, head_sum, gains, rope_tabs):
    b, s, _ = x.shape
    rope = rope_tabs is not None
    row_spec = lambda w: pl.BlockSpec((None, ROW_TILE, w), lambda bi, i: (bi, i, 0))
    out_specs = [row_spec(2 * FOURIER_WIDTH), row_spec(ATTN_WIDTH), row_spec(KV_WIDTH),
                 pl.BlockSpec((None, KV_WIDTH, ROW_TILE), lambda bi, i: (bi, 0, i))]
    out_shape = [
        jax.ShapeDtypeStruct((b, s, 2 * FOURIER_WIDTH), BF16),
        jax.ShapeDtypeStruct((b, s, ATTN_WIDTH), BF16),
        jax.ShapeDtypeStruct((b, s, KV_WIDTH), BF16 if rope else F32),
        jax.ShapeDtypeStruct((b, KV_WIDTH, s), BF16),
    ]
    if not rope:
        out_specs.append(row_spec(KV_WIDTH))
        out_shape.append(jax.ShapeDtypeStruct((b, s, KV_WIDTH), F32))
    in_specs = [
        row_spec(D_MODEL),
        pl.BlockSpec((None, None, 6, D_MODEL), lambda bi, i: (0, mod_row(bi), 0, 0)),
        _resident((D_MODEL, PROJ_WIDTH), lambda bi, i: (0, 0)),
        _resident((QK_WIDTH, QK_WIDTH), lambda bi, i: (0, 0)),
        _resident((1, QK_WIDTH), lambda bi, i: (0, 0)),
    ]
    args = [x, mod4, w_cat, head_sum, gains]
    if rope:
        in_specs += [pl.BlockSpec((ROW_TILE, QK_WIDTH), lambda bi, i: (i, 0))] * 3
        args += list(rope_tabs)
    return pl.pallas_call(
        functools.partial(_in_proj_kernel, rope=rope),
        grid=(b, s // ROW_TILE),
        in_specs=in_specs,
        out_specs=out_specs,
        out_shape=out_shape,
        compiler_params=_params(2),
        name="in_proj_rope" if rope else "in_proj",
    )(*args)


def _attn_kernel(*refs, has_cache, sub_tile):
    if has_cache:
        q_ref, k_ref, vt_ref, ck_ref, cvt_ref, o_ref = refs
    else:
        q_ref, k_ref, vt_ref, o_ref = refs
    k = k_ref[...].astype(BF16)
    nt = (((1,), (1,)), ((), ()))

    def with_ones_rows(vt):
        return jnp.concatenate([vt.astype(BF16), jnp.ones((BF16_SUBLANES, vt.shape[1]), BF16)], axis=0)

    items = []
    for t in range(q_ref.shape[0] // sub_tile):
        q = q_ref[t * sub_tile:(t + 1) * sub_tile, :]
        for j in range(N_KV_HEADS):
            kv = slice(j * HEAD_DIM, (j + 1) * HEAD_DIM)
            qs = jnp.concatenate(
                [q[:, h * HEAD_DIM:(h + 1) * HEAD_DIM] for h in range(j * Q_PER_KV, (j + 1) * Q_PER_KV)], axis=0)
            st = lax.dot_general(k[:, kv], qs, nt, preferred_element_type=F32)
            st2 = (lax.dot_general(ck_ref[:, kv].astype(BF16), qs, nt, preferred_element_type=F32)
                   if has_cache else None)
            items.append((t, kv, st, st2))

    for t, kv, st, st2 in items:
        m = jnp.max(st, axis=0, keepdims=True)
        if has_cache:
            m = jnp.maximum(m, jnp.max(st2, axis=0, keepdims=True))
        ot = _dot(with_ones_rows(vt_ref[kv, :]), jnp.exp2(st - m).astype(BF16))
        if has_cache:
            ot = ot + _dot(with_ones_rows(cvt_ref[kv, :]), jnp.exp2(st2 - m).astype(BF16))
        o = ot[0:HEAD_DIM] / ot[HEAD_DIM:HEAD_DIM + 1]
        for g in range(0, Q_PER_KV, 2):
            pair = jnp.concatenate([o[:, g * sub_tile:(g + 1) * sub_tile],
                                    o[:, (g + 1) * sub_tile:(g + 2) * sub_tile]], axis=0).T
            lane0 = (kv.start // HEAD_DIM * Q_PER_KV + g) * HEAD_DIM
            o_ref[t * sub_tile:(t + 1) * sub_tile, lane0:lane0 + LANES] = pair.astype(o_ref.dtype)


def _attention(q, k, vt, cache, tq, sub_tile):
    b, s, _ = q.shape
    has_cache = cache is not None
    seq_spec = lambda n: pl.BlockSpec((None, n, KV_WIDTH), lambda bi, i: (bi, 0, 0))
    seq_t_spec = lambda n: pl.BlockSpec((None, KV_WIDTH, n), lambda bi, i: (bi, 0, 0))
    in_specs = [pl.BlockSpec((None, tq, ATTN_WIDTH), lambda bi, i: (bi, i, 0)), seq_spec(s), seq_t_spec(s)]
    args = [q, k, vt]
    if has_cache:
        past = cache[0].shape[1]
        in_specs += [seq_spec(past), seq_t_spec(past)]
        args += list(cache)
    return pl.pallas_call(
        functools.partial(_attn_kernel, has_cache=has_cache, sub_tile=sub_tile),
        grid=(b, s // tq),
        in_specs=in_specs,
        out_specs=pl.BlockSpec((None, tq, ATTN_WIDTH), lambda bi, i: (bi, i, 0)),
        out_shape=jax.ShapeDtypeStruct((b, s, ATTN_WIDTH), BF16),
        compiler_params=_params(2),
        name="attention_cached" if has_cache else "attention",
    )(*args)


def _out_proj_kernel(x_ref, fcs_ref, dc_ref, ds_ref, attn_ref, w_ref, mod_ref, lng_ref, lnb_ref, o_ref):
    four = (_dot(dc_ref[...], fcs_ref[:, :FOURIER_WIDTH])
            - _dot(ds_ref[...], fcs_ref[:, FOURIER_WIDTH:]))
    out = (_dot(four.astype(BF16), w_ref[:FOURIER_WIDTH, :])
           + _dot(attn_ref[...], w_ref[FOURIER_WIDTH:, :]))
    gate = mod_ref[2:3, :]
    y = DEEPNORM_ALPHA * x_ref[...] + gate * out
    o_ref[...] = _layer_norm(y, lng_ref[0:1, :], lnb_ref[0:1, :])


def _out_proj(x, fcs, attn, dft_c, dft_s, w_out, mod4, mod_row, ln_g, ln_b, layer):
    b, s, _ = x.shape
    row_spec = lambda w: pl.BlockSpec((None, ROW_TILE, w), lambda bi, i: (bi, i, 0))
    ln_spec = pl.BlockSpec((None, 2, D_MODEL), lambda bi, i: (layer, 0, 0))
    return pl.pallas_call(
        _out_proj_kernel,
        grid=(b, s // ROW_TILE),
        in_specs=[
            row_spec(D_MODEL),
            pl.BlockSpec((None, s, 2 * FOURIER_WIDTH), lambda bi, i: (bi, 0, 0)),
            pl.BlockSpec((ROW_TILE, s), lambda bi, i: (i, 0)),
            pl.BlockSpec((ROW_TILE, s), lambda bi, i: (i, 0)),
            row_spec(ATTN_WIDTH),
            _resident((2 * FOURIER_WIDTH, D_MODEL), lambda bi, i: (0, 0)),
            pl.BlockSpec((None, None, 6, D_MODEL), lambda bi, i: (layer, mod_row(bi), 0, 0)),
            ln_spec, ln_spec,
        ],
        out_specs=row_spec(D_MODEL),
        out_shape=jax.ShapeDtypeStruct((b, s, D_MODEL), F32),
        compiler_params=_params(2),
        name="out_proj",
    )(x, fcs, dft_c, dft_s, attn, w_out, mod4, ln_g, ln_b)


def _load_permuted(slab_ref):
    slabs = [jnp.concatenate([slab_ref[c, pl.ds(v, SUBLANES, stride=PERM_STRIDE), :] for v in range(PERM_STRIDE)],
                             axis=0) for c in range(N_SLABS)]
    return jnp.concatenate(slabs, axis=1)


def _store_unpermuted(y, slab_ref):
    for c in range(N_SLABS):
        for v in range(PERM_STRIDE):
            slab_ref[c, pl.ds(v, SUBLANES, stride=PERM_STRIDE), :] = (
                y[v * SUBLANES:(v + 1) * SUBLANES, c * LANES:(c + 1) * LANES])


def _stage_rows(x_ref, xp_ref, xn_ref, shift, scale, xe_ref, h_ref):
    i = pl.program_id(1)
    for c in range(N_SLABS):
        cols = slice(c * LANES, (c + 1) * LANES)
        xe_ref[c, 0:HALO, :] = xp_ref[:, cols]
        xe_ref[c, HALO:PERM_ROWS, :] = x_ref[:, cols]
    rows = lax.broadcasted_iota(jnp.int32, (PERM_ROWS, 1), 0)
    is_halo = jnp.logical_and(rows % SUBLANES == 0, rows < SUBLANES * HALO)
    keep = jnp.where(jnp.logical_and(is_halo, i == 0), 0.0, 1.0)
    has_next = jnp.where(i < pl.num_programs(1) - 1, 1.0, 0.0)
    mod = lambda t: t * (1.0 + scale) + shift
    h = jnp.concatenate([mod(_load_permuted(xe_ref)) * keep, mod(xn_ref[...]) * has_next], axis=0)
    h_ref[...] = h.astype(BF16)


def _conv3_permuted(t, cw):
    body = t[0:PERM_ROWS]
    first_prev = pltpu.roll(t[PERM_ROWS - SUBLANES:PERM_ROWS], 1, 0)
    prev = jnp.concatenate([first_prev, t[0:PERM_ROWS - SUBLANES]], axis=0)
    sublane = lax.broadcasted_iota(jnp.int32, (SUBLANES, 1), 0)
    last_next = jnp.where(sublane == SUBLANES - 1, t[PERM_ROWS:PERM_ROWS + 1],
                          pltpu.roll(t[0:SUBLANES], SUBLANES - 1, 0))
    nxt = jnp.concatenate([t[SUBLANES:PERM_ROWS], last_next], axis=0)
    return cw[0:1, :] * prev + cw[1:2, :] * body + cw[2:3, :] * nxt


def _finish_rows(xe_ref, acc, gate, lng, lnb, oe_ref, o_ref):
    y = DEEPNORM_ALPHA * _load_permuted(xe_ref) + gate * acc
    _store_unpermuted(_layer_norm(y, lng, lnb), oe_ref)
    for c in range(N_SLABS):
        o_ref[:, c * LANES:(c + 1) * LANES] = oe_ref[c, HALO:PERM_ROWS, :]


def _ffn_kernel(x_ref, xp_ref, xn_ref, mod_ref, wup_ref, cw_ref, wdn_ref, lng_ref, lnb_ref, o_ref,
                xe_ref, h_ref, act_ref, oe_ref):
    _stage_rows(x_ref, xp_ref, xn_ref, mod_ref[3:4, :], mod_ref[4:5, :], xe_ref, h_ref)
    for j in range(N_FF_CHUNKS):
        a_cols = slice(j * FF_CHUNK, (j + 1) * FF_CHUNK)
        g_cols = slice(D_FF + j * FF_CHUNK, D_FF + (j + 1) * FF_CHUNK)
        a = _conv3_permuted(_dot(h_ref[...], wup_ref[:, a_cols]), cw_ref[:, a_cols])
        g = _conv3_permuted(_dot(h_ref[...], wup_ref[:, g_cols]), cw_ref[:, g_cols])
        act_ref[:, a_cols] = (g * _sigmoid(g) * a).astype(BF16)
    acc = _dot(act_ref[...], wdn_ref[...])
    _finish_rows(xe_ref, acc, mod_ref[5:6, :], lng_ref[1:2, :], lnb_ref[1:2, :], oe_ref, o_ref)


def _mixer_kernel(x_ref, xp_ref, xn_ref, mod_ref, win_ref, cw_ref, wout_ref, lng_ref, lnb_ref, o_ref,
                  xe_ref, h_ref, act_ref, oe_ref):
    _stage_rows(x_ref, xp_ref, xn_ref, mod_ref[0:1, :], mod_ref[1:2, :], xe_ref, h_ref)
    for j in range(N_MIX_CHUNKS):
        cols = slice(j * FF_CHUNK, (j + 1) * FF_CHUNK)
        bg, cg, xin = (_dot(h_ref[...], win_ref[:, k * D_MODEL + j * FF_CHUNK:k * D_MODEL + (j + 1) * FF_CHUNK])
                       for k in range(3))
        y = bg[0:PERM_ROWS] * _conv3_permuted(cg * xin, cw_ref[:, cols])
        act_ref[:, cols] = y.astype(BF16)
    acc = _dot(act_ref[...], wout_ref[...])
    _finish_rows(xe_ref, acc, mod_ref[2:3, :], lng_ref[0:1, :], lnb_ref[0:1, :], oe_ref, o_ref)


def _gated_conv_block(kernel_fn, name, x, mod4, mod_row, w1, cw, w2, ln_g, ln_b, layer, w_layer):
    b, s, _ = x.shape
    hidden = w2.shape[1]
    halo_blocks_per_tile = ROW_TILE // HALO
    last_halo_block = s // HALO - 1
    ln_spec = pl.BlockSpec((None, 2, D_MODEL), lambda bi, i: (layer, 0, 0))
    return pl.pallas_call(
        kernel_fn,
        grid=(b, s // ROW_TILE),
        in_specs=[
            pl.BlockSpec((None, ROW_TILE, D_MODEL), lambda bi, i: (bi, i, 0)),
            pl.BlockSpec((None, HALO, D_MODEL),
                         lambda bi, i: (bi, jnp.maximum(i * halo_blocks_per_tile - 1, 0), 0)),
            pl.BlockSpec((None, HALO, D_MODEL),
                         lambda bi, i: (bi, jnp.minimum((i + 1) * halo_blocks_per_tile, last_halo_block), 0)),
            pl.BlockSpec((None, None, 6, D_MODEL), lambda bi, i: (layer, mod_row(bi), 0, 0)),
            _resident((None,) + w1.shape[1:], lambda bi, i: (w_layer, 0, 0)),
            _resident((None,) + cw.shape[1:], lambda bi, i: (w_layer, 0, 0)),
            _resident((None,) + w2.shape[1:], lambda bi, i: (w_layer, 0, 0)),
            ln_spec, ln_spec,
        ],
        out_specs=pl.BlockSpec((None, ROW_TILE, D_MODEL), lambda bi, i: (bi, i, 0)),
        out_shape=jax.ShapeDtypeStruct((b, s, D_MODEL), F32),
        scratch_shapes=[
            pltpu.VMEM((N_SLABS, PERM_ROWS, LANES), F32),
            pltpu.VMEM((EXT_ROWS, D_MODEL), BF16),
            pltpu.VMEM((PERM_ROWS, hidden), BF16),
            pltpu.VMEM((N_SLABS, PERM_ROWS, LANES), F32),
        ],
        compiler_params=_params(2),
        name=name,
    )(x, x, x, mod4, w1, cw, w2, ln_g, ln_b)


def _dft_tables(n, scale):
    jk = np.outer(np.arange(n), np.arange(n)) % n
    ang = 2.0 * np.pi * jk / n
    return (np.cos(ang) * scale).astype(np.float32), (np.sin(ang) * scale).astype(np.float32)


def _channel_dft_blocks():
    c, s = _dft_tables(FOURIER_GROUP, 1.0)
    eye = np.eye(MXU_WIDTH // FOURIER_GROUP, dtype=np.float32)
    return np.kron(eye, c), np.kron(eye, s)


def _rope_tables(n):
    half = HEAD_DIM // 2
    inv = 1.0 / (ROPE_THETA ** (np.arange(0, half, 2, dtype=np.float64) / half))
    pos = np.arange(n)
    d = np.arange(QK_WIDTH) % HEAD_DIM
    coord = np.where(d < half, (pos // GRID_W)[:, None], (pos % GRID_W)[:, None])
    ang = coord * inv[d % (half // 2)][None, :]
    first = (d % half) < (half // 2)
    cos = np.cos(ang)
    sa = np.where(first[None, :], -np.sin(ang), 0.0)
    sb = np.where(first[None, :], 0.0, np.sin(ang))
    return tuple(jnp.asarray(t.astype(np.float32)) for t in (cos, sa, sb))


def _head_sum_matrix():
    return np.kron(np.eye(QK_WIDTH // HEAD_DIM, dtype=np.float32), np.ones((HEAD_DIM, HEAD_DIM), np.float32))


def kernel(x_prompt, x_sample, cache_k, cache_v, c, c_ctx, w_ada, b_ada, ln_g, ln_b, w_in_a, q_norm_g,
           k_norm_g, w_out_a, w_in_c, conv_c, w_out_c, w_up, conv_f, w_down):
    n_prompt, s_prompt, _ = x_prompt.shape
    n_sample, s_sample, _ = x_sample.shape
    past = cache_k.shape[2]

    cond = jnp.concatenate(
        [c_ctx[None, :], c, jnp.zeros((N_COND_ROWS - 1 - n_sample, D_MODEL), F32)], axis=0)
    mod4 = _modulation(cond, w_ada, b_ada).reshape(DEPTH, N_COND_ROWS, 6, D_MODEL)

    chan_c, chan_s = _channel_dft_blocks()
    w_cat = _fold_in_proj(w_in_a, jnp.asarray(chan_c), jnp.asarray(chan_s))
    head_sum = jnp.asarray(_head_sum_matrix()).astype(BF16)
    gains = jnp.concatenate([jnp.tile(q_norm_g[0], N_HEADS), jnp.tile(k_norm_g[0], N_KV_HEADS)])[None, :]
    w_out_bf = w_out_a[0].astype(BF16)

    w_up_bf, w_down_bf = w_up.astype(BF16), w_down.astype(BF16)
    w_in_c_bf, w_out_c_bf = w_in_c.astype(BF16), w_out_c.astype(BF16)

    def run_stream(x, mod_row, cache):
        s = x.shape[1]
        scale = (FOURIER_GROUP * s) ** -0.5
        dft_c, dft_s = (jnp.asarray(t).astype(BF16) for t in _dft_tables(s, scale))
        rope_tabs = _rope_tables(s) if cache is not None else None
        fcs, q, k, vt, *v = _in_proj(x, mod4, mod_row, w_cat, head_sum, gains, rope_tabs)
        attn = _attention(q, k, vt, cache, ATTN_Q_TILE, ATTN_SUB_TILE)
        x = _out_proj(x, fcs, attn, dft_c, dft_s, w_out_bf, mod4, mod_row, ln_g, ln_b, 0)
        x = _gated_conv_block(_ffn_kernel, "conv_ffn", x, mod4, mod_row, w_up_bf, conv_f, w_down_bf,
                              ln_g, ln_b, 0, 0)
        x = _gated_conv_block(_mixer_kernel, "conv_mixer", x, mod4, mod_row, w_in_c_bf, conv_c, w_out_c_bf,
                              ln_g, ln_b, 1, 0)
        x = _gated_conv_block(_ffn_kernel, "conv_ffn", x, mod4, mod_row, w_up_bf, conv_f, w_down_bf,
                              ln_g, ln_b, 1, 1)
        return x, k, v

    y_prompt, k_new, (v_new,) = run_stream(x_prompt, lambda bi: 0, None)
    cache = (cache_k[:, 0].reshape(n_sample, past, KV_WIDTH),
             cache_v[:, 0].reshape(n_sample, past, KV_WIDTH).transpose(0, 2, 1))
    y_sample, _, _ = run_stream(x_sample, lambda bi: bi + 1, cache)

    new_shape = (n_prompt, 1, s_prompt, N_KV_HEADS, HEAD_DIM)
    return y_prompt, y_sample, k_new.reshape(new_shape), v_new.reshape(new_shape)
```

```python
import functools

import numpy as np
import jax
import jax.numpy as jnp
from jax import lax
from jax.experimental import pallas as pl
from jax.experimental.pallas import tpu as pltpu

D_MODEL = 1024
DEPTH = 2
GRID_W = 64
HEAD_DIM = 64
N_HEADS = 8
N_KV_HEADS = 2
Q_PER_KV = N_HEADS // N_KV_HEADS
ATTN_WIDTH = N_HEADS * HEAD_DIM
KV_WIDTH = N_KV_HEADS * HEAD_DIM
QK_WIDTH = ATTN_WIDTH + KV_WIDTH
FOURIER_GROUP = 64
FOURIER_WIDTH = 512
D_FF = 2816
ROPE_THETA = 10000.0
EPS = 1e-6
DEEPNORM_ALPHA = (2 * DEPTH) ** 0.25
LOG2_E = 1.4426950408889634

F32 = jnp.float32
BF16 = jnp.bfloat16

SUBLANES = 8
BF16_SUBLANES = 16
LANES = 128
MXU_WIDTH = 256

ROW_TILE = 256
ATTN_Q_TILE = 256
ATTN_SUB_TILE = 128
KEY_CHUNK = MXU_WIDTH
HALO = SUBLANES
PERM_ROWS = HALO + ROW_TILE
PERM_STRIDE = PERM_ROWS // SUBLANES
EXT_ROWS = PERM_ROWS + HALO
N_SLABS = D_MODEL // LANES
FF_CHUNK = MXU_WIDTH
N_FF_CHUNKS = D_FF // FF_CHUNK
N_MIX_CHUNKS = D_MODEL // FF_CHUNK
PROJ_WIDTH = 2 * FOURIER_WIDTH + QK_WIDTH + KV_WIDTH
N_COND_ROWS = 16
VMEM_LIMIT_BYTES = 48 * 1024 * 1024


def _dot(a, b):
    return jnp.dot(a, b, preferred_element_type=F32)


def _split_bf16(a):
    hi = a.astype(BF16)
    lo = (a - hi.astype(F32)).astype(BF16)
    return hi, lo


def _sigmoid(x):
    return 1.0 / (1.0 + jnp.exp(-x))


def _layer_norm(y, g, b):
    mu = jnp.mean(y, axis=-1, keepdims=True)
    yc = y - mu
    var = jnp.mean(yc * yc, axis=-1, keepdims=True)
    return yc * lax.rsqrt(var + EPS) * g + b


def _params(n_axes):
    return pltpu.CompilerParams(dimension_semantics=("arbitrary",) * n_axes,
                                vmem_limit_bytes=VMEM_LIMIT_BYTES)


def _resident(block, index_map):
    return pl.BlockSpec(block, index_map, pipeline_mode=pl.Buffered(1))


def _mod_kernel(cond_ref, w_ref, b_ref, o_ref):
    c = cond_ref[...]
    s = (c * _sigmoid(c)).astype(BF16)
    o_ref[...] = _dot(s, w_ref[...].astype(BF16)) + b_ref[...]


def _modulation(cond, w_ada, b_ada):
    tn = 1536
    n_out = 6 * D_MODEL
    return pl.pallas_call(
        _mod_kernel,
        grid=(DEPTH, n_out // tn),
        in_specs=[
            pl.BlockSpec((N_COND_ROWS, D_MODEL), lambda l, n: (0, 0)),
            pl.BlockSpec((None, D_MODEL, tn), lambda l, n: (l, 0, n)),
            pl.BlockSpec((None, 1, tn), lambda l, n: (l, 0, n)),
        ],
        out_specs=pl.BlockSpec((None, N_COND_ROWS, tn), lambda l, n: (l, 0, n)),
        out_shape=jax.ShapeDtypeStruct((DEPTH, N_COND_ROWS, n_out), F32),
        compiler_params=_params(2),
        name="modulation",
    )(cond, w_ada, b_ada.reshape(DEPTH, 1, n_out))


def _fold_kernel(w_ref, c_ref, s_ref, o_ref):
    j = pl.program_id(0)
    n_fourier_blocks = FOURIER_WIDTH // MXU_WIDTH

    def dft(m_ref):
        w_hi, w_lo = _split_bf16(w_ref[...])
        m_hi, m_lo = _split_bf16(m_ref[...])
        return _dot(w_hi, m_hi) + _dot(w_hi, m_lo) + _dot(w_lo, m_hi)

    @pl.when(j < n_fourier_blocks)
    def _():
        o_ref[...] = dft(c_ref).astype(BF16)

    @pl.when(jnp.logical_and(j >= n_fourier_blocks, j < 2 * n_fourier_blocks))
    def _():
        o_ref[...] = dft(s_ref).astype(BF16)

    @pl.when(j >= 2 * n_fourier_blocks)
    def _():
        o_ref[...] = w_ref[...].astype(BF16)


def _fold_in_proj(w_in_a, dft_c, dft_s):
    nb = FOURIER_WIDTH // MXU_WIDTH
    return pl.pallas_call(
        _fold_kernel,
        grid=(PROJ_WIDTH // MXU_WIDTH,),
        in_specs=[
            pl.BlockSpec((None, D_MODEL, MXU_WIDTH), lambda j: (0, 0, jnp.where(j < nb, j, j - nb))),
            pl.BlockSpec((MXU_WIDTH, MXU_WIDTH), lambda j: (0, 0)),
            pl.BlockSpec((MXU_WIDTH, MXU_WIDTH), lambda j: (0, 0)),
        ],
        out_specs=pl.BlockSpec((D_MODEL, MXU_WIDTH), lambda j: (0, j)),
        out_shape=jax.ShapeDtypeStruct((D_MODEL, PROJ_WIDTH), BF16),
        compiler_params=_params(1),
        name="fold_in_proj",
    )(w_in_a, dft_c, dft_s)


def _in_proj_kernel(*refs, rope):
    if rope:
        (x_ref, mod_ref, w_ref, hs_ref, g_ref, cos_ref, sa_ref, sb_ref,
         fcs_ref, q_ref, k_ref, vt_ref) = refs
    else:
        x_ref, mod_ref, w_ref, hs_ref, g_ref, fcs_ref, q_ref, k_ref, vt_ref, v_ref = refs
    shift = mod_ref[0:1, :]
    scale = mod_ref[1:2, :]
    h = (x_ref[...] * (1.0 + scale) + shift).astype(BF16)
    proj = _dot(h, w_ref[...])
    fcs_ref[...] = proj[:, :2 * FOURIER_WIDTH].astype(BF16)
    qk = proj[:, 2 * FOURIER_WIDTH:2 * FOURIER_WIDTH + QK_WIDTH]
    v = proj[:, 2 * FOURIER_WIDTH + QK_WIDTH:]
    sq_hi, sq_lo = _split_bf16(qk * qk)
    ssq = _dot(sq_hi, hs_ref[...]) + _dot(sq_lo, hs_ref[...])
    y = qk * lax.rsqrt(ssq * (1.0 / HEAD_DIM) + EPS) * g_ref[...]
    if rope:
        parts = []
        for t in range(QK_WIDTH // LANES):
            yt = y[:, t * LANES:(t + 1) * LANES]
            lo = slice(t * LANES, (t + 1) * LANES)
            parts.append(yt * cos_ref[:, lo]
                         + pltpu.roll(yt, LANES - 16, 1) * sa_ref[:, lo]
                         + pltpu.roll(yt, 16, 1) * sb_ref[:, lo])
        y = jnp.concatenate(parts, axis=1)
    q_ref[...] = (y[:, :ATTN_WIDTH] * (HEAD_DIM ** -0.5 * LOG2_E)).astype(q_ref.dtype)
    k_ref[...] = y[:, ATTN_WIDTH:].astype(k_ref.dtype)
    vt_ref[...] = v.T.astype(vt_ref.dtype)
    if not rope:
        v_ref[...] = v


def _in_proj(x, mod4, mod_row, w_cat, head_sum, gains, rope_tabs):
    b, s, _ = x.shape
    rope = rope_tabs is not None
    row_spec = lambda w: pl.BlockSpec((None, ROW_TILE, w), lambda bi, i: (bi, i, 0))
    out_specs = [row_spec(2 * FOURIER_WIDTH), row_spec(ATTN_WIDTH), row_spec(KV_WIDTH),
                 pl.BlockSpec((None, KV_WIDTH, ROW_TILE), lambda bi, i: (bi, 0, i))]
    out_shape = [
        jax.ShapeDtypeStruct((b, s, 2 * FOURIER_WIDTH), BF16),
        jax.ShapeDtypeStruct((b, s, ATTN_WIDTH), BF16),
        jax.ShapeDtypeStruct((b, s, KV_WIDTH), BF16 if rope else F32),
        jax.ShapeDtypeStruct((b, KV_WIDTH, s), BF16),
    ]
    if not rope:
        out_specs.append(row_spec(KV_WIDTH))
        out_shape.append(jax.ShapeDtypeStruct((b, s, KV_WIDTH), F32))
    in_specs = [
        row_spec(D_MODEL),
        pl.BlockSpec((None, None, 6, D_MODEL), lambda bi, i: (0, mod_row(bi), 0, 0)),
        _resident((D_MODEL, PROJ_WIDTH), lambda bi, i: (0, 0)),
        _resident((QK_WIDTH, QK_WIDTH), lambda bi, i: (0, 0)),
        _resident((1, QK_WIDTH), lambda bi, i: (0, 0)),
    ]
    args = [x, mod4, w_cat, head_sum, gains]
    if rope:
        in_specs += [pl.BlockSpec((ROW_TILE, QK_WIDTH), lambda bi, i: (i, 0))] * 3
        args += list(rope_tabs)
    return pl.pallas_call(
        functools.partial(_in_proj_kernel, rope=rope),
        grid=(b, s // ROW_TILE),
        in_specs=in_specs,
        out_specs=out_specs,
        out_shape=out_shape,
        compiler_params=_params(2),
        name="in_proj_rope" if rope else "in_proj",
    )(*args)


def _attn_kernel(*refs, has_cache, sub_tile):
    if has_cache:
        q_ref, k_ref, vt_ref, ck_ref, cvt_ref, o_ref = refs
    else:
        q_ref, k_ref, vt_ref, o_ref = refs
    nt = (((1,), (1,)), ((), ()))
    ones_rows = jnp.ones((BF16_SUBLANES, KEY_CHUNK), BF16)

    key_sets = [(k_ref, vt_ref)] + ([(ck_ref, cvt_ref)] if has_cache else [])

    items = []
    for t in range(q_ref.shape[0] // sub_tile):
        q = q_ref[t * sub_tile:(t + 1) * sub_tile, :]
        for j in range(N_KV_HEADS):
            kv = slice(j * HEAD_DIM, (j + 1) * HEAD_DIM)
            qs = jnp.concatenate(
                [q[:, h * HEAD_DIM:(h + 1) * HEAD_DIM] for h in range(j * Q_PER_KV, (j + 1) * Q_PER_KV)], axis=0)
            sts = [lax.dot_general(keys_ref[:, kv].astype(BF16), qs, nt, preferred_element_type=F32)
                   for keys_ref, _ in key_sets]
            items.append((t, j, sts))

    partial = {}
    for t, j, sts in items:
        for st_all, (_, vts_ref) in zip(sts, key_sets):
            for c in range(0, st_all.shape[0], KEY_CHUNK):
                st = st_all[c:c + KEY_CHUNK]
                m = jnp.max(st, axis=0, keepdims=True)
                vt_ext = jnp.concatenate(
                    [vts_ref[j * HEAD_DIM:(j + 1) * HEAD_DIM, c:c + KEY_CHUNK].astype(BF16), ones_rows], axis=0)
                ot = _dot(vt_ext, jnp.exp2(st - m).astype(BF16))
                partial.setdefault((t, j), []).append((m, ot))

    for (t, j), parts in partial.items():
        m_all = functools.reduce(jnp.maximum, [m for m, _ in parts])
        ot = (functools.reduce(jnp.add, [jnp.exp2(m - m_all) * o for m, o in parts])
              if len(parts) > 1 else parts[0][1])
        o = ot[0:HEAD_DIM] / ot[HEAD_DIM:HEAD_DIM + 1]
        for g in range(0, Q_PER_KV, 2):
            pair = jnp.concatenate([o[:, g * sub_tile:(g + 1) * sub_tile],
                                    o[:, (g + 1) * sub_tile:(g + 2) * sub_tile]], axis=0).T
            lane0 = (j * Q_PER_KV + g) * HEAD_DIM
            o_ref[t * sub_tile:(t + 1) * sub_tile, lane0:lane0 + LANES] = pair.astype(o_ref.dtype)


def _attention(q, k, vt, cache, tq, sub_tile):
    b, s, _ = q.shape
    has_cache = cache is not None
    seq_spec = lambda n: pl.BlockSpec((None, n, KV_WIDTH), lambda bi, i: (bi, 0, 0))
    seq_t_spec = lambda n: pl.BlockSpec((None, KV_WIDTH, n), lambda bi, i: (bi, 0, 0))
    in_specs = [pl.BlockSpec((None, tq, ATTN_WIDTH), lambda bi, i: (bi, i, 0)), seq_spec(s), seq_t_spec(s)]
    args = [q, k, vt]
    if has_cache:
        past = cache[0].shape[1]
        in_specs += [seq_spec(past), seq_t_spec(past)]
        args += list(cache)
    return pl.pallas_call(
        functools.partial(_attn_kernel, has_cache=has_cache, sub_tile=sub_tile),
        grid=(b, s // tq),
        in_specs=in_specs,
        out_specs=pl.BlockSpec((None, tq, ATTN_WIDTH), lambda bi, i: (bi, i, 0)),
        out_shape=jax.ShapeDtypeStruct((b, s, ATTN_WIDTH), BF16),
        compiler_params=_params(2),
        name="attention_cached" if has_cache else "attention",
    )(*args)


def _out_proj_kernel(x_ref, fcs_ref, dc_ref, ds_ref, attn_ref, w_ref, mod_ref, lng_ref, lnb_ref, o_ref):
    four = (_dot(dc_ref[...], fcs_ref[:, :FOURIER_WIDTH])
            - _dot(ds_ref[...], fcs_ref[:, FOURIER_WIDTH:]))
    out = (_dot(four.astype(BF16), w_ref[:FOURIER_WIDTH, :])
           + _dot(attn_ref[...], w_ref[FOURIER_WIDTH:, :]))
    gate = mod_ref[2:3, :]
    y = DEEPNORM_ALPHA * x_ref[...] + gate * out
    o_ref[...] = _layer_norm(y, lng_ref[0:1, :], lnb_ref[0:1, :])


def _out_proj(x, fcs, attn, dft_c, dft_s, w_out, mod4, mod_row, ln_g, ln_b, layer):
    b, s, _ = x.shape
    row_spec = lambda w: pl.BlockSpec((None, ROW_TILE, w), lambda bi, i: (bi, i, 0))
    ln_spec = pl.BlockSpec((None, 2, D_MODEL), lambda bi, i: (layer, 0, 0))
    return pl.pallas_call(
        _out_proj_kernel,
        grid=(b, s // ROW_TILE),
        in_specs=[
            row_spec(D_MODEL),
            pl.BlockSpec((None, s, 2 * FOURIER_WIDTH), lambda bi, i: (bi, 0, 0)),
            pl.BlockSpec((ROW_TILE, s), lambda bi, i: (i, 0)),
            pl.BlockSpec((ROW_TILE, s), lambda bi, i: (i, 0)),
            row_spec(ATTN_WIDTH),
            _resident((2 * FOURIER_WIDTH, D_MODEL), lambda bi, i: (0, 0)),
            pl.BlockSpec((None, None, 6, D_MODEL), lambda bi, i: (layer, mod_row(bi), 0, 0)),
            ln_spec, ln_spec,
        ],
        out_specs=row_spec(D_MODEL),
        out_shape=jax.ShapeDtypeStruct((b, s, D_MODEL), F32),
        compiler_params=_params(2),
        name="out_proj",
    )(x, fcs, dft_c, dft_s, attn, w_out, mod4, ln_g, ln_b)


def _load_permuted(slab_ref):
    slabs = [jnp.concatenate([slab_ref[c, pl.ds(v, SUBLANES, stride=PERM_STRIDE), :] for v in range(PERM_STRIDE)],
                             axis=0) for c in range(N_SLABS)]
    return jnp.concatenate(slabs, axis=1)


def _store_unpermuted(y, slab_ref):
    for c in range(N_SLABS):
        for v in range(PERM_STRIDE):
            slab_ref[c, pl.ds(v, SUBLANES, stride=PERM_STRIDE), :] = (
                y[v * SUBLANES:(v + 1) * SUBLANES, c * LANES:(c + 1) * LANES])


def _stage_rows(x_ref, xp_ref, xn_ref, shift, scale, xe_ref, h_ref):
    i = pl.program_id(1)
    for c in range(N_SLABS):
        cols = slice(c * LANES, (c + 1) * LANES)
        xe_ref[c, 0:HALO, :] = xp_ref[:, cols]
        xe_ref[c, HALO:PERM_ROWS, :] = x_ref[:, cols]
    rows = lax.broadcasted_iota(jnp.int32, (PERM_ROWS, 1), 0)
    is_halo = jnp.logical_and(rows % SUBLANES == 0, rows < SUBLANES * HALO)
    keep = jnp.where(jnp.logical_and(is_halo, i == 0), 0.0, 1.0)
    has_next = jnp.where(i < pl.num_programs(1) - 1, 1.0, 0.0)
    mod = lambda t: t * (1.0 + scale) + shift
    h = jnp.concatenate([mod(_load_permuted(xe_ref)) * keep, mod(xn_ref[...]) * has_next], axis=0)
    h_ref[...] = h.astype(BF16)


def _conv3_permuted(t, cw):
    body = t[0:PERM_ROWS]
    first_prev = pltpu.roll(t[PERM_ROWS - SUBLANES:PERM_ROWS], 1, 0)
    prev = jnp.concatenate([first_prev, t[0:PERM_ROWS - SUBLANES]], axis=0)
    sublane = lax.broadcasted_iota(jnp.int32, (SUBLANES, 1), 0)
    last_next = jnp.where(sublane == SUBLANES - 1, t[PERM_ROWS:PERM_ROWS + 1],
                          pltpu.roll(t[0:SUBLANES], SUBLANES - 1, 0))
    nxt = jnp.concatenate([t[SUBLANES:PERM_ROWS], last_next], axis=0)
    return cw[0:1, :] * prev + cw[1:2, :] * body + cw[2:3, :] * nxt


def _finish_rows(xe_ref, acc, gate, lng, lnb, oe_ref, o_ref):
    y = DEEPNORM_ALPHA * _load_permuted(xe_ref) + gate * acc
    _store_unpermuted(_layer_norm(y, lng, lnb), oe_ref)
    for c in range(N_SLABS):
        o_ref[:, c * LANES:(c + 1) * LANES] = oe_ref[c, HALO:PERM_ROWS, :]


def _ffn_kernel(x_ref, xp_ref, xn_ref, mod_ref, wup_ref, cw_ref, wdn_ref, lng_ref, lnb_ref, o_ref,
                xe_ref, h_ref, act_ref, oe_ref):
    _stage_rows(x_ref, xp_ref, xn_ref, mod_ref[3:4, :], mod_ref[4:5, :], xe_ref, h_ref)
    for j in range(N_FF_CHUNKS):
        a_cols = slice(j * FF_CHUNK, (j + 1) * FF_CHUNK)
        g_cols = slice(D_FF + j * FF_CHUNK, D_FF + (j + 1) * FF_CHUNK)
        a = _conv3_permuted(_dot(h_ref[...], wup_ref[:, a_cols]), cw_ref[:, a_cols])
        g = _conv3_permuted(_dot(h_ref[...], wup_ref[:, g_cols]), cw_ref[:, g_cols])
        act_ref[:, a_cols] = (g * _sigmoid(g) * a).astype(BF16)
    acc = _dot(act_ref[...], wdn_ref[...])
    _finish_rows(xe_ref, acc, mod_ref[5:6, :], lng_ref[1:2, :], lnb_ref[1:2, :], oe_ref, o_ref)


def _mixer_kernel(x_ref, xp_ref, xn_ref, mod_ref, win_ref, cw_ref, wout_ref, lng_ref, lnb_ref, o_ref,
                  xe_ref, h_ref, act_ref, oe_ref):
    _stage_rows(x_ref, xp_ref, xn_ref, mod_ref[0:1, :], mod_ref[1:2, :], xe_ref, h_ref)
    for j in range(N_MIX_CHUNKS):
        cols = slice(j * FF_CHUNK, (j + 1) * FF_CHUNK)
        bg, cg, xin = (_dot(h_ref[...], win_ref[:, k * D_MODEL + j * FF_CHUNK:k * D_MODEL + (j + 1) * FF_CHUNK])
                       for k in range(3))
        y = bg[0:PERM_ROWS] * _conv3_permuted(cg * xin, cw_ref[:, cols])
        act_ref[:, cols] = y.astype(BF16)
    acc = _dot(act_ref[...], wout_ref[...])
    _finish_rows(xe_ref, acc, mod_ref[2:3, :], lng_ref[0:1, :], lnb_ref[0:1, :], oe_ref, o_ref)


def _gated_conv_block(kernel_fn, name, x, mod4, mod_row, w1, cw, w2, ln_g, ln_b, layer, w_layer):
    b, s, _ = x.shape
    hidden = w2.shape[1]
    halo_blocks_per_tile = ROW_TILE // HALO
    last_halo_block = s // HALO - 1
    ln_spec = pl.BlockSpec((None, 2, D_MODEL), lambda bi, i: (layer, 0, 0))
    return pl.pallas_call(
        kernel_fn,
        grid=(b, s // ROW_TILE),
        in_specs=[
            pl.BlockSpec((None, ROW_TILE, D_MODEL), lambda bi, i: (bi, i, 0)),
            pl.BlockSpec((None, HALO, D_MODEL),
                         lambda bi, i: (bi, jnp.maximum(i * halo_blocks_per_tile - 1, 0), 0)),
            pl.BlockSpec((None, HALO, D_MODEL),
                         lambda bi, i: (bi, jnp.minimum((i + 1) * halo_blocks_per_tile, last_halo_block), 0)),
            pl.BlockSpec((None, None, 6, D_MODEL), lambda bi, i: (layer, mod_row(bi), 0, 0)),
            _resident((None,) + w1.shape[1:], lambda bi, i: (w_layer, 0, 0)),
            _resident((None,) + cw.shape[1:], lambda bi, i: (w_layer, 0, 0)),
            _resident((None,) + w2.shape[1:], lambda bi, i: (w_layer, 0, 0)),
            ln_spec, ln_spec,
        ],
        out_specs=pl.BlockSpec((None, ROW_TILE, D_MODEL), lambda bi, i: (bi, i, 0)),
        out_shape=jax.ShapeDtypeStruct((b, s, D_MODEL), F32),
        scratch_shapes=[
            pltpu.VMEM((N_SLABS, PERM_ROWS, LANES), F32),
            pltpu.VMEM((EXT_ROWS, D_MODEL), BF16),
            pltpu.VMEM((PERM_ROWS, hidden), BF16),
            pltpu.VMEM((N_SLABS, PERM_ROWS, LANES), F32),
        ],
        compiler_params=_params(2),
        name=name,
    )(x, x, x, mod4, w1, cw, w2, ln_g, ln_b)


def _dft_tables(n, scale):
    jk = np.outer(np.arange(n), np.arange(n)) % n
    ang = 2.0 * np.pi * jk / n
    return (np.cos(ang) * scale).astype(np.float32), (np.sin(ang) * scale).astype(np.float32)


def _channel_dft_blocks():
    c, s = _dft_tables(FOURIER_GROUP, 1.0)
    eye = np.eye(MXU_WIDTH // FOURIER_GROUP, dtype=np.float32)
    return np.kron(eye, c), np.kron(eye, s)


def _rope_tables(n):
    half = HEAD_DIM // 2
    inv = 1.0 / (ROPE_THETA ** (np.arange(0, half, 2, dtype=np.float64) / half))
    pos = np.arange(n)
    d = np.arange(QK_WIDTH) % HEAD_DIM
    coord = np.where(d < half, (pos // GRID_W)[:, None], (pos % GRID_W)[:, None])
    ang = coord * inv[d % (half // 2)][None, :]
    first = (d % half) < (half // 2)
    cos = np.cos(ang)
    sa = np.where(first[None, :], -np.sin(ang), 0.0)
    sb = np.where(first[None, :], 0.0, np.sin(ang))
    return tuple(jnp.asarray(t.astype(np.float32)) for t in (cos, sa, sb))


def _head_sum_matrix():
    return np.kron(np.eye(QK_WIDTH // HEAD_DIM, dtype=np.float32), np.ones((HEAD_DIM, HEAD_DIM), np.float32))


def kernel(x_prompt, x_sample, cache_k, cache_v, c, c_ctx, w_ada, b_ada, ln_g, ln_b, w_in_a, q_norm_g,
           k_norm_g, w_out_a, w_in_c, conv_c, w_out_c, w_up, conv_f, w_down):
    n_prompt, s_prompt, _ = x_prompt.shape
    n_sample, s_sample, _ = x_sample.shape
    past = cache_k.shape[2]

    cond = jnp.concatenate(
        [c_ctx[None, :], c, jnp.zeros((N_COND_ROWS - 1 - n_sample, D_MODEL), F32)], axis=0)
    mod4 = _modulation(cond, w_ada, b_ada).reshape(DEPTH, N_COND_ROWS, 6, D_MODEL)

    chan_c, chan_s = _channel_dft_blocks()
    w_cat = _fold_in_proj(w_in_a, jnp.asarray(chan_c), jnp.asarray(chan_s))
    head_sum = jnp.asarray(_head_sum_matrix()).astype(BF16)
    gains = jnp.concatenate([jnp.tile(q_norm_g[0], N_HEADS), jnp.tile(k_norm_g[0], N_KV_HEADS)])[None, :]
    w_out_bf = w_out_a[0].astype(BF16)

    w_up_bf, w_down_bf = w_up.astype(BF16), w_down.astype(BF16)
    w_in_c_bf, w_out_c_bf = w_in_c.astype(BF16), w_out_c.astype(BF16)

    def run_stream(x, mod_row, cache):
        s = x.shape[1]
        scale = (FOURIER_GROUP * s) ** -0.5
        dft_c, dft_s = (jnp.asarray(t).astype(BF16) for t in _dft_tables(s, scale))
        rope_tabs = _rope_tables(s) if cache is not None else None
        fcs, q, k, vt, *v = _in_proj(x, mod4, mod_row, w_cat, head_sum, gains, rope_tabs)
        attn = _attention(q, k, vt, cache, ATTN_Q_TILE, ATTN_SUB_TILE)
        x = _out_proj(x, fcs, attn, dft_c, dft_s, w_out_bf, mod4, mod_row, ln_g, ln_b, 0)
        x = _gated_conv_block(_ffn_kernel, "conv_ffn", x, mod4, mod_row, w_up_bf, conv_f, w_down_bf,
                              ln_g, ln_b, 0, 0)
        x = _gated_conv_block(_mixer_kernel, "conv_mixer", x, mod4, mod_row, w_in_c_bf, conv_c, w_out_c_bf,
                              ln_g, ln_b, 1, 0)
        x = _gated_conv_block(_ffn_kernel, "conv_ffn", x, mod4, mod_row, w_up_bf, conv_f, w_down_bf,
                              ln_g, ln_b, 1, 1)
        return x, k, v

    y_prompt, k_new, (v_new,) = run_stream(x_prompt, lambda bi: 0, None)
    cache = (cache_k[:, 0].reshape(n_sample, past, KV_WIDTH),
             cache_v[:, 0].reshape(n_sample, past, KV_WIDTH).transpose(0, 2, 1))
    y_sample, _, _ = run_stream(x_sample, lambda bi: bi + 1, cache)

    new_shape = (n_prompt, 1, s_prompt, N_KV_HEADS, HEAD_DIM)
    return y_prompt, y_sample, k_new.reshape(new_shape), v_new.reshape(new_shape)
```

```python
import functools

import numpy as np
import jax
import jax.numpy as jnp
from jax import lax
from jax.experimental import pallas as pl
from jax.experimental.pallas import tpu as pltpu

D_MODEL = 1024
DEPTH = 2
GRID_W = 64
HEAD_DIM = 64
N_HEADS = 8
N_KV_HEADS = 2
Q_PER_KV = N_HEADS // N_KV_HEADS
ATTN_WIDTH = N_HEADS * HEAD_DIM
KV_WIDTH = N_KV_HEADS * HEAD_DIM
QK_WIDTH = ATTN_WIDTH + KV_WIDTH
FOURIER_GROUP = 64
FOURIER_WIDTH = 512
D_FF = 2816
ROPE_THETA = 10000.0
EPS = 1e-6
DEEPNORM_ALPHA = (2 * DEPTH) ** 0.25
LOG2_E = 1.4426950408889634

F32 = jnp.float32
BF16 = jnp.bfloat16

SUBLANES = 8
BF16_SUBLANES = 16
LANES = 128
MXU_WIDTH = 256

ROW_TILE = 256
ATTN_Q_TILE = 256
ATTN_SUB_TILE = 128
KEY_CHUNK = MXU_WIDTH
HALO = SUBLANES
PERM_ROWS = HALO + ROW_TILE
PERM_STRIDE = PERM_ROWS // SUBLANES
EXT_ROWS = PERM_ROWS + HALO
N_SLABS = D_MODEL // LANES
TILES_PER_STEP = 2
FF_CHUNK = MXU_WIDTH
N_FF_CHUNKS = D_FF // FF_CHUNK
N_MIX_CHUNKS = D_MODEL // FF_CHUNK
PROJ_WIDTH = 2 * FOURIER_WIDTH + QK_WIDTH + KV_WIDTH
N_COND_ROWS = 16
VMEM_LIMIT_BYTES = 48 * 1024 * 1024


def _dot(a, b):
    return jnp.dot(a, b, preferred_element_type=F32)


def _split_bf16(a):
    hi = a.astype(BF16)
    lo = (a - hi.astype(F32)).astype(BF16)
    return hi, lo


def _sigmoid(x):
    return 1.0 / (1.0 + jnp.exp(-x))


def _layer_norm(y, g, b):
    mu = jnp.mean(y, axis=-1, keepdims=True)
    yc = y - mu
    var = jnp.mean(yc * yc, axis=-1, keepdims=True)
    return yc * lax.rsqrt(var + EPS) * g + b


def _params(n_axes):
    return pltpu.CompilerParams(dimension_semantics=("arbitrary",) * n_axes,
                                vmem_limit_bytes=VMEM_LIMIT_BYTES)


def _resident(block, index_map):
    return pl.BlockSpec(block, index_map, pipeline_mode=pl.Buffered(1))


def _mod_kernel(cond_ref, w_ref, b_ref, o_ref):
    c = cond_ref[...]
    s = (c * _sigmoid(c)).astype(BF16)
    o_ref[...] = _dot(s, w_ref[...].astype(BF16)) + b_ref[...]


def _modulation(cond, w_ada, b_ada):
    tn = 1536
    n_out = 6 * D_MODEL
    return pl.pallas_call(
        _mod_kernel,
        grid=(DEPTH, n_out // tn),
        in_specs=[
            pl.BlockSpec((N_COND_ROWS, D_MODEL), lambda l, n: (0, 0)),
            pl.BlockSpec((None, D_MODEL, tn), lambda l, n: (l, 0, n)),
            pl.BlockSpec((None, 1, tn), lambda l, n: (l, 0, n)),
        ],
        out_specs=pl.BlockSpec((None, N_COND_ROWS, tn), lambda l, n: (l, 0, n)),
        out_shape=jax.ShapeDtypeStruct((DEPTH, N_COND_ROWS, n_out), F32),
        compiler_params=_params(2),
        name="modulation",
    )(cond, w_ada, b_ada.reshape(DEPTH, 1, n_out))


def _fold_kernel(w_ref, c_ref, s_ref, o_ref):
    j = pl.program_id(0)
    n_fourier_blocks = FOURIER_WIDTH // MXU_WIDTH

    def dft(m_ref):
        w_hi, w_lo = _split_bf16(w_ref[...])
        m_hi, m_lo = _split_bf16(m_ref[...])
        return _dot(w_hi, m_hi) + _dot(w_hi, m_lo) + _dot(w_lo, m_hi)

    @pl.when(j < n_fourier_blocks)
    def _():
        o_ref[...] = dft(c_ref).astype(BF16)

    @pl.when(jnp.logical_and(j >= n_fourier_blocks, j < 2 * n_fourier_blocks))
    def _():
        o_ref[...] = dft(s_ref).astype(BF16)

    @pl.when(j >= 2 * n_fourier_blocks)
    def _():
        o_ref[...] = w_ref[...].astype(BF16)


def _fold_in_proj(w_in_a, dft_c, dft_s):
    nb = FOURIER_WIDTH // MXU_WIDTH
    return pl.pallas_call(
        _fold_kernel,
        grid=(PROJ_WIDTH // MXU_WIDTH,),
        in_specs=[
            pl.BlockSpec((None, D_MODEL, MXU_WIDTH), lambda j: (0, 0, jnp.where(j < nb, j, j - nb))),
            pl.BlockSpec((MXU_WIDTH, MXU_WIDTH), lambda j: (0, 0)),
            pl.BlockSpec((MXU_WIDTH, MXU_WIDTH), lambda j: (0, 0)),
        ],
        out_specs=pl.BlockSpec((D_MODEL, MXU_WIDTH), lambda j: (0, j)),
        out_shape=jax.ShapeDtypeStruct((D_MODEL, PROJ_WIDTH), BF16),
        compiler_params=_params(1),
        name="fold_in_proj",
    )(w_in_a, dft_c, dft_s)


def _in_proj_kernel(*refs, rope):
    if rope:
        (x_ref, mod_ref, w_ref, hs_ref, g_ref, cos_ref, sa_ref, sb_ref,
         fcs_ref, q_ref, k_ref, vt_ref) = refs
    else:
        x_ref, mod_ref, w_ref, hs_ref, g_ref, fcs_ref, q_ref, k_ref, vt_ref, v_ref = refs
    shift = mod_ref[0:1, :]
    scale = mod_ref[1:2, :]
    h = (x_ref[...] * (1.0 + scale) + shift).astype(BF16)
    qk_cols = slice(2 * FOURIER_WIDTH, 2 * FOURIER_WIDTH + QK_WIDTH)
    qk = _dot(h, w_ref[:, qk_cols])
    ssq = _dot((qk * qk).astype(BF16), hs_ref[...])
    fcs_ref[...] = _dot(h, w_ref[:, :2 * FOURIER_WIDTH]).astype(BF16)
    v = _dot(h, w_ref[:, 2 * FOURIER_WIDTH + QK_WIDTH:])
    y = qk * lax.rsqrt(ssq * (1.0 / HEAD_DIM) + EPS) * g_ref[...]
    if rope:
        parts = []
        for t in range(QK_WIDTH // LANES):
            yt = y[:, t * LANES:(t + 1) * LANES]
            lo = slice(t * LANES, (t + 1) * LANES)
            parts.append(yt * cos_ref[:, lo]
                         + pltpu.roll(yt, LANES - 16, 1) * sa_ref[:, lo]
                         + pltpu.roll(yt, 16, 1) * sb_ref[:, lo])
        y = jnp.concatenate(parts, axis=1)
    q_ref[...] = (y[:, :ATTN_WIDTH] * (HEAD_DIM ** -0.5 * LOG2_E)).astype(q_ref.dtype)
    k_ref[...] = y[:, ATTN_WIDTH:].astype(k_ref.dtype)
    vt_ref[...] = v.T.astype(vt_ref.dtype)
    if not rope:
        v_ref[...] = v


def _in_proj(x, mod4, mod_row, w_cat, head_sum, gains, rope_tabs):
    b, s, _ = x.shape
    rope = rope_tabs is not None
    row_spec = lambda w: pl.BlockSpec((None, ROW_TILE, w), lambda bi, i: (bi, i, 0))
    out_specs = [row_spec(2 * FOURIER_WIDTH), row_spec(ATTN_WIDTH), row_spec(KV_WIDTH),
                 pl.BlockSpec((None, KV_WIDTH, ROW_TILE), lambda bi, i: (bi, 0, i))]
    out_shape = [
        jax.ShapeDtypeStruct((b, s, 2 * FOURIER_WIDTH), BF16),
        jax.ShapeDtypeStruct((b, s, ATTN_WIDTH), BF16),
        jax.ShapeDtypeStruct((b, s, KV_WIDTH), BF16 if rope else F32),
        jax.ShapeDtypeStruct((b, KV_WIDTH, s), BF16),
    ]
    if not rope:
        out_specs.append(row_spec(KV_WIDTH))
        out_shape.append(jax.ShapeDtypeStruct((b, s, KV_WIDTH), F32))
    in_specs = [
        row_spec(D_MODEL),
        pl.BlockSpec((None, None, 6, D_MODEL), lambda bi, i: (0, mod_row(bi), 0, 0)),
        _resident((D_MODEL, PROJ_WIDTH), lambda bi, i: (0, 0)),
        _resident((QK_WIDTH, QK_WIDTH), lambda bi, i: (0, 0)),
        _resident((1, QK_WIDTH), lambda bi, i: (0, 0)),
    ]
    args = [x, mod4, w_cat, head_sum, gains]
    if rope:
        in_specs += [pl.BlockSpec((ROW_TILE, QK_WIDTH), lambda bi, i: (i, 0))] * 3
        args += list(rope_tabs)
    return pl.pallas_call(
        functools.partial(_in_proj_kernel, rope=rope),
        grid=(b, s // ROW_TILE),
        in_specs=in_specs,
        out_specs=out_specs,
        out_shape=out_shape,
        compiler_params=_params(2),
        name="in_proj_rope" if rope else "in_proj",
    )(*args)


def _attn_kernel(*refs, has_cache, sub_tile):
    if has_cache:
        q_ref, k_ref, vt_ref, ck_ref, cvt_ref, o_ref = refs
    else:
        q_ref, k_ref, vt_ref, o_ref = refs
    nt = (((1,), (1,)), ((), ()))
    ones_rows = jnp.ones((BF16_SUBLANES, KEY_CHUNK), BF16)

    key_sets = [(k_ref, vt_ref)] + ([(ck_ref, cvt_ref)] if has_cache else [])

    items = []
    for t in range(q_ref.shape[0] // sub_tile):
        q = q_ref[t * sub_tile:(t + 1) * sub_tile, :]
        for j in range(N_KV_HEADS):
            kv = slice(j * HEAD_DIM, (j + 1) * HEAD_DIM)
            qs = jnp.concatenate(
                [q[:, h * HEAD_DIM:(h + 1) * HEAD_DIM] for h in range(j * Q_PER_KV, (j + 1) * Q_PER_KV)], axis=0)
            sts = [lax.dot_general(keys_ref[:, kv].astype(BF16), qs, nt, preferred_element_type=F32)
                   for keys_ref, _ in key_sets]
            items.append((t, j, sts))

    partial = {}
    for t, j, sts in items:
        for st_all, (_, vts_ref) in zip(sts, key_sets):
            for c in range(0, st_all.shape[0], KEY_CHUNK):
                st = st_all[c:c + KEY_CHUNK]
                m = jnp.max(st, axis=0, keepdims=True)
                vt_ext = jnp.concatenate(
                    [vts_ref[j * HEAD_DIM:(j + 1) * HEAD_DIM, c:c + KEY_CHUNK].astype(BF16), ones_rows], axis=0)
                ot = _dot(vt_ext, jnp.exp2(st - m).astype(BF16))
                partial.setdefault((t, j), []).append((m, ot))

    for (t, j), parts in partial.items():
        m_all = functools.reduce(jnp.maximum, [m for m, _ in parts])
        ot = (functools.reduce(jnp.add, [jnp.exp2(m - m_all) * o for m, o in parts])
              if len(parts) > 1 else parts[0][1])
        o = ot[0:HEAD_DIM] / ot[HEAD_DIM:HEAD_DIM + 1]
        for g in range(0, Q_PER_KV, 2):
            pair = jnp.concatenate([o[:, g * sub_tile:(g + 1) * sub_tile],
                                    o[:, (g + 1) * sub_tile:(g + 2) * sub_tile]], axis=0).T
            lane0 = (j * Q_PER_KV + g) * HEAD_DIM
            o_ref[t * sub_tile:(t + 1) * sub_tile, lane0:lane0 + LANES] = pair.astype(o_ref.dtype)


def _attention(q, k, vt, cache, tq, sub_tile):
    b, s, _ = q.shape
    has_cache = cache is not None
    seq_spec = lambda n: pl.BlockSpec((None, n, KV_WIDTH), lambda bi, i: (bi, 0, 0))
    seq_t_spec = lambda n: pl.BlockSpec((None, KV_WIDTH, n), lambda bi, i: (bi, 0, 0))
    in_specs = [pl.BlockSpec((None, tq, ATTN_WIDTH), lambda bi, i: (bi, i, 0)), seq_spec(s), seq_t_spec(s)]
    args = [q, k, vt]
    if has_cache:
        past = cache[0].shape[1]
        in_specs += [seq_spec(past), seq_t_spec(past)]
        args += list(cache)
    return pl.pallas_call(
        functools.partial(_attn_kernel, has_cache=has_cache, sub_tile=sub_tile),
        grid=(b, s // tq),
        in_specs=in_specs,
        out_specs=pl.BlockSpec((None, tq, ATTN_WIDTH), lambda bi, i: (bi, i, 0)),
        out_shape=jax.ShapeDtypeStruct((b, s, ATTN_WIDTH), BF16),
        compiler_params=_params(2),
        name="attention_cached" if has_cache else "attention",
    )(*args)


def _out_proj_kernel(x_ref, fcs_ref, dc_ref, ds_ref, attn_ref, w_ref, mod_ref, lng_ref, lnb_ref, o_ref):
    four = (_dot(dc_ref[...], fcs_ref[:, :FOURIER_WIDTH])
            - _dot(ds_ref[...], fcs_ref[:, FOURIER_WIDTH:]))
    out = (_dot(four.astype(BF16), w_ref[:FOURIER_WIDTH, :])
           + _dot(attn_ref[...], w_ref[FOURIER_WIDTH:, :]))
    gate = mod_ref[2:3, :]
    y = DEEPNORM_ALPHA * x_ref[...] + gate * out
    o_ref[...] = _layer_norm(y, lng_ref[0:1, :], lnb_ref[0:1, :])


def _out_proj(x, fcs, attn, dft_c, dft_s, w_out, mod4, mod_row, ln_g, ln_b, layer):
    b, s, _ = x.shape
    row_spec = lambda w: pl.BlockSpec((None, ROW_TILE, w), lambda bi, i: (bi, i, 0))
    ln_spec = pl.BlockSpec((None, 2, D_MODEL), lambda bi, i: (layer, 0, 0))
    return pl.pallas_call(
        _out_proj_kernel,
        grid=(b, s // ROW_TILE),
        in_specs=[
            row_spec(D_MODEL),
            pl.BlockSpec((None, s, 2 * FOURIER_WIDTH), lambda bi, i: (bi, 0, 0)),
            pl.BlockSpec((ROW_TILE, s), lambda bi, i: (i, 0)),
            pl.BlockSpec((ROW_TILE, s), lambda bi, i: (i, 0)),
            row_spec(ATTN_WIDTH),
            _resident((2 * FOURIER_WIDTH, D_MODEL), lambda bi, i: (0, 0)),
            pl.BlockSpec((None, None, 6, D_MODEL), lambda bi, i: (layer, mod_row(bi), 0, 0)),
            ln_spec, ln_spec,
        ],
        out_specs=row_spec(D_MODEL),
        out_shape=jax.ShapeDtypeStruct((b, s, D_MODEL), F32),
        compiler_params=_params(2),
        name="out_proj",
    )(x, fcs, dft_c, dft_s, attn, w_out, mod4, ln_g, ln_b)


def _load_permuted(slab_ref):
    slabs = [jnp.concatenate([slab_ref[c, pl.ds(v, SUBLANES, stride=PERM_STRIDE), :] for v in range(PERM_STRIDE)],
                             axis=0) for c in range(N_SLABS)]
    return jnp.concatenate(slabs, axis=1)


def _store_unpermuted(y, slab_ref):
    for c in range(N_SLABS):
        for v in range(PERM_STRIDE):
            slab_ref[c, pl.ds(v, SUBLANES, stride=PERM_STRIDE), :] = (
                y[v * SUBLANES:(v + 1) * SUBLANES, c * LANES:(c + 1) * LANES])


def _stage_rows(u, tiles_per_seq, x_ref, xp_ref, xn_ref, shift, scale, xe_ref, h_ref):
    lo = u * ROW_TILE
    for c in range(N_SLABS):
        cols = slice(c * LANES, (c + 1) * LANES)
        xe_ref[u, c, 0:HALO, :] = xp_ref[:, cols] if u == 0 else x_ref[lo - HALO:lo, cols]
        xe_ref[u, c, HALO:PERM_ROWS, :] = x_ref[lo:lo + ROW_TILE, cols]
    nxt = xn_ref[...] if u == TILES_PER_STEP - 1 else x_ref[lo + ROW_TILE:lo + ROW_TILE + HALO, :]
    if tiles_per_seq == 1:
        has_prev = has_next = 0.0
    else:
        tile = pl.program_id(0) * TILES_PER_STEP + u
        has_prev = jnp.where(lax.rem(tile, tiles_per_seq) != 0, 1.0, 0.0)
        has_next = jnp.where(lax.rem(tile + 1, tiles_per_seq) != 0, 1.0, 0.0)
    rows = lax.broadcasted_iota(jnp.int32, (PERM_ROWS, 1), 0)
    is_halo = jnp.logical_and(rows % SUBLANES == 0, rows < SUBLANES * HALO)
    keep = jnp.where(is_halo, has_prev, 1.0)
    mod = lambda t: t * (1.0 + scale) + shift
    h = jnp.concatenate([mod(_load_permuted(xe_ref.at[u])) * keep, mod(nxt) * has_next], axis=0)
    h_ref[u] = h.astype(BF16)


def _conv3_permuted(t, cw):
    body = t[0:PERM_ROWS]
    first_prev = pltpu.roll(t[PERM_ROWS - SUBLANES:PERM_ROWS], 1, 0)
    prev = jnp.concatenate([first_prev, t[0:PERM_ROWS - SUBLANES]], axis=0)
    sublane = lax.broadcasted_iota(jnp.int32, (SUBLANES, 1), 0)
    last_next = jnp.where(sublane == SUBLANES - 1, t[PERM_ROWS:PERM_ROWS + 1],
                          pltpu.roll(t[0:SUBLANES], SUBLANES - 1, 0))
    nxt = jnp.concatenate([t[SUBLANES:PERM_ROWS], last_next], axis=0)
    return cw[0:1, :] * prev + cw[1:2, :] * body + cw[2:3, :] * nxt


def _finish_rows(u, xe_ref, acc, gate, lng, lnb, oe_ref, o_ref):
    y = DEEPNORM_ALPHA * _load_permuted(xe_ref.at[u]) + gate * acc
    _store_unpermuted(_layer_norm(y, lng, lnb), oe_ref.at[u])
    for c in range(N_SLABS):
        o_ref[u * ROW_TILE:(u + 1) * ROW_TILE, c * LANES:(c + 1) * LANES] = oe_ref[u, c, HALO:PERM_ROWS, :]


def _ffn_kernel(x_ref, xp_ref, xn_ref, mod_ref, wup_ref, cw_ref, wdn_ref, lng_ref, lnb_ref, o_ref,
                xe_ref, h_ref, act_ref, oe_ref, *, tiles_per_seq):
    for u in range(TILES_PER_STEP):
        _stage_rows(u, tiles_per_seq, x_ref, xp_ref, xn_ref, mod_ref[3:4, :], mod_ref[4:5, :], xe_ref, h_ref)
    for u in range(TILES_PER_STEP):
        for j in range(N_FF_CHUNKS):
            a_cols = slice(j * FF_CHUNK, (j + 1) * FF_CHUNK)
            g_cols = slice(D_FF + j * FF_CHUNK, D_FF + (j + 1) * FF_CHUNK)
            a = _conv3_permuted(_dot(h_ref[u], wup_ref[:, a_cols]), cw_ref[:, a_cols])
            g = _conv3_permuted(_dot(h_ref[u], wup_ref[:, g_cols]), cw_ref[:, g_cols])
            act_ref[u, :, a_cols] = (g * _sigmoid(g) * a).astype(BF16)
        acc = _dot(act_ref[u], wdn_ref[...])
        _finish_rows(u, xe_ref, acc, mod_ref[5:6, :], lng_ref[1:2, :], lnb_ref[1:2, :], oe_ref, o_ref)


def _mixer_kernel(x_ref, xp_ref, xn_ref, mod_ref, win_ref, cw_ref, wout_ref, lng_ref, lnb_ref, o_ref,
                  xe_ref, h_ref, act_ref, oe_ref, *, tiles_per_seq):
    for u in range(TILES_PER_STEP):
        _stage_rows(u, tiles_per_seq, x_ref, xp_ref, xn_ref, mod_ref[0:1, :], mod_ref[1:2, :], xe_ref, h_ref)
    for u in range(TILES_PER_STEP):
        for j in range(N_MIX_CHUNKS):
            cols = slice(j * FF_CHUNK, (j + 1) * FF_CHUNK)
            bg, cg, xin = (_dot(h_ref[u], win_ref[:, k * D_MODEL + j * FF_CHUNK:k * D_MODEL + (j + 1) * FF_CHUNK])
                           for k in range(3))
            y = bg[0:PERM_ROWS] * _conv3_permuted(cg * xin, cw_ref[:, cols])
            act_ref[u, :, cols] = y.astype(BF16)
        acc = _dot(act_ref[u], wout_ref[...])
        _finish_rows(u, xe_ref, acc, mod_ref[2:3, :], lng_ref[0:1, :], lnb_ref[0:1, :], oe_ref, o_ref)


def _gated_conv_block(kernel_fn, name, x, mod4, mod_row, w1, cw, w2, ln_g, ln_b, layer, w_layer):
    b, s, _ = x.shape
    hidden = w2.shape[1]
    n_rows = b * s
    step_rows = TILES_PER_STEP * ROW_TILE
    assert s % ROW_TILE == 0 and (s % step_rows == 0 or step_rows % s == 0)
    halo_blocks_per_step = step_rows // HALO
    last_halo_block = n_rows // HALO - 1
    ln_spec = pl.BlockSpec((None, 2, D_MODEL), lambda i: (layer, 0, 0))
    out = pl.pallas_call(
        functools.partial(kernel_fn, tiles_per_seq=s // ROW_TILE),
        grid=(n_rows // step_rows,),
        in_specs=[
            pl.BlockSpec((step_rows, D_MODEL), lambda i: (i, 0)),
            pl.BlockSpec((HALO, D_MODEL), lambda i: (jnp.maximum(i * halo_blocks_per_step - 1, 0), 0)),
            pl.BlockSpec((HALO, D_MODEL),
                         lambda i: (jnp.minimum((i + 1) * halo_blocks_per_step, last_halo_block), 0)),
            pl.BlockSpec((None, None, 6, D_MODEL), lambda i: (layer, mod_row(i * step_rows // s), 0, 0)),
            _resident((None,) + w1.shape[1:], lambda i: (w_layer, 0, 0)),
            _resident((None,) + cw.shape[1:], lambda i: (w_layer, 0, 0)),
            _resident((None,) + w2.shape[1:], lambda i: (w_layer, 0, 0)),
            ln_spec, ln_spec,
        ],
        out_specs=pl.BlockSpec((step_rows, D_MODEL), lambda i: (i, 0)),
        out_shape=jax.ShapeDtypeStruct((n_rows, D_MODEL), F32),
        scratch_shapes=[
            pltpu.VMEM((TILES_PER_STEP, N_SLABS, PERM_ROWS, LANES), F32),
            pltpu.VMEM((TILES_PER_STEP, EXT_ROWS, D_MODEL), BF16),
            pltpu.VMEM((TILES_PER_STEP, PERM_ROWS, hidden), BF16),
            pltpu.VMEM((TILES_PER_STEP, N_SLABS, PERM_ROWS, LANES), F32),
        ],
        compiler_params=_params(1),
        name=name,
    )(*([x.reshape(n_rows, D_MODEL)] * 3), mod4, w1, cw, w2, ln_g, ln_b)
    return out.reshape(b, s, D_MODEL)


def _dft_tables(n, scale):
    jk = np.outer(np.arange(n), np.arange(n)) % n
    ang = 2.0 * np.pi * jk / n
    return (np.cos(ang) * scale).astype(np.float32), (np.sin(ang) * scale).astype(np.float32)


def _channel_dft_blocks():
    c, s = _dft_tables(FOURIER_GROUP, 1.0)
    eye = np.eye(MXU_WIDTH // FOURIER_GROUP, dtype=np.float32)
    return np.kron(eye, c), np.kron(eye, s)


def _rope_tables(n):
    half = HEAD_DIM // 2
    inv = 1.0 / (ROPE_THETA ** (np.arange(0, half, 2, dtype=np.float64) / half))
    pos = np.arange(n)
    d = np.arange(QK_WIDTH) % HEAD_DIM
    coord = np.where(d < half, (pos // GRID_W)[:, None], (pos % GRID_W)[:, None])
    ang = coord * inv[d % (half // 2)][None, :]
    first = (d % half) < (half // 2)
    cos = np.cos(ang)
    sa = np.where(first[None, :], -np.sin(ang), 0.0)
    sb = np.where(first[None, :], 0.0, np.sin(ang))
    return tuple(jnp.asarray(t.astype(np.float32)) for t in (cos, sa, sb))


def _head_sum_matrix():
    return np.kron(np.eye(QK_WIDTH // HEAD_DIM, dtype=np.float32), np.ones((HEAD_DIM, HEAD_DIM), np.float32))


def kernel(x_prompt, x_sample, cache_k, cache_v, c, c_ctx, w_ada, b_ada, ln_g, ln_b, w_in_a, q_norm_g,
           k_norm_g, w_out_a, w_in_c, conv_c, w_out_c, w_up, conv_f, w_down):
    n_prompt, s_prompt, _ = x_prompt.shape
    n_sample, s_sample, _ = x_sample.shape
    past = cache_k.shape[2]

    cond = jnp.concatenate(
        [c_ctx[None, :], c, jnp.zeros((N_COND_ROWS - 1 - n_sample, D_MODEL), F32)], axis=0)
    mod4 = _modulation(cond, w_ada, b_ada).reshape(DEPTH, N_COND_ROWS, 6, D_MODEL)

    chan_c, chan_s = _channel_dft_blocks()
    w_cat = _fold_in_proj(w_in_a, jnp.asarray(chan_c), jnp.asarray(chan_s))
    head_sum = jnp.asarray(_head_sum_matrix()).astype(BF16)
    gains = jnp.concatenate([jnp.tile(q_norm_g[0], N_HEADS), jnp.tile(k_norm_g[0], N_KV_HEADS)])[None, :]
    w_out_bf = w_out_a[0].astype(BF16)

    w_up_bf, w_down_bf = w_up.astype(BF16), w_down.astype(BF16)
    w_in_c_bf, w_out_c_bf = w_in_c.astype(BF16), w_out_c.astype(BF16)

    def run_stream(x, mod_row, cache):
        s = x.shape[1]
        scale = (FOURIER_GROUP * s) ** -0.5
        dft_c, dft_s = (jnp.asarray(t).astype(BF16) for t in _dft_tables(s, scale))
        rope_tabs = _rope_tables(s) if cache is not None else None
        fcs, q, k, vt, *v = _in_proj(x, mod4, mod_row, w_cat, head_sum, gains, rope_tabs)
        attn = _attention(q, k, vt, cache, ATTN_Q_TILE, ATTN_SUB_TILE)
        x = _out_proj(x, fcs, attn, dft_c, dft_s, w_out_bf, mod4, mod_row, ln_g, ln_b, 0)
        x = _gated_conv_block(_ffn_kernel, "conv_ffn", x, mod4, mod_row, w_up_bf, conv_f, w_down_bf,
                              ln_g, ln_b, 0, 0)
        x = _gated_conv_block(_mixer_kernel, "conv_mixer", x, mod4, mod_row, w_in_c_bf, conv_c, w_out_c_bf,
                              ln_g, ln_b, 1, 0)
        x = _gated_conv_block(_ffn_kernel, "conv_ffn", x, mod4, mod_row, w_up_bf, conv_f, w_down_bf,
                              ln_g, ln_b, 1, 1)
        return x, k, v

    y_prompt, k_new, (v_new,) = run_stream(x_prompt, lambda bi: 0, None)
    cache = (cache_k[:, 0].reshape(n_sample, past, KV_WIDTH),
             cache_v[:, 0].reshape(n_sample, past, KV_WIDTH).transpose(0, 2, 1))
    y_sample, _, _ = run_stream(x_sample, lambda bi: bi + 1, cache)

    new_shape = (n_prompt, 1, s_prompt, N_KV_HEADS, HEAD_DIM)
    return y_prompt, y_sample, k_new.reshape(new_shape), v_new.reshape(new_shape)
```

```python
import functools

import numpy as np
import jax
import jax.numpy as jnp
from jax import lax
from jax.experimental import pallas as pl
from jax.experimental.pallas import tpu as pltpu

D_MODEL = 1024
DEPTH = 2
GRID_W = 64
HEAD_DIM = 64
N_HEADS = 8
N_KV_HEADS = 2
Q_PER_KV = N_HEADS // N_KV_HEADS
ATTN_WIDTH = N_HEADS * HEAD_DIM
KV_WIDTH = N_KV_HEADS * HEAD_DIM
QK_WIDTH = ATTN_WIDTH + KV_WIDTH
FOURIER_GROUP = 64
FOURIER_WIDTH = 512
D_FF = 2816
ROPE_THETA = 10000.0
EPS = 1e-6
DEEPNORM_ALPHA = (2 * DEPTH) ** 0.25
LOG2_E = 1.4426950408889634

F32 = jnp.float32
BF16 = jnp.bfloat16

SUBLANES = 8
BF16_SUBLANES = 16
LANES = 128
MXU_WIDTH = 256

ROW_TILE = 256
ATTN_Q_TILE = 256
ATTN_SUB_TILE = 128
KEY_CHUNK = MXU_WIDTH
HALO = SUBLANES
PERM_ROWS = HALO + ROW_TILE
PERM_STRIDE = PERM_ROWS // SUBLANES
EXT_ROWS = PERM_ROWS + HALO
N_SLABS = D_MODEL // LANES
TILES_PER_STEP = 2
FF_CHUNK = MXU_WIDTH
N_FF_CHUNKS = D_FF // FF_CHUNK
N_MIX_CHUNKS = D_MODEL // FF_CHUNK
PROJ_WIDTH = 2 * FOURIER_WIDTH + QK_WIDTH + KV_WIDTH
N_COND_ROWS = 16
VMEM_LIMIT_BYTES = 48 * 1024 * 1024


def _dot(a, b):
    return jnp.dot(a, b, preferred_element_type=F32)


def _split_bf16(a):
    hi = a.astype(BF16)
    lo = (a - hi.astype(F32)).astype(BF16)
    return hi, lo


def _sigmoid(x):
    return 1.0 / (1.0 + jnp.exp(-x))


def _layer_norm(y, g, b):
    mu = jnp.mean(y, axis=-1, keepdims=True)
    yc = y - mu
    var = jnp.mean(yc * yc, axis=-1, keepdims=True)
    return yc * lax.rsqrt(var + EPS) * g + b


def _params(n_axes):
    return pltpu.CompilerParams(dimension_semantics=("arbitrary",) * n_axes,
                                vmem_limit_bytes=VMEM_LIMIT_BYTES)


def _resident(block, index_map):
    return pl.BlockSpec(block, index_map, pipeline_mode=pl.Buffered(1))


def _mod_kernel(cond_ref, w_ref, b_ref, o_ref):
    c = cond_ref[...]
    s = (c * _sigmoid(c)).astype(BF16)
    o_ref[...] = _dot(s, w_ref[...].astype(BF16)) + b_ref[...]


def _modulation(cond, w_ada, b_ada):
    tn = 1536
    n_out = 6 * D_MODEL
    return pl.pallas_call(
        _mod_kernel,
        grid=(DEPTH, n_out // tn),
        in_specs=[
            pl.BlockSpec((N_COND_ROWS, D_MODEL), lambda l, n: (0, 0)),
            pl.BlockSpec((None, D_MODEL, tn), lambda l, n: (l, 0, n)),
            pl.BlockSpec((None, 1, tn), lambda l, n: (l, 0, n)),
        ],
        out_specs=pl.BlockSpec((None, N_COND_ROWS, tn), lambda l, n: (l, 0, n)),
        out_shape=jax.ShapeDtypeStruct((DEPTH, N_COND_ROWS, n_out), F32),
        compiler_params=_params(2),
        name="modulation",
    )(cond, w_ada, b_ada.reshape(DEPTH, 1, n_out))


def _fold_kernel(w_ref, c_ref, s_ref, o_ref):
    j = pl.program_id(0)
    n_fourier_blocks = FOURIER_WIDTH // MXU_WIDTH

    def dft(m_ref):
        w_hi, w_lo = _split_bf16(w_ref[...])
        m_hi, m_lo = _split_bf16(m_ref[...])
        return _dot(w_hi, m_hi) + _dot(w_hi, m_lo) + _dot(w_lo, m_hi)

    @pl.when(j < n_fourier_blocks)
    def _():
        o_ref[...] = dft(c_ref).astype(BF16)

    @pl.when(jnp.logical_and(j >= n_fourier_blocks, j < 2 * n_fourier_blocks))
    def _():
        o_ref[...] = dft(s_ref).astype(BF16)

    @pl.when(j >= 2 * n_fourier_blocks)
    def _():
        o_ref[...] = w_ref[...].astype(BF16)


def _fold_in_proj(w_in_a, dft_c, dft_s):
    nb = FOURIER_WIDTH // MXU_WIDTH
    return pl.pallas_call(
        _fold_kernel,
        grid=(PROJ_WIDTH // MXU_WIDTH,),
        in_specs=[
            pl.BlockSpec((None, D_MODEL, MXU_WIDTH), lambda j: (0, 0, jnp.where(j < nb, j, j - nb))),
            pl.BlockSpec((MXU_WIDTH, MXU_WIDTH), lambda j: (0, 0)),
            pl.BlockSpec((MXU_WIDTH, MXU_WIDTH), lambda j: (0, 0)),
        ],
        out_specs=pl.BlockSpec((D_MODEL, MXU_WIDTH), lambda j: (0, j)),
        out_shape=jax.ShapeDtypeStruct((D_MODEL, PROJ_WIDTH), BF16),
        compiler_params=_params(1),
        name="fold_in_proj",
    )(w_in_a, dft_c, dft_s)


def _step_blocking(b, s):
    rows = min(s, TILES_PER_STEP * ROW_TILE)
    batches = TILES_PER_STEP * ROW_TILE // rows
    assert s % rows == 0 and b % batches == 0 and rows % ROW_TILE == 0
    return batches, rows, [(bu, ru) for bu in range(batches) for ru in range(0, rows, ROW_TILE)]


def _in_proj_kernel(*refs, rope, tiles):
    if rope:
        (x_ref, mod_ref, w_ref, hs_ref, g_ref, cos_ref, sa_ref, sb_ref,
         fcs_ref, q_ref, k_ref, vt_ref) = refs
    else:
        x_ref, mod_ref, w_ref, hs_ref, g_ref, fcs_ref, q_ref, k_ref, vt_ref, v_ref = refs
    shift = mod_ref[0:1, :]
    scale = mod_ref[1:2, :]
    for bu, ru in tiles:
        rows = slice(ru, ru + ROW_TILE)
        h = (x_ref[bu, rows, :] * (1.0 + scale) + shift).astype(BF16)
        qk_cols = slice(2 * FOURIER_WIDTH, 2 * FOURIER_WIDTH + QK_WIDTH)
        qk = _dot(h, w_ref[:, qk_cols])
        ssq = _dot((qk * qk).astype(BF16), hs_ref[...])
        fcs_ref[bu, rows, :] = _dot(h, w_ref[:, :2 * FOURIER_WIDTH]).astype(BF16)
        v = _dot(h, w_ref[:, 2 * FOURIER_WIDTH + QK_WIDTH:])
        y = qk * lax.rsqrt(ssq * (1.0 / HEAD_DIM) + EPS) * g_ref[...]
        if rope:
            pos = pl.ds(pl.multiple_of(pl.program_id(1) * x_ref.shape[1] + ru, ROW_TILE), ROW_TILE)
            parts = []
            for t in range(QK_WIDTH // LANES):
                yt = y[:, t * LANES:(t + 1) * LANES]
                lo = slice(t * LANES, (t + 1) * LANES)
                parts.append(yt * cos_ref[pos, lo]
                             + pltpu.roll(yt, LANES - 16, 1) * sa_ref[pos, lo]
                             + pltpu.roll(yt, 16, 1) * sb_ref[pos, lo])
            y = jnp.concatenate(parts, axis=1)
        q_ref[bu, rows, :] = (y[:, :ATTN_WIDTH] * (HEAD_DIM ** -0.5 * LOG2_E)).astype(q_ref.dtype)
        k_ref[bu, rows, :] = y[:, ATTN_WIDTH:].astype(k_ref.dtype)
        vt_ref[bu, :, rows] = v.T.astype(vt_ref.dtype)
        if not rope:
            v_ref[bu, rows, :] = v


def _in_proj(x, mod4, mod_row, w_cat, head_sum, gains, rope_tabs):
    b, s, _ = x.shape
    rope = rope_tabs is not None
    bb, tr, tiles = _step_blocking(b, s)
    row_spec = lambda w: pl.BlockSpec((bb, tr, w), lambda bi, i: (bi, i, 0))
    out_specs = [row_spec(2 * FOURIER_WIDTH), row_spec(ATTN_WIDTH), row_spec(KV_WIDTH),
                 pl.BlockSpec((bb, KV_WIDTH, tr), lambda bi, i: (bi, 0, i))]
    out_shape = [
        jax.ShapeDtypeStruct((b, s, 2 * FOURIER_WIDTH), BF16),
        jax.ShapeDtypeStruct((b, s, ATTN_WIDTH), BF16),
        jax.ShapeDtypeStruct((b, s, KV_WIDTH), BF16 if rope else F32),
        jax.ShapeDtypeStruct((b, KV_WIDTH, s), BF16),
    ]
    if not rope:
        out_specs.append(row_spec(KV_WIDTH))
        out_shape.append(jax.ShapeDtypeStruct((b, s, KV_WIDTH), F32))
    in_specs = [
        row_spec(D_MODEL),
        pl.BlockSpec((None, None, 6, D_MODEL), lambda bi, i: (0, mod_row(bi * bb), 0, 0)),
        _resident((D_MODEL, PROJ_WIDTH), lambda bi, i: (0, 0)),
        _resident((QK_WIDTH, QK_WIDTH), lambda bi, i: (0, 0)),
        _resident((1, QK_WIDTH), lambda bi, i: (0, 0)),
    ]
    args = [x, mod4, w_cat, head_sum, gains]
    if rope:
        in_specs += [_resident((s, QK_WIDTH), lambda bi, i: (0, 0))] * 3
        args += list(rope_tabs)
    return pl.pallas_call(
        functools.partial(_in_proj_kernel, rope=rope, tiles=tiles),
        grid=(b // bb, s // tr),
        in_specs=in_specs,
        out_specs=out_specs,
        out_shape=out_shape,
        compiler_params=_params(2),
        name="in_proj_rope" if rope else "in_proj",
    )(*args)


def _attn_kernel(*refs, has_cache, sub_tile):
    if has_cache:
        q_ref, k_ref, vt_ref, ck_ref, cvt_ref, o_ref = refs
    else:
        q_ref, k_ref, vt_ref, o_ref = refs
    nt = (((1,), (1,)), ((), ()))
    ones_rows = jnp.ones((BF16_SUBLANES, KEY_CHUNK), BF16)

    key_sets = [(k_ref, vt_ref)] + ([(ck_ref, cvt_ref)] if has_cache else [])

    items = []
    for t in range(q_ref.shape[0] // sub_tile):
        q = q_ref[t * sub_tile:(t + 1) * sub_tile, :]
        for j in range(N_KV_HEADS):
            kv = slice(j * HEAD_DIM, (j + 1) * HEAD_DIM)
            qs = jnp.concatenate(
                [q[:, h * HEAD_DIM:(h + 1) * HEAD_DIM] for h in range(j * Q_PER_KV, (j + 1) * Q_PER_KV)], axis=0)
            sts = [lax.dot_general(keys_ref[:, kv].astype(BF16), qs, nt, preferred_element_type=F32)
                   for keys_ref, _ in key_sets]
            items.append((t, j, sts))

    partial = {}
    for t, j, sts in items:
        for st_all, (_, vts_ref) in zip(sts, key_sets):
            for c in range(0, st_all.shape[0], KEY_CHUNK):
                st = st_all[c:c + KEY_CHUNK]
                m = jnp.max(st, axis=0, keepdims=True)
                vt_ext = jnp.concatenate(
                    [vts_ref[j * HEAD_DIM:(j + 1) * HEAD_DIM, c:c + KEY_CHUNK].astype(BF16), ones_rows], axis=0)
                ot = _dot(vt_ext, jnp.exp2(st - m).astype(BF16))
                partial.setdefault((t, j), []).append((m, ot))

    for (t, j), parts in partial.items():
        m_all = functools.reduce(jnp.maximum, [m for m, _ in parts])
        ot = (functools.reduce(jnp.add, [jnp.exp2(m - m_all) * o for m, o in parts])
              if len(parts) > 1 else parts[0][1])
        o = ot[0:HEAD_DIM] / ot[HEAD_DIM:HEAD_DIM + 1]
        for g in range(0, Q_PER_KV, 2):
            pair = jnp.concatenate([o[:, g * sub_tile:(g + 1) * sub_tile],
                                    o[:, (g + 1) * sub_tile:(g + 2) * sub_tile]], axis=0).T
            lane0 = (j * Q_PER_KV + g) * HEAD_DIM
            o_ref[t * sub_tile:(t + 1) * sub_tile, lane0:lane0 + LANES] = pair.astype(o_ref.dtype)


def _attention(q, k, vt, cache, tq, sub_tile):
    b, s, _ = q.shape
    has_cache = cache is not None
    seq_spec = lambda n: pl.BlockSpec((None, n, KV_WIDTH), lambda bi, i: (bi, 0, 0))
    seq_t_spec = lambda n: pl.BlockSpec((None, KV_WIDTH, n), lambda bi, i: (bi, 0, 0))
    in_specs = [pl.BlockSpec((None, tq, ATTN_WIDTH), lambda bi, i: (bi, i, 0)), seq_spec(s), seq_t_spec(s)]
    args = [q, k, vt]
    if has_cache:
        past = cache[0].shape[1]
        in_specs += [seq_spec(past), seq_t_spec(past)]
        args += list(cache)
    return pl.pallas_call(
        functools.partial(_attn_kernel, has_cache=has_cache, sub_tile=sub_tile),
        grid=(b, s // tq),
        in_specs=in_specs,
        out_specs=pl.BlockSpec((None, tq, ATTN_WIDTH), lambda bi, i: (bi, i, 0)),
        out_shape=jax.ShapeDtypeStruct((b, s, ATTN_WIDTH), BF16),
        compiler_params=_params(2),
        name="attention_cached" if has_cache else "attention",
    )(*args)


def _out_proj_kernel(x_ref, fcs_ref, dc_ref, ds_ref, attn_ref, w_ref, mod_ref, lng_ref, lnb_ref, o_ref, *, tiles):
    gate = mod_ref[2:3, :]
    for bu, ru in tiles:
        rows = slice(ru, ru + ROW_TILE)
        pos = pl.ds(pl.multiple_of(pl.program_id(1) * x_ref.shape[1] + ru, ROW_TILE), ROW_TILE)
        four = (_dot(dc_ref[pos, :], fcs_ref[bu, :, :FOURIER_WIDTH])
                - _dot(ds_ref[pos, :], fcs_ref[bu, :, FOURIER_WIDTH:]))
        out = (_dot(four.astype(BF16), w_ref[:FOURIER_WIDTH, :])
               + _dot(attn_ref[bu, rows, :], w_ref[FOURIER_WIDTH:, :]))
        y = DEEPNORM_ALPHA * x_ref[bu, rows, :] + gate * out
        o_ref[bu, rows, :] = _layer_norm(y, lng_ref[0:1, :], lnb_ref[0:1, :])


def _out_proj(x, fcs, attn, dft_c, dft_s, w_out, mod4, mod_row, ln_g, ln_b, layer):
    b, s, _ = x.shape
    bb, tr, tiles = _step_blocking(b, s)
    row_spec = lambda w: pl.BlockSpec((bb, tr, w), lambda bi, i: (bi, i, 0))
    ln_spec = pl.BlockSpec((None, 2, D_MODEL), lambda bi, i: (layer, 0, 0))
    return pl.pallas_call(
        functools.partial(_out_proj_kernel, tiles=tiles),
        grid=(b // bb, s // tr),
        in_specs=[
            row_spec(D_MODEL),
            pl.BlockSpec((bb, s, 2 * FOURIER_WIDTH), lambda bi, i: (bi, 0, 0)),
            _resident((s, s), lambda bi, i: (0, 0)),
            _resident((s, s), lambda bi, i: (0, 0)),
            row_spec(ATTN_WIDTH),
            _resident((2 * FOURIER_WIDTH, D_MODEL), lambda bi, i: (0, 0)),
            pl.BlockSpec((None, None, 6, D_MODEL), lambda bi, i: (layer, mod_row(bi * bb), 0, 0)),
            ln_spec, ln_spec,
        ],
        out_specs=row_spec(D_MODEL),
        out_shape=jax.ShapeDtypeStruct((b, s, D_MODEL), F32),
        compiler_params=_params(2),
        name="out_proj",
    )(x, fcs, dft_c, dft_s, attn, w_out, mod4, ln_g, ln_b)


def _load_permuted(slab_ref):
    slabs = [jnp.concatenate([slab_ref[c, pl.ds(v, SUBLANES, stride=PERM_STRIDE), :] for v in range(PERM_STRIDE)],
                             axis=0) for c in range(N_SLABS)]
    return jnp.concatenate(slabs, axis=1)


def _store_unpermuted(y, slab_ref):
    for c in range(N_SLABS):
        for v in range(PERM_STRIDE):
            slab_ref[c, pl.ds(v, SUBLANES, stride=PERM_STRIDE), :] = (
                y[v * SUBLANES:(v + 1) * SUBLANES, c * LANES:(c + 1) * LANES])


def _stage_rows(u, tiles_per_seq, x_ref, xp_ref, xn_ref, shift, scale, xe_ref, h_ref):
    lo = u * ROW_TILE
    for c in range(N_SLABS):
        cols = slice(c * LANES, (c + 1) * LANES)
        xe_ref[u, c, 0:HALO, :] = xp_ref[:, cols] if u == 0 else x_ref[lo - HALO:lo, cols]
        xe_ref[u, c, HALO:PERM_ROWS, :] = x_ref[lo:lo + ROW_TILE, cols]
    nxt = xn_ref[...] if u == TILES_PER_STEP - 1 else x_ref[lo + ROW_TILE:lo + ROW_TILE + HALO, :]
    if tiles_per_seq == 1:
        has_prev = has_next = 0.0
    else:
        tile = pl.program_id(0) * TILES_PER_STEP + u
        has_prev = jnp.where(lax.rem(tile, tiles_per_seq) != 0, 1.0, 0.0)
        has_next = jnp.where(lax.rem(tile + 1, tiles_per_seq) != 0, 1.0, 0.0)
    rows = lax.broadcasted_iota(jnp.int32, (PERM_ROWS, 1), 0)
    is_halo = jnp.logical_and(rows % SUBLANES == 0, rows < SUBLANES * HALO)
    keep = jnp.where(is_halo, has_prev, 1.0)
    mod = lambda t: t * (1.0 + scale) + shift
    h = jnp.concatenate([mod(_load_permuted(xe_ref.at[u])) * keep, mod(nxt) * has_next], axis=0)
    h_ref[u] = h.astype(BF16)


def _conv3_permuted(t, cw):
    body = t[0:PERM_ROWS]
    first_prev = pltpu.roll(t[PERM_ROWS - SUBLANES:PERM_ROWS], 1, 0)
    prev = jnp.concatenate([first_prev, t[0:PERM_ROWS - SUBLANES]], axis=0)
    sublane = lax.broadcasted_iota(jnp.int32, (SUBLANES, 1), 0)
    last_next = jnp.where(sublane == SUBLANES - 1, t[PERM_ROWS:PERM_ROWS + 1],
                          pltpu.roll(t[0:SUBLANES], SUBLANES - 1, 0))
    nxt = jnp.concatenate([t[SUBLANES:PERM_ROWS], last_next], axis=0)
    return cw[0:1, :] * prev + cw[1:2, :] * body + cw[2:3, :] * nxt


def _finish_rows(u, xe_ref, acc, gate, lng, lnb, oe_ref, o_ref):
    y = DEEPNORM_ALPHA * _load_permuted(xe_ref.at[u]) + gate * acc
    _store_unpermuted(_layer_norm(y, lng, lnb), oe_ref.at[u])
    for c in range(N_SLABS):
        o_ref[u * ROW_TILE:(u + 1) * ROW_TILE, c * LANES:(c + 1) * LANES] = oe_ref[u, c, HALO:PERM_ROWS, :]


def _ffn_kernel(x_ref, xp_ref, xn_ref, mod_ref, wup_ref, cw_ref, wdn_ref, lng_ref, lnb_ref, o_ref,
                xe_ref, h_ref, act_ref, oe_ref, *, tiles_per_seq):
    for u in range(TILES_PER_STEP):
        _stage_rows(u, tiles_per_seq, x_ref, xp_ref, xn_ref, mod_ref[3:4, :], mod_ref[4:5, :], xe_ref, h_ref)
    for u in range(TILES_PER_STEP):
        for j in range(N_FF_CHUNKS):
            a_cols = slice(j * FF_CHUNK, (j + 1) * FF_CHUNK)
            g_cols = slice(D_FF + j * FF_CHUNK, D_FF + (j + 1) * FF_CHUNK)
            a = _conv3_permuted(_dot(h_ref[u], wup_ref[:, a_cols]), cw_ref[:, a_cols])
            g = _conv3_permuted(_dot(h_ref[u], wup_ref[:, g_cols]), cw_ref[:, g_cols])
            act_ref[u, :, a_cols] = (g * _sigmoid(g) * a).astype(BF16)
        acc = _dot(act_ref[u], wdn_ref[...])
        _finish_rows(u, xe_ref, acc, mod_ref[5:6, :], lng_ref[1:2, :], lnb_ref[1:2, :], oe_ref, o_ref)


def _mixer_kernel(x_ref, xp_ref, xn_ref, mod_ref, win_ref, cw_ref, wout_ref, lng_ref, lnb_ref, o_ref,
                  xe_ref, h_ref, act_ref, oe_ref, *, tiles_per_seq):
    for u in range(TILES_PER_STEP):
        _stage_rows(u, tiles_per_seq, x_ref, xp_ref, xn_ref, mod_ref[0:1, :], mod_ref[1:2, :], xe_ref, h_ref)
    for u in range(TILES_PER_STEP):
        for j in range(N_MIX_CHUNKS):
            cols = slice(j * FF_CHUNK, (j + 1) * FF_CHUNK)
            bg, cg, xin = (_dot(h_ref[u], win_ref[:, k * D_MODEL + j * FF_CHUNK:k * D_MODEL + (j + 1) * FF_CHUNK])
                           for k in range(3))
            y = bg[0:PERM_ROWS] * _conv3_permuted(cg * xin, cw_ref[:, cols])
            act_ref[u, :, cols] = y.astype(BF16)
        acc = _dot(act_ref[u], wout_ref[...])
        _finish_rows(u, xe_ref, acc, mod_ref[2:3, :], lng_ref[0:1, :], lnb_ref[0:1, :], oe_ref, o_ref)


def _gated_conv_block(kernel_fn, name, x, mod4, mod_row, w1, cw, w2, ln_g, ln_b, layer, w_layer):
    b, s, _ = x.shape
    hidden = w2.shape[1]
    n_rows = b * s
    step_rows = TILES_PER_STEP * ROW_TILE
    assert s % ROW_TILE == 0 and (s % step_rows == 0 or step_rows % s == 0)
    halo_blocks_per_step = step_rows // HALO
    last_halo_block = n_rows // HALO - 1
    ln_spec = pl.BlockSpec((None, 2, D_MODEL), lambda i: (layer, 0, 0))
    out = pl.pallas_call(
        functools.partial(kernel_fn, tiles_per_seq=s // ROW_TILE),
        grid=(n_rows // step_rows,),
        in_specs=[
            pl.BlockSpec((step_rows, D_MODEL), lambda i: (i, 0)),
            pl.BlockSpec((HALO, D_MODEL), lambda i: (jnp.maximum(i * halo_blocks_per_step - 1, 0), 0)),
            pl.BlockSpec((HALO, D_MODEL),
                         lambda i: (jnp.minimum((i + 1) * halo_blocks_per_step, last_halo_block), 0)),
            pl.BlockSpec((None, None, 6, D_MODEL), lambda i: (layer, mod_row(i * step_rows // s), 0, 0)),
            _resident((None,) + w1.shape[1:], lambda i: (w_layer, 0, 0)),
            _resident((None,) + cw.shape[1:], lambda i: (w_layer, 0, 0)),
            _resident((None,) + w2.shape[1:], lambda i: (w_layer, 0, 0)),
            ln_spec, ln_spec,
        ],
        out_specs=pl.BlockSpec((step_rows, D_MODEL), lambda i: (i, 0)),
        out_shape=jax.ShapeDtypeStruct((n_rows, D_MODEL), F32),
        scratch_shapes=[
            pltpu.VMEM((TILES_PER_STEP, N_SLABS, PERM_ROWS, LANES), F32),
            pltpu.VMEM((TILES_PER_STEP, EXT_ROWS, D_MODEL), BF16),
            pltpu.VMEM((TILES_PER_STEP, PERM_ROWS, hidden), BF16),
            pltpu.VMEM((TILES_PER_STEP, N_SLABS, PERM_ROWS, LANES), F32),
        ],
        compiler_params=_params(1),
        name=name,
    )(*([x.reshape(n_rows, D_MODEL)] * 3), mod4, w1, cw, w2, ln_g, ln_b)
    return out.reshape(b, s, D_MODEL)


def _dft_tables(n, scale):
    jk = np.outer(np.arange(n), np.arange(n)) % n
    ang = 2.0 * np.pi * jk / n
    return (np.cos(ang) * scale).astype(np.float32), (np.sin(ang) * scale).astype(np.float32)


def _channel_dft_blocks():
    c, s = _dft_tables(FOURIER_GROUP, 1.0)
    eye = np.eye(MXU_WIDTH // FOURIER_GROUP, dtype=np.float32)
    return np.kron(eye, c), np.kron(eye, s)


def _rope_tables(n):
    half = HEAD_DIM // 2
    inv = 1.0 / (ROPE_THETA ** (np.arange(0, half, 2, dtype=np.float64) / half))
    pos = np.arange(n)
    d = np.arange(QK_WIDTH) % HEAD_DIM
    coord = np.where(d < half, (pos // GRID_W)[:, None], (pos % GRID_W)[:, None])
    ang = coord * inv[d % (half // 2)][None, :]
    first = (d % half) < (half // 2)
    cos = np.cos(ang)
    sa = np.where(first[None, :], -np.sin(ang), 0.0)
    sb = np.where(first[None, :], 0.0, np.sin(ang))
    return tuple(jnp.asarray(t.astype(np.float32)) for t in (cos, sa, sb))


def _head_sum_matrix():
    return np.kron(np.eye(QK_WIDTH // HEAD_DIM, dtype=np.float32), np.ones((HEAD_DIM, HEAD_DIM), np.float32))


def kernel(x_prompt, x_sample, cache_k, cache_v, c, c_ctx, w_ada, b_ada, ln_g, ln_b, w_in_a, q_norm_g,
           k_norm_g, w_out_a, w_in_c, conv_c, w_out_c, w_up, conv_f, w_down):
    n_prompt, s_prompt, _ = x_prompt.shape
    n_sample, s_sample, _ = x_sample.shape
    past = cache_k.shape[2]

    cond = jnp.concatenate(
        [c_ctx[None, :], c, jnp.zeros((N_COND_ROWS - 1 - n_sample, D_MODEL), F32)], axis=0)
    mod4 = _modulation(cond, w_ada, b_ada).reshape(DEPTH, N_COND_ROWS, 6, D_MODEL)

    chan_c, chan_s = _channel_dft_blocks()
    w_cat = _fold_in_proj(w_in_a, jnp.asarray(chan_c), jnp.asarray(chan_s))
    head_sum = jnp.asarray(_head_sum_matrix()).astype(BF16)
    gains = jnp.concatenate([jnp.tile(q_norm_g[0], N_HEADS), jnp.tile(k_norm_g[0], N_KV_HEADS)])[None, :]
    w_out_bf = w_out_a[0].astype(BF16)

    w_up_bf, w_down_bf = w_up.astype(BF16), w_down.astype(BF16)
    w_in_c_bf, w_out_c_bf = w_in_c.astype(BF16), w_out_c.astype(BF16)

    def run_stream(x, mod_row, cache):
        s = x.shape[1]
        scale = (FOURIER_GROUP * s) ** -0.5
        dft_c, dft_s = (jnp.asarray(t).astype(BF16) for t in _dft_tables(s, scale))
        rope_tabs = _rope_tables(s) if cache is not None else None
        fcs, q, k, vt, *v = _in_proj(x, mod4, mod_row, w_cat, head_sum, gains, rope_tabs)
        attn = _attention(q, k, vt, cache, ATTN_Q_TILE, ATTN_SUB_TILE)
        x = _out_proj(x, fcs, attn, dft_c, dft_s, w_out_bf, mod4, mod_row, ln_g, ln_b, 0)
        x = _gated_conv_block(_ffn_kernel, "conv_ffn", x, mod4, mod_row, w_up_bf, conv_f, w_down_bf,
                              ln_g, ln_b, 0, 0)
        x = _gated_conv_block(_mixer_kernel, "conv_mixer", x, mod4, mod_row, w_in_c_bf, conv_c, w_out_c_bf,
                              ln_g, ln_b, 1, 0)
        x = _gated_conv_block(_ffn_kernel, "conv_ffn", x, mod4, mod_row, w_up_bf, conv_f, w_down_bf,
                              ln_g, ln_b, 1, 1)
        return x, k, v

    y_prompt, k_new, (v_new,) = run_stream(x_prompt, lambda bi: 0, None)
    cache = (cache_k[:, 0].reshape(n_sample, past, KV_WIDTH),
             cache_v[:, 0].reshape(n_sample, past, KV_WIDTH).transpose(0, 2, 1))
    y_sample, _, _ = run_stream(x_sample, lambda bi: bi + 1, cache)

    new_shape = (n_prompt, 1, s_prompt, N_KV_HEADS, HEAD_DIM)
    return y_prompt, y_sample, k_new.reshape(new_shape), v_new.reshape(new_shape)
```

```python
import functools

import numpy as np
import jax
import jax.numpy as jnp
from jax import lax
from jax.experimental import pallas as pl
from jax.experimental.pallas import tpu as pltpu

D_MODEL = 1024
DEPTH = 2
GRID_W = 64
HEAD_DIM = 64
N_HEADS = 8
N_KV_HEADS = 2
Q_PER_KV = N_HEADS // N_KV_HEADS
ATTN_WIDTH = N_HEADS * HEAD_DIM
KV_WIDTH = N_KV_HEADS * HEAD_DIM
QK_WIDTH = ATTN_WIDTH + KV_WIDTH
FOURIER_GROUP = 64
FOURIER_WIDTH = 512
D_FF = 2816
ROPE_THETA = 10000.0
EPS = 1e-6
DEEPNORM_ALPHA = (2 * DEPTH) ** 0.25
LOG2_E = 1.4426950408889634

F32 = jnp.float32
BF16 = jnp.bfloat16

SUBLANES = 8
BF16_SUBLANES = 16
LANES = 128
MXU_WIDTH = 256

ROW_TILE = 256
ATTN_Q_TILE = 256
ATTN_SUB_TILE = 128
KEY_CHUNK = MXU_WIDTH
HALO = SUBLANES
PERM_ROWS = HALO + ROW_TILE
PERM_STRIDE = PERM_ROWS // SUBLANES
N_SLABS = D_MODEL // LANES
TILES_PER_STEP = 2
FF_CHUNK = MXU_WIDTH
N_FF_CHUNKS = D_FF // FF_CHUNK
N_MIX_CHUNKS = D_MODEL // FF_CHUNK
PROJ_WIDTH = 2 * FOURIER_WIDTH + QK_WIDTH + KV_WIDTH
N_COND_ROWS = 16
VMEM_LIMIT_BYTES = 48 * 1024 * 1024


def _dot(a, b):
    return jnp.dot(a, b, preferred_element_type=F32)


def _split_bf16(a):
    hi = a.astype(BF16)
    lo = (a - hi.astype(F32)).astype(BF16)
    return hi, lo


def _sigmoid(x):
    return 1.0 / (1.0 + jnp.exp(-x))


def _layer_norm(y, g, b):
    mu = jnp.mean(y, axis=-1, keepdims=True)
    yc = y - mu
    var = jnp.mean(yc * yc, axis=-1, keepdims=True)
    return yc * lax.rsqrt(var + EPS) * g + b


def _params(n_axes):
    return pltpu.CompilerParams(dimension_semantics=("arbitrary",) * n_axes,
                                vmem_limit_bytes=VMEM_LIMIT_BYTES)


def _resident(block, index_map):
    return pl.BlockSpec(block, index_map, pipeline_mode=pl.Buffered(1))


def _mod_kernel(cond_ref, w_ref, b_ref, o_ref):
    c = cond_ref[...]
    s = (c * _sigmoid(c)).astype(BF16)
    o_ref[...] = _dot(s, w_ref[...].astype(BF16)) + b_ref[...]


def _modulation(cond, w_ada, b_ada):
    tn = 1536
    n_out = 6 * D_MODEL
    return pl.pallas_call(
        _mod_kernel,
        grid=(DEPTH, n_out // tn),
        in_specs=[
            pl.BlockSpec((N_COND_ROWS, D_MODEL), lambda l, n: (0, 0)),
            pl.BlockSpec((None, D_MODEL, tn), lambda l, n: (l, 0, n)),
            pl.BlockSpec((None, 1, tn), lambda l, n: (l, 0, n)),
        ],
        out_specs=pl.BlockSpec((None, N_COND_ROWS, tn), lambda l, n: (l, 0, n)),
        out_shape=jax.ShapeDtypeStruct((DEPTH, N_COND_ROWS, n_out), F32),
        compiler_params=_params(2),
        name="modulation",
    )(cond, w_ada, b_ada.reshape(DEPTH, 1, n_out))


def _fold_kernel(w_ref, c_ref, s_ref, o_ref):
    j = pl.program_id(0)
    n_fourier_blocks = FOURIER_WIDTH // MXU_WIDTH

    def dft(m_ref):
        w_hi, w_lo = _split_bf16(w_ref[...])
        m_hi, m_lo = _split_bf16(m_ref[...])
        return _dot(w_hi, m_hi) + _dot(w_hi, m_lo) + _dot(w_lo, m_hi)

    @pl.when(j < n_fourier_blocks)
    def _():
        o_ref[...] = dft(c_ref).astype(BF16)

    @pl.when(jnp.logical_and(j >= n_fourier_blocks, j < 2 * n_fourier_blocks))
    def _():
        o_ref[...] = dft(s_ref).astype(BF16)

    @pl.when(j >= 2 * n_fourier_blocks)
    def _():
        o_ref[...] = w_ref[...].astype(BF16)


def _fold_in_proj(w_in_a, dft_c, dft_s):
    nb = FOURIER_WIDTH // MXU_WIDTH
    return pl.pallas_call(
        _fold_kernel,
        grid=(PROJ_WIDTH // MXU_WIDTH,),
        in_specs=[
            pl.BlockSpec((None, D_MODEL, MXU_WIDTH), lambda j: (0, 0, jnp.where(j < nb, j, j - nb))),
            pl.BlockSpec((MXU_WIDTH, MXU_WIDTH), lambda j: (0, 0)),
            pl.BlockSpec((MXU_WIDTH, MXU_WIDTH), lambda j: (0, 0)),
        ],
        out_specs=pl.BlockSpec((D_MODEL, MXU_WIDTH), lambda j: (0, j)),
        out_shape=jax.ShapeDtypeStruct((D_MODEL, PROJ_WIDTH), BF16),
        compiler_params=_params(1),
        name="fold_in_proj",
    )(w_in_a, dft_c, dft_s)


def _step_blocking(b, s):
    rows = min(s, TILES_PER_STEP * ROW_TILE)
    batches = TILES_PER_STEP * ROW_TILE // rows
    assert s % rows == 0 and b % batches == 0 and rows % ROW_TILE == 0
    return batches, rows, [(bu, ru) for bu in range(batches) for ru in range(0, rows, ROW_TILE)]


def _in_proj_kernel(*refs, rope, tiles):
    if rope:
        (x_ref, mod_ref, w_ref, hs_ref, g_ref, cos_ref, sa_ref, sb_ref,
         fcs_ref, q_ref, k_ref, vt_ref) = refs
    else:
        x_ref, mod_ref, w_ref, hs_ref, g_ref, fcs_ref, q_ref, k_ref, vt_ref, v_ref = refs
    shift = mod_ref[0:1, :]
    scale = mod_ref[1:2, :]
    for bu, ru in tiles:
        rows = slice(ru, ru + ROW_TILE)
        h = (x_ref[bu, rows, :] * (1.0 + scale) + shift).astype(BF16)
        qk_cols = slice(2 * FOURIER_WIDTH, 2 * FOURIER_WIDTH + QK_WIDTH)
        qk = _dot(h, w_ref[:, qk_cols])
        ssq = _dot((qk * qk).astype(BF16), hs_ref[...])
        fcs_ref[bu, rows, :] = _dot(h, w_ref[:, :2 * FOURIER_WIDTH]).astype(BF16)
        v = _dot(h, w_ref[:, 2 * FOURIER_WIDTH + QK_WIDTH:])
        y = qk * lax.rsqrt(ssq * (1.0 / HEAD_DIM) + EPS) * g_ref[...]
        if rope:
            pos = pl.ds(pl.multiple_of(pl.program_id(1) * x_ref.shape[1] + ru, ROW_TILE), ROW_TILE)
            parts = []
            for t in range(QK_WIDTH // LANES):
                yt = y[:, t * LANES:(t + 1) * LANES]
                lo = slice(t * LANES, (t + 1) * LANES)
                parts.append(yt * cos_ref[pos, lo]
                             + pltpu.roll(yt, LANES - 16, 1) * sa_ref[pos, lo]
                             + pltpu.roll(yt, 16, 1) * sb_ref[pos, lo])
            y = jnp.concatenate(parts, axis=1)
        q_ref[bu, rows, :] = (y[:, :ATTN_WIDTH] * (HEAD_DIM ** -0.5 * LOG2_E)).astype(q_ref.dtype)
        k_ref[bu, rows, :] = y[:, ATTN_WIDTH:].astype(k_ref.dtype)
        vt_ref[bu, :, rows] = v.T.astype(vt_ref.dtype)
        if not rope:
            v_ref[bu, rows, :] = v


def _in_proj(x, mod4, mod_row, w_cat, head_sum, gains, rope_tabs):
    b, s, _ = x.shape
    rope = rope_tabs is not None
    bb, tr, tiles = _step_blocking(b, s)
    row_spec = lambda w: pl.BlockSpec((bb, tr, w), lambda bi, i: (bi, i, 0))
    out_specs = [row_spec(2 * FOURIER_WIDTH), row_spec(ATTN_WIDTH), row_spec(KV_WIDTH),
                 pl.BlockSpec((bb, KV_WIDTH, tr), lambda bi, i: (bi, 0, i))]
    out_shape = [
        jax.ShapeDtypeStruct((b, s, 2 * FOURIER_WIDTH), BF16),
        jax.ShapeDtypeStruct((b, s, ATTN_WIDTH), BF16),
        jax.ShapeDtypeStruct((b, s, KV_WIDTH), BF16 if rope else F32),
        jax.ShapeDtypeStruct((b, KV_WIDTH, s), BF16),
    ]
    if not rope:
        out_specs.append(row_spec(KV_WIDTH))
        out_shape.append(jax.ShapeDtypeStruct((b, s, KV_WIDTH), F32))
    in_specs = [
        row_spec(D_MODEL),
        pl.BlockSpec((None, None, 6, D_MODEL), lambda bi, i: (0, mod_row(bi * bb), 0, 0)),
        _resident((D_MODEL, PROJ_WIDTH), lambda bi, i: (0, 0)),
        _resident((QK_WIDTH, QK_WIDTH), lambda bi, i: (0, 0)),
        _resident((1, QK_WIDTH), lambda bi, i: (0, 0)),
    ]
    args = [x, mod4, w_cat, head_sum, gains]
    if rope:
        in_specs += [_resident((s, QK_WIDTH), lambda bi, i: (0, 0))] * 3
        args += list(rope_tabs)
    return pl.pallas_call(
        functools.partial(_in_proj_kernel, rope=rope, tiles=tiles),
        grid=(b // bb, s // tr),
        in_specs=in_specs,
        out_specs=out_specs,
        out_shape=out_shape,
        compiler_params=_params(2),
        name="in_proj_rope" if rope else "in_proj",
    )(*args)


def _attn_kernel(*refs, has_cache, sub_tile):
    if has_cache:
        q_ref, k_ref, vt_ref, ck_ref, cvt_ref, o_ref = refs
    else:
        q_ref, k_ref, vt_ref, o_ref = refs
    nt = (((1,), (1,)), ((), ()))
    ones_rows = jnp.ones((BF16_SUBLANES, KEY_CHUNK), BF16)

    key_sets = [(k_ref, vt_ref)] + ([(ck_ref, cvt_ref)] if has_cache else [])

    items = []
    for t in range(q_ref.shape[0] // sub_tile):
        q = q_ref[t * sub_tile:(t + 1) * sub_tile, :]
        for j in range(N_KV_HEADS):
            kv = slice(j * HEAD_DIM, (j + 1) * HEAD_DIM)
            qs = jnp.concatenate(
                [q[:, h * HEAD_DIM:(h + 1) * HEAD_DIM] for h in range(j * Q_PER_KV, (j + 1) * Q_PER_KV)], axis=0)
            sts = [lax.dot_general(keys_ref[:, kv].astype(BF16), qs, nt, preferred_element_type=F32)
                   for keys_ref, _ in key_sets]
            items.append((t, j, sts))

    partial = {}
    for t, j, sts in items:
        for st_all, (_, vts_ref) in zip(sts, key_sets):
            for c in range(0, st_all.shape[0], KEY_CHUNK):
                st = st_all[c:c + KEY_CHUNK]
                m = jnp.max(st, axis=0, keepdims=True)
                vt_ext = jnp.concatenate(
                    [vts_ref[j * HEAD_DIM:(j + 1) * HEAD_DIM, c:c + KEY_CHUNK].astype(BF16), ones_rows], axis=0)
                ot = _dot(vt_ext, jnp.exp2(st - m).astype(BF16))
                partial.setdefault((t, j), []).append((m, ot))

    for (t, j), parts in partial.items():
        m_all = functools.reduce(jnp.maximum, [m for m, _ in parts])
        ot = (functools.reduce(jnp.add, [jnp.exp2(m - m_all) * o for m, o in parts])
              if len(parts) > 1 else parts[0][1])
        o = ot[0:HEAD_DIM] / ot[HEAD_DIM:HEAD_DIM + 1]
        for g in range(0, Q_PER_KV, 2):
            pair = jnp.concatenate([o[:, g * sub_tile:(g + 1) * sub_tile],
                                    o[:, (g + 1) * sub_tile:(g + 2) * sub_tile]], axis=0).T
            lane0 = (j * Q_PER_KV + g) * HEAD_DIM
            o_ref[t * sub_tile:(t + 1) * sub_tile, lane0:lane0 + LANES] = pair.astype(o_ref.dtype)


def _attention(q, k, vt, cache, tq, sub_tile):
    b, s, _ = q.shape
    has_cache = cache is not None
    seq_spec = lambda n: pl.BlockSpec((None, n, KV_WIDTH), lambda bi, i: (bi, 0, 0))
    seq_t_spec = lambda n: pl.BlockSpec((None, KV_WIDTH, n), lambda bi, i: (bi, 0, 0))
    in_specs = [pl.BlockSpec((None, tq, ATTN_WIDTH), lambda bi, i: (bi, i, 0)), seq_spec(s), seq_t_spec(s)]
    args = [q, k, vt]
    if has_cache:
        past = cache[0].shape[1]
        in_specs += [seq_spec(past), seq_t_spec(past)]
        args += list(cache)
    return pl.pallas_call(
        functools.partial(_attn_kernel, has_cache=has_cache, sub_tile=sub_tile),
        grid=(b, s // tq),
        in_specs=in_specs,
        out_specs=pl.BlockSpec((None, tq, ATTN_WIDTH), lambda bi, i: (bi, i, 0)),
        out_shape=jax.ShapeDtypeStruct((b, s, ATTN_WIDTH), BF16),
        compiler_params=_params(2),
        name="attention_cached" if has_cache else "attention",
    )(*args)


def _out_proj_kernel(x_ref, fcs_ref, dc_ref, ds_ref, attn_ref, w_ref, mod_ref, lng_ref, lnb_ref, o_ref, *, tiles):
    gate = mod_ref[2:3, :]
    for bu, ru in tiles:
        rows = slice(ru, ru + ROW_TILE)
        pos = pl.ds(pl.multiple_of(pl.program_id(1) * x_ref.shape[1] + ru, ROW_TILE), ROW_TILE)
        four = (_dot(dc_ref[pos, :], fcs_ref[bu, :, :FOURIER_WIDTH])
                - _dot(ds_ref[pos, :], fcs_ref[bu, :, FOURIER_WIDTH:]))
        out = (_dot(four.astype(BF16), w_ref[:FOURIER_WIDTH, :])
               + _dot(attn_ref[bu, rows, :], w_ref[FOURIER_WIDTH:, :]))
        y = DEEPNORM_ALPHA * x_ref[bu, rows, :] + gate * out
        o_ref[bu, rows, :] = _layer_norm(y, lng_ref[0:1, :], lnb_ref[0:1, :])


def _out_proj(x, fcs, attn, dft_c, dft_s, w_out, mod4, mod_row, ln_g, ln_b, layer):
    b, s, _ = x.shape
    bb, tr, tiles = _step_blocking(b, s)
    row_spec = lambda w: pl.BlockSpec((bb, tr, w), lambda bi, i: (bi, i, 0))
    ln_spec = pl.BlockSpec((None, 2, D_MODEL), lambda bi, i: (layer, 0, 0))
    return pl.pallas_call(
        functools.partial(_out_proj_kernel, tiles=tiles),
        grid=(b // bb, s // tr),
        in_specs=[
            row_spec(D_MODEL),
            pl.BlockSpec((bb, s, 2 * FOURIER_WIDTH), lambda bi, i: (bi, 0, 0)),
            _resident((s, s), lambda bi, i: (0, 0)),
            _resident((s, s), lambda bi, i: (0, 0)),
            row_spec(ATTN_WIDTH),
            _resident((2 * FOURIER_WIDTH, D_MODEL), lambda bi, i: (0, 0)),
            pl.BlockSpec((None, None, 6, D_MODEL), lambda bi, i: (layer, mod_row(bi * bb), 0, 0)),
            ln_spec, ln_spec,
        ],
        out_specs=row_spec(D_MODEL),
        out_shape=jax.ShapeDtypeStruct((b, s, D_MODEL), F32),
        compiler_params=_params(2),
        name="out_proj",
    )(x, fcs, dft_c, dft_s, attn, w_out, mod4, ln_g, ln_b)


def _load_permuted(slab_ref):
    slabs = [jnp.concatenate([slab_ref[c, pl.ds(v, SUBLANES, stride=PERM_STRIDE), :] for v in range(PERM_STRIDE)],
                             axis=0) for c in range(N_SLABS)]
    return jnp.concatenate(slabs, axis=1)


def _store_unpermuted(y, slab_ref):
    for c in range(N_SLABS):
        for v in range(PERM_STRIDE):
            slab_ref[c, pl.ds(v, SUBLANES, stride=PERM_STRIDE), :] = (
                y[v * SUBLANES:(v + 1) * SUBLANES, c * LANES:(c + 1) * LANES])


def _stage_rows(u, tiles_per_seq, x_ref, xp_ref, xn_ref, shift, scale, xe_ref, h_ref):
    lo = u * ROW_TILE
    sublane = lax.broadcasted_iota(jnp.int32, (SUBLANES, 1), 0)
    for c in range(N_SLABS):
        cols = slice(c * LANES, (c + 1) * LANES)
        before = xp_ref[:, cols] if u == 0 else x_ref[lo - HALO:lo, cols]
        after = xn_ref[0:1, cols] if u == TILES_PER_STEP - 1 else x_ref[lo + ROW_TILE:lo + ROW_TILE + 1, cols]
        xe_ref[u, c, 0:HALO, :] = jnp.where(sublane == 0, after, before)
        xe_ref[u, c, HALO:PERM_ROWS, :] = x_ref[lo:lo + ROW_TILE, cols]
    if tiles_per_seq == 1:
        has_prev = has_next = 0.0
    else:
        tile = pl.program_id(0) * TILES_PER_STEP + u
        has_prev = jnp.where(lax.rem(tile, tiles_per_seq) != 0, 1.0, 0.0)
        has_next = jnp.where(lax.rem(tile + 1, tiles_per_seq) != 0, 1.0, 0.0)
    rows = lax.broadcasted_iota(jnp.int32, (PERM_ROWS, 1), 0)
    keep = jnp.where(rows == 0, has_next,
                     jnp.where(rows == SUBLANES * (HALO - 1), has_prev,
                               jnp.where(jnp.logical_and(rows % SUBLANES == 0, rows < SUBLANES * HALO), 0.0, 1.0)))
    h = (_load_permuted(xe_ref.at[u]) * (1.0 + scale) + shift) * keep
    h_ref[u] = h.astype(BF16)


def _conv3_permuted(t, cw):
    first_prev = pltpu.roll(t[PERM_ROWS - SUBLANES:PERM_ROWS], 1, 0)
    prev = jnp.concatenate([first_prev, t[0:PERM_ROWS - SUBLANES]], axis=0)
    last_next = pltpu.roll(t[0:SUBLANES], SUBLANES - 1, 0)
    nxt = jnp.concatenate([t[SUBLANES:PERM_ROWS], last_next], axis=0)
    return cw[0:1, :] * prev + cw[1:2, :] * t + cw[2:3, :] * nxt


def _finish_rows(u, xe_ref, acc, gate, lng, lnb, oe_ref, o_ref):
    y = DEEPNORM_ALPHA * _load_permuted(xe_ref.at[u]) + gate * acc
    _store_unpermuted(_layer_norm(y, lng, lnb), oe_ref.at[u])
    for c in range(N_SLABS):
        o_ref[u * ROW_TILE:(u + 1) * ROW_TILE, c * LANES:(c + 1) * LANES] = oe_ref[u, c, HALO:PERM_ROWS, :]


def _ffn_kernel(x_ref, xp_ref, xn_ref, mod_ref, wup_ref, cw_ref, wdn_ref, lng_ref, lnb_ref, o_ref,
                xe_ref, h_ref, act_ref, oe_ref, *, tiles_per_seq):
    for u in range(TILES_PER_STEP):
        _stage_rows(u, tiles_per_seq, x_ref, xp_ref, xn_ref, mod_ref[3:4, :], mod_ref[4:5, :], xe_ref, h_ref)
    for u in range(TILES_PER_STEP):
        for j in range(N_FF_CHUNKS):
            a_cols = slice(j * FF_CHUNK, (j + 1) * FF_CHUNK)
            g_cols = slice(D_FF + j * FF_CHUNK, D_FF + (j + 1) * FF_CHUNK)
            a = _conv3_permuted(_dot(h_ref[u], wup_ref[:, a_cols]), cw_ref[:, a_cols])
            g = _conv3_permuted(_dot(h_ref[u], wup_ref[:, g_cols]), cw_ref[:, g_cols])
            act_ref[u, :, a_cols] = (g * _sigmoid(g) * a).astype(BF16)
    for u in range(TILES_PER_STEP):
        acc = _dot(act_ref[u], wdn_ref[...])
        _finish_rows(u, xe_ref, acc, mod_ref[5:6, :], lng_ref[1:2, :], lnb_ref[1:2, :], oe_ref, o_ref)


def _mixer_kernel(x_ref, xp_ref, xn_ref, mod_ref, win_ref, cw_ref, wout_ref, lng_ref, lnb_ref, o_ref,
                  xe_ref, h_ref, act_ref, oe_ref, *, tiles_per_seq):
    for u in range(TILES_PER_STEP):
        _stage_rows(u, tiles_per_seq, x_ref, xp_ref, xn_ref, mod_ref[0:1, :], mod_ref[1:2, :], xe_ref, h_ref)
    for u in range(TILES_PER_STEP):
        for j in range(N_MIX_CHUNKS):
            cols = slice(j * FF_CHUNK, (j + 1) * FF_CHUNK)
            bg, cg, xin = (_dot(h_ref[u], win_ref[:, k * D_MODEL + j * FF_CHUNK:k * D_MODEL + (j + 1) * FF_CHUNK])
                           for k in range(3))
            y = bg * _conv3_permuted(cg * xin, cw_ref[:, cols])
            act_ref[u, :, cols] = y.astype(BF16)
    for u in range(TILES_PER_STEP):
        acc = _dot(act_ref[u], wout_ref[...])
        _finish_rows(u, xe_ref, acc, mod_ref[2:3, :], lng_ref[0:1, :], lnb_ref[0:1, :], oe_ref, o_ref)


def _gated_conv_block(kernel_fn, name, x, mod4, mod_row, w1, cw, w2, ln_g, ln_b, layer, w_layer):
    b, s, _ = x.shape
    hidden = w2.shape[1]
    n_rows = b * s
    step_rows = TILES_PER_STEP * ROW_TILE
    assert s % ROW_TILE == 0 and (s % step_rows == 0 or step_rows % s == 0)
    halo_blocks_per_step = step_rows // HALO
    last_halo_block = n_rows // HALO - 1
    ln_spec = pl.BlockSpec((None, 2, D_MODEL), lambda i: (layer, 0, 0))
    out = pl.pallas_call(
        functools.partial(kernel_fn, tiles_per_seq=s // ROW_TILE),
        grid=(n_rows // step_rows,),
        in_specs=[
            pl.BlockSpec((step_rows, D_MODEL), lambda i: (i, 0)),
            pl.BlockSpec((HALO, D_MODEL), lambda i: (jnp.maximum(i * halo_blocks_per_step - 1, 0), 0)),
            pl.BlockSpec((HALO, D_MODEL),
                         lambda i: (jnp.minimum((i + 1) * halo_blocks_per_step, last_halo_block), 0)),
            pl.BlockSpec((None, None, 6, D_MODEL), lambda i: (layer, mod_row(i * step_rows // s), 0, 0)),
            _resident((None,) + w1.shape[1:], lambda i: (w_layer, 0, 0)),
            _resident((None,) + cw.shape[1:], lambda i: (w_layer, 0, 0)),
            _resident((None,) + w2.shape[1:], lambda i: (w_layer, 0, 0)),
            ln_spec, ln_spec,
        ],
        out_specs=pl.BlockSpec((step_rows, D_MODEL), lambda i: (i, 0)),
        out_shape=jax.ShapeDtypeStruct((n_rows, D_MODEL), F32),
        scratch_shapes=[
            pltpu.VMEM((TILES_PER_STEP, N_SLABS, PERM_ROWS, LANES), F32),
            pltpu.VMEM((TILES_PER_STEP, PERM_ROWS, D_MODEL), BF16),
            pltpu.VMEM((TILES_PER_STEP, PERM_ROWS, hidden), BF16),
            pltpu.VMEM((TILES_PER_STEP, N_SLABS, PERM_ROWS, LANES), F32),
        ],
        compiler_params=_params(1),
        name=name,
    )(*([x.reshape(n_rows, D_MODEL)] * 3), mod4, w1, cw, w2, ln_g, ln_b)
    return out.reshape(b, s, D_MODEL)


def _dft_tables(n, scale):
    jk = np.outer(np.arange(n), np.arange(n)) % n
    ang = 2.0 * np.pi * jk / n
    return (np.cos(ang) * scale).astype(np.float32), (np.sin(ang) * scale).astype(np.float32)


def _channel_dft_blocks():
    c, s = _dft_tables(FOURIER_GROUP, 1.0)
    eye = np.eye(MXU_WIDTH // FOURIER_GROUP, dtype=np.float32)
    return np.kron(eye, c), np.kron(eye, s)


def _rope_tables(n):
    half = HEAD_DIM // 2
    inv = 1.0 / (ROPE_THETA ** (np.arange(0, half, 2, dtype=np.float64) / half))
    pos = np.arange(n)
    d = np.arange(QK_WIDTH) % HEAD_DIM
    coord = np.where(d < half, (pos // GRID_W)[:, None], (pos % GRID_W)[:, None])
    ang = coord * inv[d % (half // 2)][None, :]
    first = (d % half) < (half // 2)
    cos = np.cos(ang)
    sa = np.where(first[None, :], -np.sin(ang), 0.0)
    sb = np.where(first[None, :], 0.0, np.sin(ang))
    return tuple(jnp.asarray(t.astype(np.float32)) for t in (cos, sa, sb))


def _head_sum_matrix():
    return np.kron(np.eye(QK_WIDTH // HEAD_DIM, dtype=np.float32), np.ones((HEAD_DIM, HEAD_DIM), np.float32))


def kernel(x_prompt, x_sample, cache_k, cache_v, c, c_ctx, w_ada, b_ada, ln_g, ln_b, w_in_a, q_norm_g,
           k_norm_g, w_out_a, w_in_c, conv_c, w_out_c, w_up, conv_f, w_down):
    n_prompt, s_prompt, _ = x_prompt.shape
    n_sample, s_sample, _ = x_sample.shape
    past = cache_k.shape[2]

    cond = jnp.concatenate(
        [c_ctx[None, :], c, jnp.zeros((N_COND_ROWS - 1 - n_sample, D_MODEL), F32)], axis=0)
    mod4 = _modulation(cond, w_ada, b_ada).reshape(DEPTH, N_COND_ROWS, 6, D_MODEL)

    chan_c, chan_s = _channel_dft_blocks()
    w_cat = _fold_in_proj(w_in_a, jnp.asarray(chan_c), jnp.asarray(chan_s))
    head_sum = jnp.asarray(_head_sum_matrix()).astype(BF16)
    gains = jnp.concatenate([jnp.tile(q_norm_g[0], N_HEADS), jnp.tile(k_norm_g[0], N_KV_HEADS)])[None, :]
    w_out_bf = w_out_a[0].astype(BF16)

    w_up_bf, w_down_bf = w_up.astype(BF16), w_down.astype(BF16)
    w_in_c_bf, w_out_c_bf = w_in_c.astype(BF16), w_out_c.astype(BF16)

    def run_stream(x, mod_row, cache):
        s = x.shape[1]
        scale = (FOURIER_GROUP * s) ** -0.5
        dft_c, dft_s = (jnp.asarray(t).astype(BF16) for t in _dft_tables(s, scale))
        rope_tabs = _rope_tables(s) if cache is not None else None
        fcs, q, k, vt, *v = _in_proj(x, mod4, mod_row, w_cat, head_sum, gains, rope_tabs)
        attn = _attention(q, k, vt, cache, ATTN_Q_TILE, ATTN_SUB_TILE)
        x = _out_proj(x, fcs, attn, dft_c, dft_s, w_out_bf, mod4, mod_row, ln_g, ln_b, 0)
        x = _gated_conv_block(_ffn_kernel, "conv_ffn", x, mod4, mod_row, w_up_bf, conv_f, w_down_bf,
                              ln_g, ln_b, 0, 0)
        x = _gated_conv_block(_mixer_kernel, "conv_mixer", x, mod4, mod_row, w_in_c_bf, conv_c, w_out_c_bf,
                              ln_g, ln_b, 1, 0)
        x = _gated_conv_block(_ffn_kernel, "conv_ffn", x, mod4, mod_row, w_up_bf, conv_f, w_down_bf,
                              ln_g, ln_b, 1, 1)
        return x, k, v

    y_prompt, k_new, (v_new,) = run_stream(x_prompt, lambda bi: 0, None)
    cache = (cache_k[:, 0].reshape(n_sample, past, KV_WIDTH),
             cache_v[:, 0].reshape(n_sample, past, KV_WIDTH).transpose(0, 2, 1))
    y_sample, _, _ = run_stream(x_sample, lambda bi: bi + 1, cache)

    new_shape = (n_prompt, 1, s_prompt, N_KV_HEADS, HEAD_DIM)
    return y_prompt, y_sample, k_new.reshape(new_shape), v_new.reshape(new_shape)
```

```python
import functools

import numpy as np
import jax
import jax.numpy as jnp
from jax import lax
from jax.experimental import pallas as pl
from jax.experimental.pallas import tpu as pltpu

D_MODEL = 1024
DEPTH = 2
GRID_W = 64
HEAD_DIM = 64
N_HEADS = 8
N_KV_HEADS = 2
Q_PER_KV = N_HEADS // N_KV_HEADS
ATTN_WIDTH = N_HEADS * HEAD_DIM
KV_WIDTH = N_KV_HEADS * HEAD_DIM
QK_WIDTH = ATTN_WIDTH + KV_WIDTH
FOURIER_GROUP = 64
FOURIER_WIDTH = 512
D_FF = 2816
ROPE_THETA = 10000.0
EPS = 1e-6
DEEPNORM_ALPHA = (2 * DEPTH) ** 0.25
LOG2_E = 1.4426950408889634

F32 = jnp.float32
BF16 = jnp.bfloat16

SUBLANES = 8
BF16_SUBLANES = 16
LANES = 128
MXU_WIDTH = 256

ROW_TILE = 256
ATTN_Q_TILE = 256
ATTN_SUB_TILE = 128
KEY_CHUNK = MXU_WIDTH
HALO = SUBLANES
PERM_ROWS = HALO + ROW_TILE
PERM_STRIDE = PERM_ROWS // SUBLANES
N_SLABS = D_MODEL // LANES
TILES_PER_STEP = 2
CONV_TILES_PER_STEP = 4
FF_CHUNK = MXU_WIDTH
N_FF_CHUNKS = D_FF // FF_CHUNK
N_MIX_CHUNKS = D_MODEL // FF_CHUNK
PROJ_WIDTH = 2 * FOURIER_WIDTH + QK_WIDTH + KV_WIDTH
N_COND_ROWS = 16
VMEM_LIMIT_BYTES = 48 * 1024 * 1024
CONV_VMEM_LIMIT_BYTES = 56 * 1024 * 1024


def _dot(a, b):
    return jnp.dot(a, b, preferred_element_type=F32)


def _split_bf16(a):
    hi = a.astype(BF16)
    lo = (a - hi.astype(F32)).astype(BF16)
    return hi, lo


def _sigmoid(x):
    return 1.0 / (1.0 + jnp.exp(-x))


def _layer_norm(y, g, b):
    mu = jnp.mean(y, axis=-1, keepdims=True)
    yc = y - mu
    var = jnp.mean(yc * yc, axis=-1, keepdims=True)
    return yc * lax.rsqrt(var + EPS) * g + b


def _params(n_axes):
    return pltpu.CompilerParams(dimension_semantics=("arbitrary",) * n_axes,
                                vmem_limit_bytes=VMEM_LIMIT_BYTES)


def _resident(block, index_map):
    return pl.BlockSpec(block, index_map, pipeline_mode=pl.Buffered(1))


def _mod_kernel(cond_ref, w_ref, b_ref, o_ref):
    c = cond_ref[...]
    s = (c * _sigmoid(c)).astype(BF16)
    o_ref[...] = _dot(s, w_ref[...].astype(BF16)) + b_ref[...]


def _modulation(cond, w_ada, b_ada):
    tn = 1536
    n_out = 6 * D_MODEL
    return pl.pallas_call(
        _mod_kernel,
        grid=(DEPTH, n_out // tn),
        in_specs=[
            pl.BlockSpec((N_COND_ROWS, D_MODEL), lambda l, n: (0, 0)),
            pl.BlockSpec((None, D_MODEL, tn), lambda l, n: (l, 0, n)),
            pl.BlockSpec((None, 1, tn), lambda l, n: (l, 0, n)),
        ],
        out_specs=pl.BlockSpec((None, N_COND_ROWS, tn), lambda l, n: (l, 0, n)),
        out_shape=jax.ShapeDtypeStruct((DEPTH, N_COND_ROWS, n_out), F32),
        compiler_params=_params(2),
        name="modulation",
    )(cond, w_ada, b_ada.reshape(DEPTH, 1, n_out))


def _fold_kernel(w_ref, c_ref, s_ref, o_ref):
    j = pl.program_id(0)
    n_fourier_blocks = FOURIER_WIDTH // MXU_WIDTH

    def dft(m_ref):
        w_hi, w_lo = _split_bf16(w_ref[...])
        m_hi, m_lo = _split_bf16(m_ref[...])
        return _dot(w_hi, m_hi) + _dot(w_hi, m_lo) + _dot(w_lo, m_hi)

    @pl.when(j < n_fourier_blocks)
    def _():
        o_ref[...] = dft(c_ref).astype(BF16)

    @pl.when(jnp.logical_and(j >= n_fourier_blocks, j < 2 * n_fourier_blocks))
    def _():
        o_ref[...] = dft(s_ref).astype(BF16)

    @pl.when(j >= 2 * n_fourier_blocks)
    def _():
        o_ref[...] = w_ref[...].astype(BF16)


def _fold_in_proj(w_in_a, dft_c, dft_s):
    nb = FOURIER_WIDTH // MXU_WIDTH
    return pl.pallas_call(
        _fold_kernel,
        grid=(PROJ_WIDTH // MXU_WIDTH,),
        in_specs=[
            pl.BlockSpec((None, D_MODEL, MXU_WIDTH), lambda j: (0, 0, jnp.where(j < nb, j, j - nb))),
            pl.BlockSpec((MXU_WIDTH, MXU_WIDTH), lambda j: (0, 0)),
            pl.BlockSpec((MXU_WIDTH, MXU_WIDTH), lambda j: (0, 0)),
        ],
        out_specs=pl.BlockSpec((D_MODEL, MXU_WIDTH), lambda j: (0, j)),
        out_shape=jax.ShapeDtypeStruct((D_MODEL, PROJ_WIDTH), BF16),
        compiler_params=_params(1),
        name="fold_in_proj",
    )(w_in_a, dft_c, dft_s)


def _step_blocking(b, s):
    rows = min(s, TILES_PER_STEP * ROW_TILE)
    batches = TILES_PER_STEP * ROW_TILE // rows
    assert s % rows == 0 and b % batches == 0 and rows % ROW_TILE == 0
    return batches, rows, [(bu, ru) for bu in range(batches) for ru in range(0, rows, ROW_TILE)]


def _in_proj_kernel(*refs, rope, tiles):
    if rope:
        (x_ref, mod_ref, w_ref, hs_ref, g_ref, cos_ref, sa_ref, sb_ref,
         fcs_ref, q_ref, k_ref, vt_ref) = refs
    else:
        x_ref, mod_ref, w_ref, hs_ref, g_ref, fcs_ref, q_ref, k_ref, vt_ref, v_ref = refs
    shift = mod_ref[0:1, :]
    scale = mod_ref[1:2, :]
    for bu, ru in tiles:
        rows = slice(ru, ru + ROW_TILE)
        h = (x_ref[bu, rows, :] * (1.0 + scale) + shift).astype(BF16)
        qk_cols = slice(2 * FOURIER_WIDTH, 2 * FOURIER_WIDTH + QK_WIDTH)
        qk = _dot(h, w_ref[:, qk_cols])
        ssq = _dot((qk * qk).astype(BF16), hs_ref[...])
        fcs_ref[bu, rows, :] = _dot(h, w_ref[:, :2 * FOURIER_WIDTH]).astype(BF16)
        v = _dot(h, w_ref[:, 2 * FOURIER_WIDTH + QK_WIDTH:])
        y = qk * lax.rsqrt(ssq * (1.0 / HEAD_DIM) + EPS) * g_ref[...]
        if rope:
            pos = pl.ds(pl.multiple_of(pl.program_id(1) * x_ref.shape[1] + ru, ROW_TILE), ROW_TILE)
            parts = []
            for t in range(QK_WIDTH // LANES):
                yt = y[:, t * LANES:(t + 1) * LANES]
                lo = slice(t * LANES, (t + 1) * LANES)
                parts.append(yt * cos_ref[pos, lo]
                             + pltpu.roll(yt, LANES - 16, 1) * sa_ref[pos, lo]
                             + pltpu.roll(yt, 16, 1) * sb_ref[pos, lo])
            y = jnp.concatenate(parts, axis=1)
        q_ref[bu, rows, :] = (y[:, :ATTN_WIDTH] * (HEAD_DIM ** -0.5 * LOG2_E)).astype(q_ref.dtype)
        k_ref[bu, rows, :] = y[:, ATTN_WIDTH:].astype(k_ref.dtype)
        vt_ref[bu, :, rows] = v.T.astype(vt_ref.dtype)
        if not rope:
            v_ref[bu, rows, :] = v


def _in_proj(x, mod4, mod_row, w_cat, head_sum, gains, rope_tabs):
    b, s, _ = x.shape
    rope = rope_tabs is not None
    bb, tr, tiles = _step_blocking(b, s)
    row_spec = lambda w: pl.BlockSpec((bb, tr, w), lambda bi, i: (bi, i, 0))
    out_specs = [row_spec(2 * FOURIER_WIDTH), row_spec(ATTN_WIDTH), row_spec(KV_WIDTH),
                 pl.BlockSpec((bb, KV_WIDTH, tr), lambda bi, i: (bi, 0, i))]
    out_shape = [
        jax.ShapeDtypeStruct((b, s, 2 * FOURIER_WIDTH), BF16),
        jax.ShapeDtypeStruct((b, s, ATTN_WIDTH), BF16),
        jax.ShapeDtypeStruct((b, s, KV_WIDTH), BF16 if rope else F32),
        jax.ShapeDtypeStruct((b, KV_WIDTH, s), BF16),
    ]
    if not rope:
        out_specs.append(row_spec(KV_WIDTH))
        out_shape.append(jax.ShapeDtypeStruct((b, s, KV_WIDTH), F32))
    in_specs = [
        row_spec(D_MODEL),
        pl.BlockSpec((None, None, 6, D_MODEL), lambda bi, i: (0, mod_row(bi * bb), 0, 0)),
        _resident((D_MODEL, PROJ_WIDTH), lambda bi, i: (0, 0)),
        _resident((QK_WIDTH, QK_WIDTH), lambda bi, i: (0, 0)),
        _resident((1, QK_WIDTH), lambda bi, i: (0, 0)),
    ]
    args = [x, mod4, w_cat, head_sum, gains]
    if rope:
        in_specs += [_resident((s, QK_WIDTH), lambda bi, i: (0, 0))] * 3
        args += list(rope_tabs)
    return pl.pallas_call(
        functools.partial(_in_proj_kernel, rope=rope, tiles=tiles),
        grid=(b // bb, s // tr),
        in_specs=in_specs,
        out_specs=out_specs,
        out_shape=out_shape,
        compiler_params=_params(2),
        name="in_proj_rope" if rope else "in_proj",
    )(*args)


def _attn_kernel(*refs, has_cache, sub_tile):
    if has_cache:
        q_ref, k_ref, vt_ref, ck_ref, cvt_ref, o_ref = refs
    else:
        q_ref, k_ref, vt_ref, o_ref = refs
    nt = (((1,), (1,)), ((), ()))
    ones_rows = jnp.ones((BF16_SUBLANES, KEY_CHUNK), BF16)

    key_sets = [(k_ref, vt_ref)] + ([(ck_ref, cvt_ref)] if has_cache else [])

    items = []
    for t in range(q_ref.shape[0] // sub_tile):
        q = q_ref[t * sub_tile:(t + 1) * sub_tile, :]
        for j in range(N_KV_HEADS):
            kv = slice(j * HEAD_DIM, (j + 1) * HEAD_DIM)
            qs = jnp.concatenate(
                [q[:, h * HEAD_DIM:(h + 1) * HEAD_DIM] for h in range(j * Q_PER_KV, (j + 1) * Q_PER_KV)], axis=0)
            sts = [lax.dot_general(keys_ref[:, kv].astype(BF16), qs, nt, preferred_element_type=F32)
                   for keys_ref, _ in key_sets]
            items.append((t, j, sts))

    partial = {}
    for t, j, sts in items:
        for st_all, (_, vts_ref) in zip(sts, key_sets):
            for c in range(0, st_all.shape[0], KEY_CHUNK):
                st = st_all[c:c + KEY_CHUNK]
                m = jnp.max(st, axis=0, keepdims=True)
                vt_ext = jnp.concatenate(
                    [vts_ref[j * HEAD_DIM:(j + 1) * HEAD_DIM, c:c + KEY_CHUNK].astype(BF16), ones_rows], axis=0)
                ot = _dot(vt_ext, jnp.exp2(st - m).astype(BF16))
                partial.setdefault((t, j), []).append((m, ot))

    for (t, j), parts in partial.items():
        m_all = functools.reduce(jnp.maximum, [m for m, _ in parts])
        ot = (functools.reduce(jnp.add, [jnp.exp2(m - m_all) * o for m, o in parts])
              if len(parts) > 1 else parts[0][1])
        o = ot[0:HEAD_DIM] / ot[HEAD_DIM:HEAD_DIM + 1]
        for g in range(0, Q_PER_KV, 2):
            pair = jnp.concatenate([o[:, g * sub_tile:(g + 1) * sub_tile],
                                    o[:, (g + 1) * sub_tile:(g + 2) * sub_tile]], axis=0).T
            lane0 = (j * Q_PER_KV + g) * HEAD_DIM
            o_ref[t * sub_tile:(t + 1) * sub_tile, lane0:lane0 + LANES] = pair.astype(o_ref.dtype)


def _attention(q, k, vt, cache, tq, sub_tile):
    b, s, _ = q.shape
    has_cache = cache is not None
    seq_spec = lambda n: pl.BlockSpec((None, n, KV_WIDTH), lambda bi, i: (bi, 0, 0))
    seq_t_spec = lambda n: pl.BlockSpec((None, KV_WIDTH, n), lambda bi, i: (bi, 0, 0))
    in_specs = [pl.BlockSpec((None, tq, ATTN_WIDTH), lambda bi, i: (bi, i, 0)), seq_spec(s), seq_t_spec(s)]
    args = [q, k, vt]
    if has_cache:
        past = cache[0].shape[1]
        in_specs += [seq_spec(past), seq_t_spec(past)]
        args += list(cache)
    return pl.pallas_call(
        functools.partial(_attn_kernel, has_cache=has_cache, sub_tile=sub_tile),
        grid=(b, s // tq),
        in_specs=in_specs,
        out_specs=pl.BlockSpec((None, tq, ATTN_WIDTH), lambda bi, i: (bi, i, 0)),
        out_shape=jax.ShapeDtypeStruct((b, s, ATTN_WIDTH), BF16),
        compiler_params=_params(2),
        name="attention_cached" if has_cache else "attention",
    )(*args)


def _out_proj_kernel(x_ref, fcs_ref, dc_ref, ds_ref, attn_ref, w_ref, mod_ref, lng_ref, lnb_ref, o_ref, *, tiles):
    gate = mod_ref[2:3, :]
    for bu, ru in tiles:
        rows = slice(ru, ru + ROW_TILE)
        pos = pl.ds(pl.multiple_of(pl.program_id(1) * x_ref.shape[1] + ru, ROW_TILE), ROW_TILE)
        four = (_dot(dc_ref[pos, :], fcs_ref[bu, :, :FOURIER_WIDTH])
                - _dot(ds_ref[pos, :], fcs_ref[bu, :, FOURIER_WIDTH:]))
        out = (_dot(four.astype(BF16), w_ref[:FOURIER_WIDTH, :])
               + _dot(attn_ref[bu, rows, :], w_ref[FOURIER_WIDTH:, :]))
        y = DEEPNORM_ALPHA * x_ref[bu, rows, :] + gate * out
        o_ref[bu, rows, :] = _layer_norm(y, lng_ref[0:1, :], lnb_ref[0:1, :])


def _out_proj(x, fcs, attn, dft_c, dft_s, w_out, mod4, mod_row, ln_g, ln_b, layer):
    b, s, _ = x.shape
    bb, tr, tiles = _step_blocking(b, s)
    row_spec = lambda w: pl.BlockSpec((bb, tr, w), lambda bi, i: (bi, i, 0))
    ln_spec = pl.BlockSpec((None, 2, D_MODEL), lambda bi, i: (layer, 0, 0))
    return pl.pallas_call(
        functools.partial(_out_proj_kernel, tiles=tiles),
        grid=(b // bb, s // tr),
        in_specs=[
            row_spec(D_MODEL),
            pl.BlockSpec((bb, s, 2 * FOURIER_WIDTH), lambda bi, i: (bi, 0, 0)),
            _resident((s, s), lambda bi, i: (0, 0)),
            _resident((s, s), lambda bi, i: (0, 0)),
            row_spec(ATTN_WIDTH),
            _resident((2 * FOURIER_WIDTH, D_MODEL), lambda bi, i: (0, 0)),
            pl.BlockSpec((None, None, 6, D_MODEL), lambda bi, i: (layer, mod_row(bi * bb), 0, 0)),
            ln_spec, ln_spec,
        ],
        out_specs=row_spec(D_MODEL),
        out_shape=jax.ShapeDtypeStruct((b, s, D_MODEL), F32),
        compiler_params=_params(2),
        name="out_proj",
    )(x, fcs, dft_c, dft_s, attn, w_out, mod4, ln_g, ln_b)


def _load_permuted(slab_ref):
    slabs = [jnp.concatenate([slab_ref[c, pl.ds(v, SUBLANES, stride=PERM_STRIDE), :] for v in range(PERM_STRIDE)],
                             axis=0) for c in range(N_SLABS)]
    return jnp.concatenate(slabs, axis=1)


def _store_unpermuted(y, slab_ref):
    for c in range(N_SLABS):
        for v in range(PERM_STRIDE):
            slab_ref[c, pl.ds(v, SUBLANES, stride=PERM_STRIDE), :] = (
                y[v * SUBLANES:(v + 1) * SUBLANES, c * LANES:(c + 1) * LANES])


def _stage_rows(u, tiles_per_seq, x_ref, xp_ref, xn_ref, shift, scale, xe_ref, h_ref):
    lo = u * ROW_TILE
    sublane = lax.broadcasted_iota(jnp.int32, (SUBLANES, 1), 0)
    for c in range(N_SLABS):
        cols = slice(c * LANES, (c + 1) * LANES)
        before = xp_ref[:, cols] if u == 0 else x_ref[lo - HALO:lo, cols]
        after = xn_ref[0:1, cols] if u == CONV_TILES_PER_STEP - 1 else x_ref[lo + ROW_TILE:lo + ROW_TILE + 1, cols]
        xe_ref[u, c, 0:HALO, :] = jnp.where(sublane == 0, after, before)
        xe_ref[u, c, HALO:PERM_ROWS, :] = x_ref[lo:lo + ROW_TILE, cols]
    if tiles_per_seq == 1:
        has_prev = has_next = 0.0
    else:
        tile = pl.program_id(0) * CONV_TILES_PER_STEP + u
        has_prev = jnp.where(lax.rem(tile, tiles_per_seq) != 0, 1.0, 0.0)
        has_next = jnp.where(lax.rem(tile + 1, tiles_per_seq) != 0, 1.0, 0.0)
    rows = lax.broadcasted_iota(jnp.int32, (PERM_ROWS, 1), 0)
    keep = jnp.where(rows == 0, has_next,
                     jnp.where(rows == SUBLANES * (HALO - 1), has_prev,
                               jnp.where(jnp.logical_and(rows % SUBLANES == 0, rows < SUBLANES * HALO), 0.0, 1.0)))
    h = (_load_permuted(xe_ref.at[u]) * (1.0 + scale) + shift) * keep
    h_ref[u] = h.astype(BF16)


def _conv3_permuted(t, cw):
    first_prev = pltpu.roll(t[PERM_ROWS - SUBLANES:PERM_ROWS], 1, 0)
    prev = jnp.concatenate([first_prev, t[0:PERM_ROWS - SUBLANES]], axis=0)
    last_next = pltpu.roll(t[0:SUBLANES], SUBLANES - 1, 0)
    nxt = jnp.concatenate([t[SUBLANES:PERM_ROWS], last_next], axis=0)
    return cw[0:1, :] * prev + cw[1:2, :] * t + cw[2:3, :] * nxt


def _finish_rows(u, xe_ref, acc, gate, lng, lnb, o_ref):
    y = DEEPNORM_ALPHA * _load_permuted(xe_ref.at[u]) + gate * acc
    _store_unpermuted(_layer_norm(y, lng, lnb), xe_ref.at[u])
    for c in range(N_SLABS):
        o_ref[u * ROW_TILE:(u + 1) * ROW_TILE, c * LANES:(c + 1) * LANES] = xe_ref[u, c, HALO:PERM_ROWS, :]


def _ffn_kernel(x_ref, xp_ref, xn_ref, mod_ref, wup_ref, cw_ref, wdn_ref, lng_ref, lnb_ref, o_ref,
                xe_ref, h_ref, act_ref, *, tiles_per_seq):
    for u in range(CONV_TILES_PER_STEP):
        _stage_rows(u, tiles_per_seq, x_ref, xp_ref, xn_ref, mod_ref[3:4, :], mod_ref[4:5, :], xe_ref, h_ref)
        for j in range(N_FF_CHUNKS):
            a_cols = slice(j * FF_CHUNK, (j + 1) * FF_CHUNK)
            g_cols = slice(D_FF + j * FF_CHUNK, D_FF + (j + 1) * FF_CHUNK)
            a = _conv3_permuted(_dot(h_ref[u], wup_ref[:, a_cols]), cw_ref[:, a_cols])
            g = _conv3_permuted(_dot(h_ref[u], wup_ref[:, g_cols]), cw_ref[:, g_cols])
            act_ref[u, :, a_cols] = (g * _sigmoid(g) * a).astype(BF16)
    for u in range(CONV_TILES_PER_STEP):
        acc = _dot(act_ref[u], wdn_ref[...])
        _finish_rows(u, xe_ref, acc, mod_ref[5:6, :], lng_ref[1:2, :], lnb_ref[1:2, :], o_ref)


def _mixer_kernel(x_ref, xp_ref, xn_ref, mod_ref, win_ref, cw_ref, wout_ref, lng_ref, lnb_ref, o_ref,
                  xe_ref, h_ref, act_ref, *, tiles_per_seq):
    for u in range(CONV_TILES_PER_STEP):
        _stage_rows(u, tiles_per_seq, x_ref, xp_ref, xn_ref, mod_ref[0:1, :], mod_ref[1:2, :], xe_ref, h_ref)
        for j in range(N_MIX_CHUNKS):
            cols = slice(j * FF_CHUNK, (j + 1) * FF_CHUNK)
            bg, cg, xin = (_dot(h_ref[u], win_ref[:, k * D_MODEL + j * FF_CHUNK:k * D_MODEL + (j + 1) * FF_CHUNK])
                           for k in range(3))
            y = bg * _conv3_permuted(cg * xin, cw_ref[:, cols])
            act_ref[u, :, cols] = y.astype(BF16)
    for u in range(CONV_TILES_PER_STEP):
        acc = _dot(act_ref[u], wout_ref[...])
        _finish_rows(u, xe_ref, acc, mod_ref[2:3, :], lng_ref[0:1, :], lnb_ref[0:1, :], o_ref)


def _gated_conv_block(kernel_fn, name, x, mod4, mod_row, w1, cw, w2, ln_g, ln_b, layer, w_layer):
    b, s, _ = x.shape
    hidden = w2.shape[1]
    n_rows = b * s
    step_rows = CONV_TILES_PER_STEP * ROW_TILE
    assert s % ROW_TILE == 0 and (s % step_rows == 0 or step_rows % s == 0)
    halo_blocks_per_step = step_rows // HALO
    last_halo_block = n_rows // HALO - 1
    ln_spec = pl.BlockSpec((None, 2, D_MODEL), lambda i: (layer, 0, 0))
    out = pl.pallas_call(
        functools.partial(kernel_fn, tiles_per_seq=s // ROW_TILE),
        grid=(n_rows // step_rows,),
        in_specs=[
            pl.BlockSpec((step_rows, D_MODEL), lambda i: (i, 0)),
            pl.BlockSpec((HALO, D_MODEL), lambda i: (jnp.maximum(i * halo_blocks_per_step - 1, 0), 0)),
            pl.BlockSpec((HALO, D_MODEL),
                         lambda i: (jnp.minimum((i + 1) * halo_blocks_per_step, last_halo_block), 0)),
            pl.BlockSpec((None, None, 6, D_MODEL), lambda i: (layer, mod_row(i * step_rows // s), 0, 0)),
            _resident((None,) + w1.shape[1:], lambda i: (w_layer, 0, 0)),
            _resident((None,) + cw.shape[1:], lambda i: (w_layer, 0, 0)),
            _resident((None,) + w2.shape[1:], lambda i: (w_layer, 0, 0)),
            ln_spec, ln_spec,
        ],
        out_specs=pl.BlockSpec((step_rows, D_MODEL), lambda i: (i, 0)),
        out_shape=jax.ShapeDtypeStruct((n_rows, D_MODEL), F32),
        scratch_shapes=[
            pltpu.VMEM((CONV_TILES_PER_STEP, N_SLABS, PERM_ROWS, LANES), F32),
            pltpu.VMEM((CONV_TILES_PER_STEP, PERM_ROWS, D_MODEL), BF16),
            pltpu.VMEM((CONV_TILES_PER_STEP, PERM_ROWS, hidden), BF16),
        ],
        compiler_params=pltpu.CompilerParams(dimension_semantics=("arbitrary",),
                                             vmem_limit_bytes=CONV_VMEM_LIMIT_BYTES),
        name=name,
    )(*([x.reshape(n_rows, D_MODEL)] * 3), mod4, w1, cw, w2, ln_g, ln_b)
    return out.reshape(b, s, D_MODEL)


def _dft_tables(n, scale):
    jk = np.outer(np.arange(n), np.arange(n)) % n
    ang = 2.0 * np.pi * jk / n
    return (np.cos(ang) * scale).astype(np.float32), (np.sin(ang) * scale).astype(np.float32)


def _channel_dft_blocks():
    c, s = _dft_tables(FOURIER_GROUP, 1.0)
    eye = np.eye(MXU_WIDTH // FOURIER_GROUP, dtype=np.float32)
    return np.kron(eye, c), np.kron(eye, s)


def _rope_tables(n):
    half = HEAD_DIM // 2
    inv = 1.0 / (ROPE_THETA ** (np.arange(0, half, 2, dtype=np.float64) / half))
    pos = np.arange(n)
    d = np.arange(QK_WIDTH) % HEAD_DIM
    coord = np.where(d < half, (pos // GRID_W)[:, None], (pos % GRID_W)[:, None])
    ang = coord * inv[d % (half // 2)][None, :]
    first = (d % half) < (half // 2)
    cos = np.cos(ang)
    sa = np.where(first[None, :], -np.sin(ang), 0.0)
    sb = np.where(first[None, :], 0.0, np.sin(ang))
    return tuple(jnp.asarray(t.astype(np.float32)) for t in (cos, sa, sb))


def _head_sum_matrix():
    return np.kron(np.eye(QK_WIDTH // HEAD_DIM, dtype=np.float32), np.ones((HEAD_DIM, HEAD_DIM), np.float32))


def kernel(x_prompt, x_sample, cache_k, cache_v, c, c_ctx, w_ada, b_ada, ln_g, ln_b, w_in_a, q_norm_g,
           k_norm_g, w_out_a, w_in_c, conv_c, w_out_c, w_up, conv_f, w_down):
    n_prompt, s_prompt, _ = x_prompt.shape
    n_sample, s_sample, _ = x_sample.shape
    past = cache_k.shape[2]

    cond = jnp.concatenate(
        [c_ctx[None, :], c, jnp.zeros((N_COND_ROWS - 1 - n_sample, D_MODEL), F32)], axis=0)
    mod4 = _modulation(cond, w_ada, b_ada).reshape(DEPTH, N_COND_ROWS, 6, D_MODEL)

    chan_c, chan_s = _channel_dft_blocks()
    w_cat = _fold_in_proj(w_in_a, jnp.asarray(chan_c), jnp.asarray(chan_s))
    head_sum = jnp.asarray(_head_sum_matrix()).astype(BF16)
    gains = jnp.concatenate([jnp.tile(q_norm_g[0], N_HEADS), jnp.tile(k_norm_g[0], N_KV_HEADS)])[None, :]
    w_out_bf = w_out_a[0].astype(BF16)

    w_up_bf, w_down_bf = w_up.astype(BF16), w_down.astype(BF16)
    w_in_c_bf, w_out_c_bf = w_in_c.astype(BF16), w_out_c.astype(BF16)

    def run_stream(x, mod_row, cache):
        s = x.shape[1]
        scale = (FOURIER_GROUP * s) ** -0.5
        dft_c, dft_s = (jnp.asarray(t).astype(BF16) for t in _dft_tables(s, scale))
        rope_tabs = _rope_tables(s) if cache is not None else None
        fcs, q, k, vt, *v = _in_proj(x, mod4, mod_row, w_cat, head_sum, gains, rope_tabs)
        attn = _attention(q, k, vt, cache, ATTN_Q_TILE, ATTN_SUB_TILE)
        x = _out_proj(x, fcs, attn, dft_c, dft_s, w_out_bf, mod4, mod_row, ln_g, ln_b, 0)
        x = _gated_conv_block(_ffn_kernel, "conv_ffn", x, mod4, mod_row, w_up_bf, conv_f, w_down_bf,
                              ln_g, ln_b, 0, 0)
        x = _gated_conv_block(_mixer_kernel, "conv_mixer", x, mod4, mod_row, w_in_c_bf, conv_c, w_out_c_bf,
                              ln_g, ln_b, 1, 0)
        x = _gated_conv_block(_ffn_kernel, "conv_ffn", x, mod4, mod_row, w_up_bf, conv_f, w_down_bf,
                              ln_g, ln_b, 1, 1)
        return x, k, v

    y_prompt, k_new, (v_new,) = run_stream(x_prompt, lambda bi: 0, None)
    cache = (cache_k[:, 0].reshape(n_sample, past, KV_WIDTH),
             cache_v[:, 0].reshape(n_sample, past, KV_WIDTH).transpose(0, 2, 1))
    y_sample, _, _ = run_stream(x_sample, lambda bi: bi + 1, cache)

    new_shape = (n_prompt, 1, s_prompt, N_KV_HEADS, HEAD_DIM)
    return y_prompt, y_sample, k_new.reshape(new_shape), v_new.reshape(new_shape)
```

```python
import functools

import numpy as np
import jax
import jax.numpy as jnp
from jax import lax
from jax.experimental import pallas as pl
from jax.experimental.pallas import tpu as pltpu

D_MODEL = 1024
DEPTH = 2
GRID_W = 64
HEAD_DIM = 64
N_HEADS = 8
N_KV_HEADS = 2
Q_PER_KV = N_HEADS // N_KV_HEADS
ATTN_WIDTH = N_HEADS * HEAD_DIM
KV_WIDTH = N_KV_HEADS * HEAD_DIM
QK_WIDTH = ATTN_WIDTH + KV_WIDTH
FOURIER_GROUP = 64
FOURIER_WIDTH = 512
D_FF = 2816
ROPE_THETA = 10000.0
EPS = 1e-6
DEEPNORM_ALPHA = (2 * DEPTH) ** 0.25
LOG2_E = 1.4426950408889634

F32 = jnp.float32
BF16 = jnp.bfloat16

SUBLANES = 8
BF16_SUBLANES = 16
LANES = 128
MXU_WIDTH = 256

ROW_TILE = 256
ATTN_SUB_TILE = 128
KEY_CHUNK = MXU_WIDTH
HALO = SUBLANES
PERM_ROWS = HALO + ROW_TILE
PERM_STRIDE = PERM_ROWS // SUBLANES
N_SLABS = D_MODEL // LANES
TILES_PER_STEP = 2
CONV_TILES_PER_STEP = 4
FF_CHUNK = MXU_WIDTH
N_FF_CHUNKS = D_FF // FF_CHUNK
N_MIX_CHUNKS = D_MODEL // FF_CHUNK
PROJ_WIDTH = 2 * FOURIER_WIDTH + QK_WIDTH + KV_WIDTH
N_COND_ROWS = 16
VMEM_LIMIT_BYTES = 48 * 1024 * 1024
CONV_VMEM_LIMIT_BYTES = 56 * 1024 * 1024


def _dot(a, b):
    return jnp.dot(a, b, preferred_element_type=F32)


def _split_bf16(a):
    hi = a.astype(BF16)
    lo = (a - hi.astype(F32)).astype(BF16)
    return hi, lo


def _sigmoid(x):
    return 1.0 / (1.0 + jnp.exp(-x))


def _layer_norm(y, g, b):
    mu = jnp.mean(y, axis=-1, keepdims=True)
    yc = y - mu
    var = jnp.mean(yc * yc, axis=-1, keepdims=True)
    return yc * lax.rsqrt(var + EPS) * g + b


def _params(n_axes):
    return pltpu.CompilerParams(dimension_semantics=("arbitrary",) * n_axes,
                                vmem_limit_bytes=VMEM_LIMIT_BYTES)


def _resident(block, index_map):
    return pl.BlockSpec(block, index_map, pipeline_mode=pl.Buffered(1))


def _mod_kernel(cond_ref, w_ref, b_ref, o_ref):
    c = cond_ref[...]
    s = (c * _sigmoid(c)).astype(BF16)
    o_ref[...] = _dot(s, w_ref[...].astype(BF16)) + b_ref[...]


def _modulation(cond, w_ada, b_ada):
    tn = 1536
    n_out = 6 * D_MODEL
    return pl.pallas_call(
        _mod_kernel,
        grid=(DEPTH, n_out // tn),
        in_specs=[
            pl.BlockSpec((N_COND_ROWS, D_MODEL), lambda l, n: (0, 0)),
            pl.BlockSpec((None, D_MODEL, tn), lambda l, n: (l, 0, n)),
            pl.BlockSpec((None, 1, tn), lambda l, n: (l, 0, n)),
        ],
        out_specs=pl.BlockSpec((None, N_COND_ROWS, tn), lambda l, n: (l, 0, n)),
        out_shape=jax.ShapeDtypeStruct((DEPTH, N_COND_ROWS, n_out), F32),
        compiler_params=_params(2),
        name="modulation",
    )(cond, w_ada, b_ada.reshape(DEPTH, 1, n_out))


def _fold_kernel(w_ref, c_ref, s_ref, o_ref):
    j = pl.program_id(0)
    n_fourier_blocks = FOURIER_WIDTH // MXU_WIDTH

    def dft(m_ref):
        w_hi, w_lo = _split_bf16(w_ref[...])
        m_hi, m_lo = _split_bf16(m_ref[...])
        return _dot(w_hi, m_hi) + _dot(w_hi, m_lo) + _dot(w_lo, m_hi)

    @pl.when(j < n_fourier_blocks)
    def _():
        o_ref[...] = dft(c_ref).astype(BF16)

    @pl.when(jnp.logical_and(j >= n_fourier_blocks, j < 2 * n_fourier_blocks))
    def _():
        o_ref[...] = dft(s_ref).astype(BF16)

    @pl.when(j >= 2 * n_fourier_blocks)
    def _():
        o_ref[...] = w_ref[...].astype(BF16)


def _fold_in_proj(w_in_a, dft_c, dft_s):
    nb = FOURIER_WIDTH // MXU_WIDTH
    return pl.pallas_call(
        _fold_kernel,
        grid=(PROJ_WIDTH // MXU_WIDTH,),
        in_specs=[
            pl.BlockSpec((None, D_MODEL, MXU_WIDTH), lambda j: (0, 0, jnp.where(j < nb, j, j - nb))),
            pl.BlockSpec((MXU_WIDTH, MXU_WIDTH), lambda j: (0, 0)),
            pl.BlockSpec((MXU_WIDTH, MXU_WIDTH), lambda j: (0, 0)),
        ],
        out_specs=pl.BlockSpec((D_MODEL, MXU_WIDTH), lambda j: (0, j)),
        out_shape=jax.ShapeDtypeStruct((D_MODEL, PROJ_WIDTH), BF16),
        compiler_params=_params(1),
        name="fold_in_proj",
    )(w_in_a, dft_c, dft_s)


def _step_blocking(b, s):
    rows = min(s, TILES_PER_STEP * ROW_TILE)
    batches = TILES_PER_STEP * ROW_TILE // rows
    assert s % rows == 0 and b % batches == 0 and rows % ROW_TILE == 0
    return batches, rows, [(bu, ru) for bu in range(batches) for ru in range(0, rows, ROW_TILE)]


def _in_proj_kernel(*refs, rope, tiles):
    if rope:
        (x_ref, mod_ref, w_ref, hs_ref, g_ref, cos_ref, sa_ref, sb_ref,
         fcs_ref, q_ref, k_ref, vt_ref) = refs
    else:
        x_ref, mod_ref, w_ref, hs_ref, g_ref, fcs_ref, q_ref, k_ref, vt_ref, v_ref = refs
    shift = mod_ref[0:1, :]
    scale = mod_ref[1:2, :]
    for bu, ru in tiles:
        rows = slice(ru, ru + ROW_TILE)
        h = (x_ref[bu, rows, :] * (1.0 + scale) + shift).astype(BF16)
        qk_cols = slice(2 * FOURIER_WIDTH, 2 * FOURIER_WIDTH + QK_WIDTH)
        qk = _dot(h, w_ref[:, qk_cols])
        ssq = _dot((qk * qk).astype(BF16), hs_ref[...])
        fcs_ref[bu, rows, :] = _dot(h, w_ref[:, :2 * FOURIER_WIDTH]).astype(BF16)
        v = _dot(h, w_ref[:, 2 * FOURIER_WIDTH + QK_WIDTH:])
        y = qk * lax.rsqrt(ssq * (1.0 / HEAD_DIM) + EPS) * g_ref[...]
        if rope:
            pos = pl.ds(pl.multiple_of(pl.program_id(1) * x_ref.shape[1] + ru, ROW_TILE), ROW_TILE)
            parts = []
            for t in range(QK_WIDTH // LANES):
                yt = y[:, t * LANES:(t + 1) * LANES]
                lo = slice(t * LANES, (t + 1) * LANES)
                parts.append(yt * cos_ref[pos, lo]
                             + pltpu.roll(yt, LANES - 16, 1) * sa_ref[pos, lo]
                             + pltpu.roll(yt, 16, 1) * sb_ref[pos, lo])
            y = jnp.concatenate(parts, axis=1)
        q_ref[bu, rows, :] = (y[:, :ATTN_WIDTH] * (HEAD_DIM ** -0.5 * LOG2_E)).astype(q_ref.dtype)
        k_ref[bu, rows, :] = y[:, ATTN_WIDTH:].astype(k_ref.dtype)
        vt_ref[bu, :, rows] = v.T.astype(vt_ref.dtype)
        if not rope:
            v_ref[bu, rows, :] = v


def _in_proj(x, mod4, mod_row, w_cat, head_sum, gains, rope_tabs):
    b, s, _ = x.shape
    rope = rope_tabs is not None
    bb, tr, tiles = _step_blocking(b, s)
    row_spec = lambda w: pl.BlockSpec((bb, tr, w), lambda bi, i: (bi, i, 0))
    out_specs = [row_spec(2 * FOURIER_WIDTH), row_spec(ATTN_WIDTH), row_spec(KV_WIDTH),
                 pl.BlockSpec((bb, KV_WIDTH, tr), lambda bi, i: (bi, 0, i))]
    out_shape = [
        jax.ShapeDtypeStruct((b, s, 2 * FOURIER_WIDTH), BF16),
        jax.ShapeDtypeStruct((b, s, ATTN_WIDTH), BF16),
        jax.ShapeDtypeStruct((b, s, KV_WIDTH), BF16 if rope else F32),
        jax.ShapeDtypeStruct((b, KV_WIDTH, s), BF16),
    ]
    if not rope:
        out_specs.append(row_spec(KV_WIDTH))
        out_shape.append(jax.ShapeDtypeStruct((b, s, KV_WIDTH), F32))
    in_specs = [
        row_spec(D_MODEL),
        pl.BlockSpec((None, None, 6, D_MODEL), lambda bi, i: (0, mod_row(bi * bb), 0, 0)),
        _resident((D_MODEL, PROJ_WIDTH), lambda bi, i: (0, 0)),
        _resident((QK_WIDTH, QK_WIDTH), lambda bi, i: (0, 0)),
        _resident((1, QK_WIDTH), lambda bi, i: (0, 0)),
    ]
    args = [x, mod4, w_cat, head_sum, gains]
    if rope:
        in_specs += [_resident((s, QK_WIDTH), lambda bi, i: (0, 0))] * 3
        args += list(rope_tabs)
    return pl.pallas_call(
        functools.partial(_in_proj_kernel, rope=rope, tiles=tiles),
        grid=(b // bb, s // tr),
        in_specs=in_specs,
        out_specs=out_specs,
        out_shape=out_shape,
        compiler_params=_params(2),
        name="in_proj_rope" if rope else "in_proj",
    )(*args)


def _attention_tile(q_ref, bu, row0, key_sets, attn_ref, arow0, sub_tile):
    nt = (((1,), (1,)), ((), ()))
    ones_rows = jnp.ones((BF16_SUBLANES, KEY_CHUNK), BF16)

    items = []
    for t in range(ROW_TILE // sub_tile):
        q = q_ref[bu, row0 + t * sub_tile:row0 + (t + 1) * sub_tile, :]
        for j in range(N_KV_HEADS):
            kv = slice(j * HEAD_DIM, (j + 1) * HEAD_DIM)
            qs = jnp.concatenate(
                [q[:, h * HEAD_DIM:(h + 1) * HEAD_DIM] for h in range(j * Q_PER_KV, (j + 1) * Q_PER_KV)], axis=0)
            sts = [lax.dot_general(keys_ref[bu, :, kv].astype(BF16), qs, nt, preferred_element_type=F32)
                   for keys_ref, _ in key_sets]
            items.append((t, j, sts))

    def finish():
        for t, j, sts in items:
            parts = []
            for st_all, (_, vts_ref) in zip(sts, key_sets):
                for c in range(0, st_all.shape[0], KEY_CHUNK):
                    st = st_all[c:c + KEY_CHUNK]
                    m = jnp.max(st, axis=0, keepdims=True)
                    vt_ext = jnp.concatenate(
                        [vts_ref[bu, j * HEAD_DIM:(j + 1) * HEAD_DIM, c:c + KEY_CHUNK].astype(BF16), ones_rows],
                        axis=0)
                    parts.append((m, _dot(vt_ext, jnp.exp2(st - m).astype(BF16))))
            m_all = functools.reduce(jnp.maximum, [m for m, _ in parts])
            ot = (functools.reduce(jnp.add, [jnp.exp2(m - m_all) * o for m, o in parts])
                  if len(parts) > 1 else parts[0][1])
            o = ot[0:HEAD_DIM] / ot[HEAD_DIM:HEAD_DIM + 1]
            for g in range(0, Q_PER_KV, 2):
                pair = jnp.concatenate([o[:, g * sub_tile:(g + 1) * sub_tile],
                                        o[:, (g + 1) * sub_tile:(g + 2) * sub_tile]], axis=0).T
                lane0 = (j * Q_PER_KV + g) * HEAD_DIM
                attn_ref[arow0 + t * sub_tile:arow0 + (t + 1) * sub_tile, lane0:lane0 + LANES] = (
                    pair.astype(attn_ref.dtype))

    return finish


def _attn_out_kernel(*refs, has_cache, tiles, sub_tile):
    if has_cache:
        (q_ref, k_ref, vt_ref, ck_ref, cvt_ref, x_ref, fcs_ref, dc_ref, ds_ref, w_ref, mod_ref, lng_ref, lnb_ref,
         o_ref, attn_ref) = refs
        key_sets = [(k_ref, vt_ref), (ck_ref, cvt_ref)]
    else:
        (q_ref, k_ref, vt_ref, x_ref, fcs_ref, dc_ref, ds_ref, w_ref, mod_ref, lng_ref, lnb_ref,
         o_ref, attn_ref) = refs
        key_sets = [(k_ref, vt_ref)]
    gate = mod_ref[2:3, :]
    for n, (bu, ru) in enumerate(tiles):
        rows = slice(ru, ru + ROW_TILE)
        arows = slice(n * ROW_TILE, (n + 1) * ROW_TILE)
        finish_attention = _attention_tile(q_ref, bu, ru, key_sets, attn_ref, n * ROW_TILE, sub_tile)
        pos = pl.ds(pl.multiple_of(pl.program_id(1) * x_ref.shape[1] + ru, ROW_TILE), ROW_TILE)
        four = (_dot(dc_ref[pos, :], fcs_ref[bu, :, :FOURIER_WIDTH])
                - _dot(ds_ref[pos, :], fcs_ref[bu, :, FOURIER_WIDTH:]))
        finish_attention()
        out = (_dot(four.astype(BF16), w_ref[:FOURIER_WIDTH, :])
               + _dot(attn_ref[arows, :], w_ref[FOURIER_WIDTH:, :]))
        y = DEEPNORM_ALPHA * x_ref[bu, rows, :] + gate * out
        o_ref[bu, rows, :] = _layer_norm(y, lng_ref[0:1, :], lnb_ref[0:1, :])


def _attn_out(x, fcs, q, k, vt, cache, dft_c, dft_s, w_out, mod4, mod_row, ln_g, ln_b, layer):
    b, s, _ = x.shape
    has_cache = cache is not None
    bb, tr, tiles = _step_blocking(b, s)
    row_spec = lambda w: pl.BlockSpec((bb, tr, w), lambda bi, i: (bi, i, 0))
    seq_spec = lambda n, w: pl.BlockSpec((bb, n, w), lambda bi, i: (bi, 0, 0))
    seq_t_spec = lambda n: pl.BlockSpec((bb, KV_WIDTH, n), lambda bi, i: (bi, 0, 0))
    ln_spec = pl.BlockSpec((None, 2, D_MODEL), lambda bi, i: (layer, 0, 0))
    in_specs = [row_spec(ATTN_WIDTH), seq_spec(s, KV_WIDTH), seq_t_spec(s)]
    args = [q, k, vt]
    if has_cache:
        past = cache[0].shape[1]
        in_specs += [seq_spec(past, KV_WIDTH), seq_t_spec(past)]
        args += list(cache)
    in_specs += [
        row_spec(D_MODEL),
        seq_spec(s, 2 * FOURIER_WIDTH),
        _resident((s, s), lambda bi, i: (0, 0)),
        _resident((s, s), lambda bi, i: (0, 0)),
        _resident((2 * FOURIER_WIDTH, D_MODEL), lambda bi, i: (0, 0)),
        pl.BlockSpec((None, None, 6, D_MODEL), lambda bi, i: (layer, mod_row(bi * bb), 0, 0)),
        ln_spec, ln_spec,
    ]
    args += [x, fcs, dft_c, dft_s, w_out, mod4, ln_g, ln_b]
    return pl.pallas_call(
        functools.partial(_attn_out_kernel, has_cache=has_cache, tiles=tiles, sub_tile=ATTN_SUB_TILE),
        grid=(b // bb, s // tr),
        in_specs=in_specs,
        out_specs=row_spec(D_MODEL),
        out_shape=jax.ShapeDtypeStruct((b, s, D_MODEL), F32),
        scratch_shapes=[pltpu.VMEM((bb * tr, ATTN_WIDTH), BF16)],
        compiler_params=_params(2),
        name="attn_out_cached" if has_cache else "attn_out",
    )(*args)


def _load_permuted(slab_ref):
    slabs = [jnp.concatenate([slab_ref[c, pl.ds(v, SUBLANES, stride=PERM_STRIDE), :] for v in range(PERM_STRIDE)],
                             axis=0) for c in range(N_SLABS)]
    return jnp.concatenate(slabs, axis=1)


def _store_unpermuted(y, slab_ref):
    for c in range(N_SLABS):
        for v in range(PERM_STRIDE):
            slab_ref[c, pl.ds(v, SUBLANES, stride=PERM_STRIDE), :] = (
                y[v * SUBLANES:(v + 1) * SUBLANES, c * LANES:(c + 1) * LANES])


def _stage_rows(u, tiles_per_seq, x_ref, xp_ref, xn_ref, shift, scale, xe_ref, h_ref):
    lo = u * ROW_TILE
    sublane = lax.broadcasted_iota(jnp.int32, (SUBLANES, 1), 0)
    for c in range(N_SLABS):
        cols = slice(c * LANES, (c + 1) * LANES)
        before = xp_ref[:, cols] if u == 0 else x_ref[lo - HALO:lo, cols]
        after = xn_ref[0:1, cols] if u == CONV_TILES_PER_STEP - 1 else x_ref[lo + ROW_TILE:lo + ROW_TILE + 1, cols]
        xe_ref[u, c, 0:HALO, :] = jnp.where(sublane == 0, after, before)
        xe_ref[u, c, HALO:PERM_ROWS, :] = x_ref[lo:lo + ROW_TILE, cols]
    if tiles_per_seq == 1:
        has_prev = has_next = 0.0
    else:
        tile = pl.program_id(0) * CONV_TILES_PER_STEP + u
        has_prev = jnp.where(lax.rem(tile, tiles_per_seq) != 0, 1.0, 0.0)
        has_next = jnp.where(lax.rem(tile + 1, tiles_per_seq) != 0, 1.0, 0.0)
    rows = lax.broadcasted_iota(jnp.int32, (PERM_ROWS, 1), 0)
    keep = jnp.where(rows == 0, has_next,
                     jnp.where(rows == SUBLANES * (HALO - 1), has_prev,
                               jnp.where(jnp.logical_and(rows % SUBLANES == 0, rows < SUBLANES * HALO), 0.0, 1.0)))
    h = (_load_permuted(xe_ref.at[u]) * (1.0 + scale) + shift) * keep
    h_ref[u] = h.astype(BF16)


def _conv3_permuted(t, cw):
    first_prev = pltpu.roll(t[PERM_ROWS - SUBLANES:PERM_ROWS], 1, 0)
    prev = jnp.concatenate([first_prev, t[0:PERM_ROWS - SUBLANES]], axis=0)
    last_next = pltpu.roll(t[0:SUBLANES], SUBLANES - 1, 0)
    nxt = jnp.concatenate([t[SUBLANES:PERM_ROWS], last_next], axis=0)
    return cw[0:1, :] * prev + cw[1:2, :] * t + cw[2:3, :] * nxt


def _finish_rows(u, xe_ref, acc, gate, lng, lnb, o_ref):
    y = DEEPNORM_ALPHA * _load_permuted(xe_ref.at[u]) + gate * acc
    _store_unpermuted(_layer_norm(y, lng, lnb), xe_ref.at[u])
    for c in range(N_SLABS):
        o_ref[u * ROW_TILE:(u + 1) * ROW_TILE, c * LANES:(c + 1) * LANES] = xe_ref[u, c, HALO:PERM_ROWS, :]


def _ffn_kernel(x_ref, xp_ref, xn_ref, mod_ref, wup_ref, cw_ref, wdn_ref, lng_ref, lnb_ref, o_ref,
                xe_ref, h_ref, act_ref, *, tiles_per_seq):
    for u in range(CONV_TILES_PER_STEP):
        _stage_rows(u, tiles_per_seq, x_ref, xp_ref, xn_ref, mod_ref[3:4, :], mod_ref[4:5, :], xe_ref, h_ref)
        for j in range(N_FF_CHUNKS):
            a_cols = slice(j * FF_CHUNK, (j + 1) * FF_CHUNK)
            g_cols = slice(D_FF + j * FF_CHUNK, D_FF + (j + 1) * FF_CHUNK)
            a = _conv3_permuted(_dot(h_ref[u], wup_ref[:, a_cols]), cw_ref[:, a_cols])
            g = _conv3_permuted(_dot(h_ref[u], wup_ref[:, g_cols]), cw_ref[:, g_cols])
            act_ref[u, :, a_cols] = (g * _sigmoid(g) * a).astype(BF16)
    for u in range(CONV_TILES_PER_STEP):
        acc = _dot(act_ref[u], wdn_ref[...])
        _finish_rows(u, xe_ref, acc, mod_ref[5:6, :], lng_ref[1:2, :], lnb_ref[1:2, :], o_ref)


def _mixer_kernel(x_ref, xp_ref, xn_ref, mod_ref, win_ref, cw_ref, wout_ref, lng_ref, lnb_ref, o_ref,
                  xe_ref, h_ref, act_ref, *, tiles_per_seq):
    for u in range(CONV_TILES_PER_STEP):
        _stage_rows(u, tiles_per_seq, x_ref, xp_ref, xn_ref, mod_ref[0:1, :], mod_ref[1:2, :], xe_ref, h_ref)
        for j in range(N_MIX_CHUNKS):
            cols = slice(j * FF_CHUNK, (j + 1) * FF_CHUNK)
            bg, cg, xin = (_dot(h_ref[u], win_ref[:, k * D_MODEL + j * FF_CHUNK:k * D_MODEL + (j + 1) * FF_CHUNK])
                           for k in range(3))
            y = bg * _conv3_permuted(cg * xin, cw_ref[:, cols])
            act_ref[u, :, cols] = y.astype(BF16)
    for u in range(CONV_TILES_PER_STEP):
        acc = _dot(act_ref[u], wout_ref[...])
        _finish_rows(u, xe_ref, acc, mod_ref[2:3, :], lng_ref[0:1, :], lnb_ref[0:1, :], o_ref)


def _gated_conv_block(kernel_fn, name, x, mod4, mod_row, w1, cw, w2, ln_g, ln_b, layer, w_layer):
    b, s, _ = x.shape
    hidden = w2.shape[1]
    n_rows = b * s
    step_rows = CONV_TILES_PER_STEP * ROW_TILE
    assert s % ROW_TILE == 0 and (s % step_rows == 0 or step_rows % s == 0)
    halo_blocks_per_step = step_rows // HALO
    last_halo_block = n_rows // HALO - 1
    ln_spec = pl.BlockSpec((None, 2, D_MODEL), lambda i: (layer, 0, 0))
    out = pl.pallas_call(
        functools.partial(kernel_fn, tiles_per_seq=s // ROW_TILE),
        grid=(n_rows // step_rows,),
        in_specs=[
            pl.BlockSpec((step_rows, D_MODEL), lambda i: (i, 0)),
            pl.BlockSpec((HALO, D_MODEL), lambda i: (jnp.maximum(i * halo_blocks_per_step - 1, 0), 0)),
            pl.BlockSpec((HALO, D_MODEL),
                         lambda i: (jnp.minimum((i + 1) * halo_blocks_per_step, last_halo_block), 0)),
            pl.BlockSpec((None, None, 6, D_MODEL), lambda i: (layer, mod_row(i * step_rows // s), 0, 0)),
            _resident((None,) + w1.shape[1:], lambda i: (w_layer, 0, 0)),
            _resident((None,) + cw.shape[1:], lambda i: (w_layer, 0, 0)),
            _resident((None,) + w2.shape[1:], lambda i: (w_layer, 0, 0)),
            ln_spec, ln_spec,
        ],
        out_specs=pl.BlockSpec((step_rows, D_MODEL), lambda i: (i, 0)),
        out_shape=jax.ShapeDtypeStruct((n_rows, D_MODEL), F32),
        scratch_shapes=[
            pltpu.VMEM((CONV_TILES_PER_STEP, N_SLABS, PERM_ROWS, LANES), F32),
            pltpu.VMEM((CONV_TILES_PER_STEP, PERM_ROWS, D_MODEL), BF16),
            pltpu.VMEM((CONV_TILES_PER_STEP, PERM_ROWS, hidden), BF16),
        ],
        compiler_params=pltpu.CompilerParams(dimension_semantics=("arbitrary",),
                                             vmem_limit_bytes=CONV_VMEM_LIMIT_BYTES),
        name=name,
    )(*([x.reshape(n_rows, D_MODEL)] * 3), mod4, w1, cw, w2, ln_g, ln_b)
    return out.reshape(b, s, D_MODEL)


def _dft_tables(n, scale):
    jk = np.outer(np.arange(n), np.arange(n)) % n
    ang = 2.0 * np.pi * jk / n
    return (np.cos(ang) * scale).astype(np.float32), (np.sin(ang) * scale).astype(np.float32)


def _channel_dft_blocks():
    c, s = _dft_tables(FOURIER_GROUP, 1.0)
    eye = np.eye(MXU_WIDTH // FOURIER_GROUP, dtype=np.float32)
    return np.kron(eye, c), np.kron(eye, s)


def _rope_tables(n):
    half = HEAD_DIM // 2
    inv = 1.0 / (ROPE_THETA ** (np.arange(0, half, 2, dtype=np.float64) / half))
    pos = np.arange(n)
    d = np.arange(QK_WIDTH) % HEAD_DIM
    coord = np.where(d < half, (pos // GRID_W)[:, None], (pos % GRID_W)[:, None])
    ang = coord * inv[d % (half // 2)][None, :]
    first = (d % half) < (half // 2)
    cos = np.cos(ang)
    sa = np.where(first[None, :], -np.sin(ang), 0.0)
    sb = np.where(first[None, :], 0.0, np.sin(ang))
    return tuple(jnp.asarray(t.astype(np.float32)) for t in (cos, sa, sb))


def _head_sum_matrix():
    return np.kron(np.eye(QK_WIDTH // HEAD_DIM, dtype=np.float32), np.ones((HEAD_DIM, HEAD_DIM), np.float32))


def kernel(x_prompt, x_sample, cache_k, cache_v, c, c_ctx, w_ada, b_ada, ln_g, ln_b, w_in_a, q_norm_g,
           k_norm_g, w_out_a, w_in_c, conv_c, w_out_c, w_up, conv_f, w_down):
    n_prompt, s_prompt, _ = x_prompt.shape
    n_sample, s_sample, _ = x_sample.shape
    past = cache_k.shape[2]

    cond = jnp.concatenate(
        [c_ctx[None, :], c, jnp.zeros((N_COND_ROWS - 1 - n_sample, D_MODEL), F32)], axis=0)
    mod4 = _modulation(cond, w_ada, b_ada).reshape(DEPTH, N_COND_ROWS, 6, D_MODEL)

    chan_c, chan_s = _channel_dft_blocks()
    w_cat = _fold_in_proj(w_in_a, jnp.asarray(chan_c), jnp.asarray(chan_s))
    head_sum = jnp.asarray(_head_sum_matrix()).astype(BF16)
    gains = jnp.concatenate([jnp.tile(q_norm_g[0], N_HEADS), jnp.tile(k_norm_g[0], N_KV_HEADS)])[None, :]
    w_out_bf = w_out_a[0].astype(BF16)

    w_up_bf, w_down_bf = w_up.astype(BF16), w_down.astype(BF16)
    w_in_c_bf, w_out_c_bf = w_in_c.astype(BF16), w_out_c.astype(BF16)

    def run_stream(x, mod_row, cache):
        s = x.shape[1]
        scale = (FOURIER_GROUP * s) ** -0.5
        dft_c, dft_s = (jnp.asarray(t).astype(BF16) for t in _dft_tables(s, scale))
        rope_tabs = _rope_tables(s) if cache is not None else None
        fcs, q, k, vt, *v = _in_proj(x, mod4, mod_row, w_cat, head_sum, gains, rope_tabs)
        x = _attn_out(x, fcs, q, k, vt, cache, dft_c, dft_s, w_out_bf, mod4, mod_row, ln_g, ln_b, 0)
        x = _gated_conv_block(_ffn_kernel, "conv_ffn", x, mod4, mod_row, w_up_bf, conv_f, w_down_bf,
                              ln_g, ln_b, 0, 0)
        x = _gated_conv_block(_mixer_kernel, "conv_mixer", x, mod4, mod_row, w_in_c_bf, conv_c, w_out_c_bf,
                              ln_g, ln_b, 1, 0)
        x = _gated_conv_block(_ffn_kernel, "conv_ffn", x, mod4, mod_row, w_up_bf, conv_f, w_down_bf,
                              ln_g, ln_b, 1, 1)
        return x, k, v

    y_prompt, k_new, (v_new,) = run_stream(x_prompt, lambda bi: 0, None)
    cache = (cache_k[:, 0].reshape(n_sample, past, KV_WIDTH),
             cache_v[:, 0].reshape(n_sample, past, KV_WIDTH).transpose(0, 2, 1))
    y_sample, _, _ = run_stream(x_sample, lambda bi: bi + 1, cache)

    new_shape = (n_prompt, 1, s_prompt, N_KV_HEADS, HEAD_DIM)
    return y_prompt, y_sample, k_new.reshape(new_shape), v_new.reshape(new_shape)
```

```python
import functools

import numpy as np
import jax
import jax.numpy as jnp
from jax import lax
from jax.experimental import pallas as pl
from jax.experimental.pallas import tpu as pltpu

D_MODEL = 1024
DEPTH = 2
GRID_W = 64
HEAD_DIM = 64
N_HEADS = 8
N_KV_HEADS = 2
Q_PER_KV = N_HEADS // N_KV_HEADS
ATTN_WIDTH = N_HEADS * HEAD_DIM
KV_WIDTH = N_KV_HEADS * HEAD_DIM
QK_WIDTH = ATTN_WIDTH + KV_WIDTH
FOURIER_GROUP = 64
FOURIER_WIDTH = 512
D_FF = 2816
ROPE_THETA = 10000.0
EPS = 1e-6
DEEPNORM_ALPHA = (2 * DEPTH) ** 0.25
LOG2_E = 1.4426950408889634

F32 = jnp.float32
BF16 = jnp.bfloat16

SUBLANES = 8
BF16_SUBLANES = 16
LANES = 128
MXU_WIDTH = 256

ROW_TILE = 256
ATTN_SUB_TILE = 128
KEY_CHUNK = MXU_WIDTH
HALO = SUBLANES
SEG_ROWS = HALO + ROW_TILE
SEGS_PER_BLOCK = 2
BLOCK_ROWS = SEGS_PER_BLOCK * SEG_ROWS
PERM_STRIDE = BLOCK_ROWS // SUBLANES
BLOCKS_PER_STEP = 2
N_SLABS = D_MODEL // LANES
TILES_PER_STEP = 2
CONV_TILES_PER_STEP = BLOCKS_PER_STEP * SEGS_PER_BLOCK
FF_CHUNK = MXU_WIDTH
N_FF_CHUNKS = D_FF // FF_CHUNK
N_MIX_CHUNKS = D_MODEL // FF_CHUNK
PROJ_WIDTH = 2 * FOURIER_WIDTH + QK_WIDTH + KV_WIDTH
N_COND_ROWS = 16
VMEM_LIMIT_BYTES = 48 * 1024 * 1024
CONV_VMEM_LIMIT_BYTES = 56 * 1024 * 1024


def _dot(a, b):
    return jnp.dot(a, b, preferred_element_type=F32)


def _split_bf16(a):
    hi = a.astype(BF16)
    lo = (a - hi.astype(F32)).astype(BF16)
    return hi, lo


def _sigmoid(x):
    return 1.0 / (1.0 + jnp.exp(-x))


def _layer_norm(y, g, b):
    mu = jnp.mean(y, axis=-1, keepdims=True)
    yc = y - mu
    var = jnp.mean(yc * yc, axis=-1, keepdims=True)
    return yc * lax.rsqrt(var + EPS) * g + b


def _params(n_axes):
    return pltpu.CompilerParams(dimension_semantics=("arbitrary",) * n_axes,
                                vmem_limit_bytes=VMEM_LIMIT_BYTES)


def _resident(block, index_map):
    return pl.BlockSpec(block, index_map, pipeline_mode=pl.Buffered(1))


def _mod_kernel(cond_ref, w_ref, b_ref, o_ref):
    c = cond_ref[...]
    s = (c * _sigmoid(c)).astype(BF16)
    o_ref[...] = _dot(s, w_ref[...].astype(BF16)) + b_ref[...]


def _modulation(cond, w_ada, b_ada):
    tn = 1536
    n_out = 6 * D_MODEL
    return pl.pallas_call(
        _mod_kernel,
        grid=(DEPTH, n_out // tn),
        in_specs=[
            pl.BlockSpec((N_COND_ROWS, D_MODEL), lambda l, n: (0, 0)),
            pl.BlockSpec((None, D_MODEL, tn), lambda l, n: (l, 0, n)),
            pl.BlockSpec((None, 1, tn), lambda l, n: (l, 0, n)),
        ],
        out_specs=pl.BlockSpec((None, N_COND_ROWS, tn), lambda l, n: (l, 0, n)),
        out_shape=jax.ShapeDtypeStruct((DEPTH, N_COND_ROWS, n_out), F32),
        compiler_params=_params(2),
        name="modulation",
    )(cond, w_ada, b_ada.reshape(DEPTH, 1, n_out))


def _fold_kernel(w_ref, c_ref, s_ref, o_ref):
    j = pl.program_id(0)
    n_fourier_blocks = FOURIER_WIDTH // MXU_WIDTH

    def dft(m_ref):
        w_hi, w_lo = _split_bf16(w_ref[...])
        m_hi, m_lo = _split_bf16(m_ref[...])
        return _dot(w_hi, m_hi) + _dot(w_hi, m_lo) + _dot(w_lo, m_hi)

    @pl.when(j < n_fourier_blocks)
    def _():
        o_ref[...] = dft(c_ref).astype(BF16)

    @pl.when(jnp.logical_and(j >= n_fourier_blocks, j < 2 * n_fourier_blocks))
    def _():
        o_ref[...] = dft(s_ref).astype(BF16)

    @pl.when(j >= 2 * n_fourier_blocks)
    def _():
        o_ref[...] = w_ref[...].astype(BF16)


def _fold_in_proj(w_in_a, dft_c, dft_s):
    nb = FOURIER_WIDTH // MXU_WIDTH
    return pl.pallas_call(
        _fold_kernel,
        grid=(PROJ_WIDTH // MXU_WIDTH,),
        in_specs=[
            pl.BlockSpec((None, D_MODEL, MXU_WIDTH), lambda j: (0, 0, jnp.where(j < nb, j, j - nb))),
            pl.BlockSpec((MXU_WIDTH, MXU_WIDTH), lambda j: (0, 0)),
            pl.BlockSpec((MXU_WIDTH, MXU_WIDTH), lambda j: (0, 0)),
        ],
        out_specs=pl.BlockSpec((D_MODEL, MXU_WIDTH), lambda j: (0, j)),
        out_shape=jax.ShapeDtypeStruct((D_MODEL, PROJ_WIDTH), BF16),
        compiler_params=_params(1),
        name="fold_in_proj",
    )(w_in_a, dft_c, dft_s)


def _step_blocking(b, s):
    rows = min(s, TILES_PER_STEP * ROW_TILE)
    batches = TILES_PER_STEP * ROW_TILE // rows
    assert s % rows == 0 and b % batches == 0 and rows % ROW_TILE == 0
    return batches, rows, [(bu, ru) for bu in range(batches) for ru in range(0, rows, ROW_TILE)]


def _in_proj_kernel(*refs, rope, tiles):
    if rope:
        (x_ref, mod_ref, w_ref, hs_ref, g_ref, cos_ref, sa_ref, sb_ref,
         fcs_ref, q_ref, k_ref, vt_ref) = refs
    else:
        x_ref, mod_ref, w_ref, hs_ref, g_ref, fcs_ref, q_ref, k_ref, vt_ref, v_ref = refs
    shift = mod_ref[0:1, :]
    scale = mod_ref[1:2, :]
    for bu, ru in tiles:
        rows = slice(ru, ru + ROW_TILE)
        h = (x_ref[bu, rows, :] * (1.0 + scale) + shift).astype(BF16)
        qk_cols = slice(2 * FOURIER_WIDTH, 2 * FOURIER_WIDTH + QK_WIDTH)
        qk = _dot(h, w_ref[:, qk_cols])
        ssq = _dot((qk * qk).astype(BF16), hs_ref[...])
        fcs_ref[bu, rows, :] = _dot(h, w_ref[:, :2 * FOURIER_WIDTH]).astype(BF16)
        v = _dot(h, w_ref[:, 2 * FOURIER_WIDTH + QK_WIDTH:])
        y = qk * lax.rsqrt(ssq * (1.0 / HEAD_DIM) + EPS) * g_ref[...]
        if rope:
            pos = pl.ds(pl.multiple_of(pl.program_id(1) * x_ref.shape[1] + ru, ROW_TILE), ROW_TILE)
            parts = []
            for t in range(QK_WIDTH // LANES):
                yt = y[:, t * LANES:(t + 1) * LANES]
                lo = slice(t * LANES, (t + 1) * LANES)
                parts.append(yt * cos_ref[pos, lo]
                             + pltpu.roll(yt, LANES - 16, 1) * sa_ref[pos, lo]
                             + pltpu.roll(yt, 16, 1) * sb_ref[pos, lo])
            y = jnp.concatenate(parts, axis=1)
        q_ref[bu, rows, :] = (y[:, :ATTN_WIDTH] * (HEAD_DIM ** -0.5 * LOG2_E)).astype(q_ref.dtype)
        k_ref[bu, rows, :] = y[:, ATTN_WIDTH:].astype(k_ref.dtype)
        vt_ref[bu, :, rows] = v.T.astype(vt_ref.dtype)
        if not rope:
            v_ref[bu, rows, :] = v


def _in_proj(x, mod4, mod_row, w_cat, head_sum, gains, rope_tabs):
    b, s, _ = x.shape
    rope = rope_tabs is not None
    bb, tr, tiles = _step_blocking(b, s)
    row_spec = lambda w: pl.BlockSpec((bb, tr, w), lambda bi, i: (bi, i, 0))
    out_specs = [row_spec(2 * FOURIER_WIDTH), row_spec(ATTN_WIDTH), row_spec(KV_WIDTH),
                 pl.BlockSpec((bb, KV_WIDTH, tr), lambda bi, i: (bi, 0, i))]
    out_shape = [
        jax.ShapeDtypeStruct((b, s, 2 * FOURIER_WIDTH), BF16),
        jax.ShapeDtypeStruct((b, s, ATTN_WIDTH), BF16),
        jax.ShapeDtypeStruct((b, s, KV_WIDTH), BF16 if rope else F32),
        jax.ShapeDtypeStruct((b, KV_WIDTH, s), BF16),
    ]
    if not rope:
        out_specs.append(row_spec(KV_WIDTH))
        out_shape.append(jax.ShapeDtypeStruct((b, s, KV_WIDTH), F32))
    in_specs = [
        row_spec(D_MODEL),
        pl.BlockSpec((None, None, 6, D_MODEL), lambda bi, i: (0, mod_row(bi * bb), 0, 0)),
        _resident((D_MODEL, PROJ_WIDTH), lambda bi, i: (0, 0)),
        _resident((QK_WIDTH, QK_WIDTH), lambda bi, i: (0, 0)),
        _resident((1, QK_WIDTH), lambda bi, i: (0, 0)),
    ]
    args = [x, mod4, w_cat, head_sum, gains]
    if rope:
        in_specs += [_resident((s, QK_WIDTH), lambda bi, i: (0, 0))] * 3
        args += list(rope_tabs)
    return pl.pallas_call(
        functools.partial(_in_proj_kernel, rope=rope, tiles=tiles),
        grid=(b // bb, s // tr),
        in_specs=in_specs,
        out_specs=out_specs,
        out_shape=out_shape,
        compiler_params=_params(2),
        name="in_proj_rope" if rope else "in_proj",
    )(*args)


def _attention_scores(q_ref, bu, row0, key_sets, sub_tile):
    nt = (((1,), (1,)), ((), ()))
    q = q_ref[bu, row0:row0 + sub_tile, :]
    items = []
    for j in range(N_KV_HEADS):
        kv = slice(j * HEAD_DIM, (j + 1) * HEAD_DIM)
        qs = jnp.concatenate(
            [q[:, h * HEAD_DIM:(h + 1) * HEAD_DIM] for h in range(j * Q_PER_KV, (j + 1) * Q_PER_KV)], axis=0)
        items.append([lax.dot_general(keys_ref[bu, :, kv].astype(BF16), qs, nt, preferred_element_type=F32)
                      for keys_ref, _ in key_sets])
    return items


def _attention_finish(items, bu, key_sets, attn_ref, arow0, sub_tile):
    ones_rows = jnp.ones((BF16_SUBLANES, KEY_CHUNK), BF16)
    for j, sts in enumerate(items):
        parts = []
        for st_all, (_, vts_ref) in zip(sts, key_sets):
            for c in range(0, st_all.shape[0], KEY_CHUNK):
                st = st_all[c:c + KEY_CHUNK]
                m = jnp.max(st, axis=0, keepdims=True)
                vt_ext = jnp.concatenate(
                    [vts_ref[bu, j * HEAD_DIM:(j + 1) * HEAD_DIM, c:c + KEY_CHUNK].astype(BF16), ones_rows], axis=0)
                parts.append((m, _dot(vt_ext, jnp.exp2(st - m).astype(BF16))))
        m_all = functools.reduce(jnp.maximum, [m for m, _ in parts])
        ot = (functools.reduce(jnp.add, [jnp.exp2(m - m_all) * o for m, o in parts])
              if len(parts) > 1 else parts[0][1])
        o = ot[0:HEAD_DIM] / ot[HEAD_DIM:HEAD_DIM + 1]
        for g in range(0, Q_PER_KV, 2):
            pair = jnp.concatenate([o[:, g * sub_tile:(g + 1) * sub_tile],
                                    o[:, (g + 1) * sub_tile:(g + 2) * sub_tile]], axis=0).T
            lane0 = (j * Q_PER_KV + g) * HEAD_DIM
            attn_ref[arow0:arow0 + sub_tile, lane0:lane0 + LANES] = pair.astype(attn_ref.dtype)


def _attn_out_kernel(*refs, has_cache, tiles, sub_tile):
    if has_cache:
        (q_ref, k_ref, vt_ref, ck_ref, cvt_ref, x_ref, fcs_ref, dc_ref, ds_ref, w_ref, mod_ref, lng_ref, lnb_ref,
         o_ref, attn_ref) = refs
        key_sets = [(k_ref, vt_ref), (ck_ref, cvt_ref)]
    else:
        (q_ref, k_ref, vt_ref, x_ref, fcs_ref, dc_ref, ds_ref, w_ref, mod_ref, lng_ref, lnb_ref,
         o_ref, attn_ref) = refs
        key_sets = [(k_ref, vt_ref)]
    gate = mod_ref[2:3, :]
    subs_per_tile = ROW_TILE // sub_tile
    subs = [(n, bu, ru, t) for n, (bu, ru) in enumerate(tiles) for t in range(subs_per_tile)]
    scores = lambda n, bu, ru, t: _attention_scores(q_ref, bu, ru + t * sub_tile, key_sets, sub_tile)

    four = None
    items_next = scores(*subs[0])
    for i, (n, bu, ru, t) in enumerate(subs):
        items = items_next
        if i + 1 < len(subs):
            items_next = scores(*subs[i + 1])
        if t == 0:
            pos = pl.ds(pl.multiple_of(pl.program_id(1) * x_ref.shape[1] + ru, ROW_TILE), ROW_TILE)
            four = (_dot(dc_ref[pos, :], fcs_ref[bu, :, :FOURIER_WIDTH])
                    - _dot(ds_ref[pos, :], fcs_ref[bu, :, FOURIER_WIDTH:]))
        _attention_finish(items, bu, key_sets, attn_ref, n * ROW_TILE + t * sub_tile, sub_tile)
        if t == subs_per_tile - 1:
            rows = slice(ru, ru + ROW_TILE)
            out = (_dot(four.astype(BF16), w_ref[:FOURIER_WIDTH, :])
                   + _dot(attn_ref[n * ROW_TILE:(n + 1) * ROW_TILE, :], w_ref[FOURIER_WIDTH:, :]))
            y = DEEPNORM_ALPHA * x_ref[bu, rows, :] + gate * out
            o_ref[bu, rows, :] = _layer_norm(y, lng_ref[0:1, :], lnb_ref[0:1, :])


def _attn_out(x, fcs, q, k, vt, cache, dft_c, dft_s, w_out, mod4, mod_row, ln_g, ln_b, layer):
    b, s, _ = x.shape
    has_cache = cache is not None
    bb, tr, tiles = _step_blocking(b, s)
    row_spec = lambda w: pl.BlockSpec((bb, tr, w), lambda bi, i: (bi, i, 0))
    seq_spec = lambda n, w: pl.BlockSpec((bb, n, w), lambda bi, i: (bi, 0, 0))
    seq_t_spec = lambda n: pl.BlockSpec((bb, KV_WIDTH, n), lambda bi, i: (bi, 0, 0))
    ln_spec = pl.BlockSpec((None, 2, D_MODEL), lambda bi, i: (layer, 0, 0))
    in_specs = [row_spec(ATTN_WIDTH), seq_spec(s, KV_WIDTH), seq_t_spec(s)]
    args = [q, k, vt]
    if has_cache:
        past = cache[0].shape[1]
        in_specs += [seq_spec(past, KV_WIDTH), seq_t_spec(past)]
        args += list(cache)
    in_specs += [
        row_spec(D_MODEL),
        seq_spec(s, 2 * FOURIER_WIDTH),
        _resident((s, s), lambda bi, i: (0, 0)),
        _resident((s, s), lambda bi, i: (0, 0)),
        _resident((2 * FOURIER_WIDTH, D_MODEL), lambda bi, i: (0, 0)),
        pl.BlockSpec((None, None, 6, D_MODEL), lambda bi, i: (layer, mod_row(bi * bb), 0, 0)),
        ln_spec, ln_spec,
    ]
    args += [x, fcs, dft_c, dft_s, w_out, mod4, ln_g, ln_b]
    return pl.pallas_call(
        functools.partial(_attn_out_kernel, has_cache=has_cache, tiles=tiles, sub_tile=ATTN_SUB_TILE),
        grid=(b // bb, s // tr),
        in_specs=in_specs,
        out_specs=row_spec(D_MODEL),
        out_shape=jax.ShapeDtypeStruct((b, s, D_MODEL), F32),
        scratch_shapes=[pltpu.VMEM((bb * tr, ATTN_WIDTH), BF16)],
        compiler_params=_params(2),
        name="attn_out_cached" if has_cache else "attn_out",
    )(*args)


def _load_permuted(slab_ref):
    slabs = [jnp.concatenate([slab_ref[c, pl.ds(v, SUBLANES, stride=PERM_STRIDE), :] for v in range(PERM_STRIDE)],
                             axis=0) for c in range(N_SLABS)]
    return jnp.concatenate(slabs, axis=1)


def _store_unpermuted(y, slab_ref):
    for c in range(N_SLABS):
        for v in range(PERM_STRIDE):
            slab_ref[c, pl.ds(v, SUBLANES, stride=PERM_STRIDE), :] = (
                y[v * SUBLANES:(v + 1) * SUBLANES, c * LANES:(c + 1) * LANES])


def _perm_row(q):
    return SUBLANES * (q % PERM_STRIDE) + q // PERM_STRIDE


def _stage_block(blk, tiles_per_seq, x_ref, xp_ref, xn_ref, shift, scale, xe_ref, h_ref):
    sublane = lax.broadcasted_iota(jnp.int32, (SUBLANES, 1), 0)
    rows = lax.broadcasted_iota(jnp.int32, (BLOCK_ROWS, 1), 0)
    keep = jnp.ones((BLOCK_ROWS, 1), F32)
    tiles = [blk * SEGS_PER_BLOCK + seg for seg in range(SEGS_PER_BLOCK)]

    def in_sequence(first_tile, second_tile):
        if tiles_per_seq == 1:
            return 0.0
        tile = pl.program_id(0) * CONV_TILES_PER_STEP + second_tile
        return jnp.where(lax.rem(tile, tiles_per_seq) != 0, 1.0, 0.0)

    for seg, u in enumerate(tiles):
        lo = u * ROW_TILE
        base = seg * SEG_ROWS
        w = tiles[seg - 1]
        after_lo = (w + 1) * ROW_TILE
        for c in range(N_SLABS):
            cols = slice(c * LANES, (c + 1) * LANES)
            before = xp_ref[:, cols] if u == 0 else x_ref[lo - HALO:lo, cols]
            after = xn_ref[0:1, cols] if w == CONV_TILES_PER_STEP - 1 else x_ref[after_lo:after_lo + 1, cols]
            xe_ref[blk, c, base:base + HALO, :] = jnp.where(sublane == 0, after, before)
            xe_ref[blk, c, base + HALO:base + SEG_ROWS, :] = x_ref[lo:lo + ROW_TILE, cols]
        for slot in range(HALO):
            value = in_sequence(u - 1, u) if slot == HALO - 1 else in_sequence(w, w + 1) if slot == 0 else 0.0
            keep = jnp.where(rows == _perm_row(base + slot), value, keep)
    h = (_load_permuted(xe_ref.at[blk]) * (1.0 + scale) + shift) * keep
    h_ref[blk] = h.astype(BF16)


def _conv3_permuted(t, cw):
    first_prev = pltpu.roll(t[BLOCK_ROWS - SUBLANES:BLOCK_ROWS], 1, 0)
    prev = jnp.concatenate([first_prev, t[0:BLOCK_ROWS - SUBLANES]], axis=0)
    last_next = pltpu.roll(t[0:SUBLANES], SUBLANES - 1, 0)
    nxt = jnp.concatenate([t[SUBLANES:BLOCK_ROWS], last_next], axis=0)
    return cw[0:1, :] * prev + cw[1:2, :] * t + cw[2:3, :] * nxt


def _finish_block(blk, xe_ref, acc, gate, lng, lnb, o_ref):
    y = DEEPNORM_ALPHA * _load_permuted(xe_ref.at[blk]) + gate * acc
    _store_unpermuted(_layer_norm(y, lng, lnb), xe_ref.at[blk])
    for seg in range(SEGS_PER_BLOCK):
        lo = (blk * SEGS_PER_BLOCK + seg) * ROW_TILE
        base = seg * SEG_ROWS
        for c in range(N_SLABS):
            o_ref[lo:lo + ROW_TILE, c * LANES:(c + 1) * LANES] = xe_ref[blk, c, base + HALO:base + SEG_ROWS, :]


def _ffn_kernel(x_ref, xp_ref, xn_ref, mod_ref, wup_ref, cw_ref, wdn_ref, lng_ref, lnb_ref, o_ref,
                xe_ref, h_ref, act_ref, *, tiles_per_seq):
    for blk in range(BLOCKS_PER_STEP):
        _stage_block(blk, tiles_per_seq, x_ref, xp_ref, xn_ref, mod_ref[3:4, :], mod_ref[4:5, :], xe_ref, h_ref)
        for j in range(N_FF_CHUNKS):
            a_cols = slice(j * FF_CHUNK, (j + 1) * FF_CHUNK)
            g_cols = slice(D_FF + j * FF_CHUNK, D_FF + (j + 1) * FF_CHUNK)
            a = _conv3_permuted(_dot(h_ref[blk], wup_ref[:, a_cols]), cw_ref[:, a_cols])
            g = _conv3_permuted(_dot(h_ref[blk], wup_ref[:, g_cols]), cw_ref[:, g_cols])
            act_ref[blk, :, a_cols] = (g * _sigmoid(g) * a).astype(BF16)
    for blk in range(BLOCKS_PER_STEP):
        acc = _dot(act_ref[blk], wdn_ref[...])
        _finish_block(blk, xe_ref, acc, mod_ref[5:6, :], lng_ref[1:2, :], lnb_ref[1:2, :], o_ref)


def _mixer_kernel(x_ref, xp_ref, xn_ref, mod_ref, win_ref, cw_ref, wout_ref, lng_ref, lnb_ref, o_ref,
                  xe_ref, h_ref, act_ref, *, tiles_per_seq):
    for blk in range(BLOCKS_PER_STEP):
        _stage_block(blk, tiles_per_seq, x_ref, xp_ref, xn_ref, mod_ref[0:1, :], mod_ref[1:2, :], xe_ref, h_ref)
        for j in range(N_MIX_CHUNKS):
            cols = slice(j * FF_CHUNK, (j + 1) * FF_CHUNK)
            bg, cg, xin = (_dot(h_ref[blk], win_ref[:, k * D_MODEL + j * FF_CHUNK:k * D_MODEL + (j + 1) * FF_CHUNK])
                           for k in range(3))
            y = bg * _conv3_permuted(cg * xin, cw_ref[:, cols])
            act_ref[blk, :, cols] = y.astype(BF16)
    for blk in range(BLOCKS_PER_STEP):
        acc = _dot(act_ref[blk], wout_ref[...])
        _finish_block(blk, xe_ref, acc, mod_ref[2:3, :], lng_ref[0:1, :], lnb_ref[0:1, :], o_ref)


def _gated_conv_block(kernel_fn, name, x, mod4, mod_row, w1, cw, w2, ln_g, ln_b, layer, w_layer):
    b, s, _ = x.shape
    hidden = w2.shape[1]
    n_rows = b * s
    step_rows = CONV_TILES_PER_STEP * ROW_TILE
    assert s % ROW_TILE == 0 and (s % step_rows == 0 or step_rows % s == 0)
    halo_blocks_per_step = step_rows // HALO
    last_halo_block = n_rows // HALO - 1
    ln_spec = pl.BlockSpec((None, 2, D_MODEL), lambda i: (layer, 0, 0))
    out = pl.pallas_call(
        functools.partial(kernel_fn, tiles_per_seq=s // ROW_TILE),
        grid=(n_rows // step_rows,),
        in_specs=[
            pl.BlockSpec((step_rows, D_MODEL), lambda i: (i, 0)),
            pl.BlockSpec((HALO, D_MODEL), lambda i: (jnp.maximum(i * halo_blocks_per_step - 1, 0), 0)),
            pl.BlockSpec((HALO, D_MODEL),
                         lambda i: (jnp.minimum((i + 1) * halo_blocks_per_step, last_halo_block), 0)),
            pl.BlockSpec((None, None, 6, D_MODEL), lambda i: (layer, mod_row(i * step_rows // s), 0, 0)),
            _resident((None,) + w1.shape[1:], lambda i: (w_layer, 0, 0)),
            _resident((None,) + cw.shape[1:], lambda i: (w_layer, 0, 0)),
            _resident((None,) + w2.shape[1:], lambda i: (w_layer, 0, 0)),
            ln_spec, ln_spec,
        ],
        out_specs=pl.BlockSpec((step_rows, D_MODEL), lambda i: (i, 0)),
        out_shape=jax.ShapeDtypeStruct((n_rows, D_MODEL), F32),
        scratch_shapes=[
            pltpu.VMEM((BLOCKS_PER_STEP, N_SLABS, BLOCK_ROWS, LANES), F32),
            pltpu.VMEM((BLOCKS_PER_STEP, BLOCK_ROWS, D_MODEL), BF16),
            pltpu.VMEM((BLOCKS_PER_STEP, BLOCK_ROWS, hidden), BF16),
        ],
        compiler_params=pltpu.CompilerParams(dimension_semantics=("arbitrary",),
                                             vmem_limit_bytes=CONV_VMEM_LIMIT_BYTES),
        name=name,
    )(*([x.reshape(n_rows, D_MODEL)] * 3), mod4, w1, cw, w2, ln_g, ln_b)
    return out.reshape(b, s, D_MODEL)


def _dft_tables(n, scale):
    jk = np.outer(np.arange(n), np.arange(n)) % n
    ang = 2.0 * np.pi * jk / n
    return (np.cos(ang) * scale).astype(np.float32), (np.sin(ang) * scale).astype(np.float32)


def _channel_dft_blocks():
    c, s = _dft_tables(FOURIER_GROUP, 1.0)
    eye = np.eye(MXU_WIDTH // FOURIER_GROUP, dtype=np.float32)
    return np.kron(eye, c), np.kron(eye, s)


def _rope_tables(n):
    half = HEAD_DIM // 2
    inv = 1.0 / (ROPE_THETA ** (np.arange(0, half, 2, dtype=np.float64) / half))
    pos = np.arange(n)
    d = np.arange(QK_WIDTH) % HEAD_DIM
    coord = np.where(d < half, (pos // GRID_W)[:, None], (pos % GRID_W)[:, None])
    ang = coord * inv[d % (half // 2)][None, :]
    first = (d % half) < (half // 2)
    cos = np.cos(ang)
    sa = np.where(first[None, :], -np.sin(ang), 0.0)
    sb = np.where(first[None, :], 0.0, np.sin(ang))
    return tuple(jnp.asarray(t.astype(np.float32)) for t in (cos, sa, sb))


def _head_sum_matrix():
    return np.kron(np.eye(QK_WIDTH // HEAD_DIM, dtype=np.float32), np.ones((HEAD_DIM, HEAD_DIM), np.float32))


def kernel(x_prompt, x_sample, cache_k, cache_v, c, c_ctx, w_ada, b_ada, ln_g, ln_b, w_in_a, q_norm_g,
           k_norm_g, w_out_a, w_in_c, conv_c, w_out_c, w_up, conv_f, w_down):
    n_prompt, s_prompt, _ = x_prompt.shape
    n_sample, s_sample, _ = x_sample.shape
    past = cache_k.shape[2]

    cond = jnp.concatenate(
        [c_ctx[None, :], c, jnp.zeros((N_COND_ROWS - 1 - n_sample, D_MODEL), F32)], axis=0)
    mod4 = _modulation(cond, w_ada, b_ada).reshape(DEPTH, N_COND_ROWS, 6, D_MODEL)

    chan_c, chan_s = _channel_dft_blocks()
    w_cat = _fold_in_proj(w_in_a, jnp.asarray(chan_c), jnp.asarray(chan_s))
    head_sum = jnp.asarray(_head_sum_matrix()).astype(BF16)
    gains = jnp.concatenate([jnp.tile(q_norm_g[0], N_HEADS), jnp.tile(k_norm_g[0], N_KV_HEADS)])[None, :]
    w_out_bf = w_out_a[0].astype(BF16)

    w_up_bf, w_down_bf = w_up.astype(BF16), w_down.astype(BF16)
    w_in_c_bf, w_out_c_bf = w_in_c.astype(BF16), w_out_c.astype(BF16)

    def run_stream(x, mod_row, cache):
        s = x.shape[1]
        scale = (FOURIER_GROUP * s) ** -0.5
        dft_c, dft_s = (jnp.asarray(t).astype(BF16) for t in _dft_tables(s, scale))
        rope_tabs = _rope_tables(s) if cache is not None else None
        fcs, q, k, vt, *v = _in_proj(x, mod4, mod_row, w_cat, head_sum, gains, rope_tabs)
        x = _attn_out(x, fcs, q, k, vt, cache, dft_c, dft_s, w_out_bf, mod4, mod_row, ln_g, ln_b, 0)
        x = _gated_conv_block(_ffn_kernel, "conv_ffn", x, mod4, mod_row, w_up_bf, conv_f, w_down_bf,
                              ln_g, ln_b, 0, 0)
        x = _gated_conv_block(_mixer_kernel, "conv_mixer", x, mod4, mod_row, w_in_c_bf, conv_c, w_out_c_bf,
                              ln_g, ln_b, 1, 0)
        x = _gated_conv_block(_ffn_kernel, "conv_ffn", x, mod4, mod_row, w_up_bf, conv_f, w_down_bf,
                              ln_g, ln_b, 1, 1)
        return x, k, v

    y_prompt, k_new, (v_new,) = run_stream(x_prompt, lambda bi: 0, None)
    cache = (cache_k[:, 0].reshape(n_sample, past, KV_WIDTH),
             cache_v[:, 0].reshape(n_sample, past, KV_WIDTH).transpose(0, 2, 1))
    y_sample, _, _ = run_stream(x_sample, lambda bi: bi + 1, cache)

    new_shape = (n_prompt, 1, s_prompt, N_KV_HEADS, HEAD_DIM)
    return y_prompt, y_sample, k_new.reshape(new_shape), v_new.reshape(new_shape)
```

```python
import functools

import numpy as np
import jax
import jax.numpy as jnp
from jax import lax
from jax.experimental import pallas as pl
from jax.experimental.pallas import tpu as pltpu

D_MODEL = 1024
DEPTH = 2
GRID_W = 64
HEAD_DIM = 64
N_HEADS = 8
N_KV_HEADS = 2
Q_PER_KV = N_HEADS // N_KV_HEADS
ATTN_WIDTH = N_HEADS * HEAD_DIM
KV_WIDTH = N_KV_HEADS * HEAD_DIM
QK_WIDTH = ATTN_WIDTH + KV_WIDTH
FOURIER_GROUP = 64
FOURIER_WIDTH = 512
D_FF = 2816
ROPE_THETA = 10000.0
EPS = 1e-6
DEEPNORM_ALPHA = (2 * DEPTH) ** 0.25
LOG2_E = 1.4426950408889634

F32 = jnp.float32
BF16 = jnp.bfloat16

SUBLANES = 8
BF16_SUBLANES = 16
LANES = 128
MXU_WIDTH = 256

ROW_TILE = 256
ATTN_SUB_TILE = 128
KEY_CHUNK = MXU_WIDTH
HALO = SUBLANES
SEG_ROWS = HALO + ROW_TILE
SEGS_PER_BLOCK = 2
BLOCK_ROWS = SEGS_PER_BLOCK * SEG_ROWS
PERM_STRIDE = BLOCK_ROWS // SUBLANES
BLOCKS_PER_STEP = 1
N_SLABS = D_MODEL // LANES
TILES_PER_STEP = 2
CONV_TILES_PER_STEP = BLOCKS_PER_STEP * SEGS_PER_BLOCK
FF_CHUNK = MXU_WIDTH
N_FF_CHUNKS = D_FF // FF_CHUNK
N_MIX_CHUNKS = D_MODEL // FF_CHUNK
PROJ_WIDTH = 2 * FOURIER_WIDTH + QK_WIDTH + KV_WIDTH
N_COND_ROWS = 16
VMEM_LIMIT_BYTES = 48 * 1024 * 1024
CONV_VMEM_LIMIT_BYTES = 56 * 1024 * 1024


def _dot(a, b):
    return lax.dot_general(a, b, (((a.ndim - 1,), (0,)), ((), ())), preferred_element_type=F32)


def _split_bf16(a):
    hi = a.astype(BF16)
    lo = (a - hi.astype(F32)).astype(BF16)
    return hi, lo


def _sigmoid(x):
    return 1.0 / (1.0 + jnp.exp(-x))


def _layer_norm(y, g, b):
    mu = jnp.mean(y, axis=-1, keepdims=True)
    yc = y - mu
    var = jnp.mean(yc * yc, axis=-1, keepdims=True)
    return yc * lax.rsqrt(var + EPS) * g + b


def _params(n_axes):
    return pltpu.CompilerParams(dimension_semantics=("arbitrary",) * n_axes,
                                vmem_limit_bytes=VMEM_LIMIT_BYTES)


def _resident(block, index_map):
    return pl.BlockSpec(block, index_map, pipeline_mode=pl.Buffered(1))


def _mod_kernel(cond_ref, w_ref, b_ref, o_ref):
    c = cond_ref[...]
    s = (c * _sigmoid(c)).astype(BF16)
    o_ref[...] = _dot(s, w_ref[...].astype(BF16)) + b_ref[...]


def _modulation(cond, w_ada, b_ada):
    tn = 1536
    n_out = 6 * D_MODEL
    return pl.pallas_call(
        _mod_kernel,
        grid=(DEPTH, n_out // tn),
        in_specs=[
            pl.BlockSpec((N_COND_ROWS, D_MODEL), lambda l, n: (0, 0)),
            pl.BlockSpec((None, D_MODEL, tn), lambda l, n: (l, 0, n)),
            pl.BlockSpec((None, 1, tn), lambda l, n: (l, 0, n)),
        ],
        out_specs=pl.BlockSpec((None, N_COND_ROWS, tn), lambda l, n: (l, 0, n)),
        out_shape=jax.ShapeDtypeStruct((DEPTH, N_COND_ROWS, n_out), F32),
        compiler_params=_params(2),
        name="modulation",
    )(cond, w_ada, b_ada.reshape(DEPTH, 1, n_out))


def _fold_kernel(w_ref, c_ref, s_ref, o_ref):
    j = pl.program_id(0)
    n_fourier_blocks = FOURIER_WIDTH // MXU_WIDTH

    def dft(m_ref):
        w_hi, w_lo = _split_bf16(w_ref[...])
        m_hi, m_lo = _split_bf16(m_ref[...])
        return _dot(w_hi, m_hi) + _dot(w_hi, m_lo) + _dot(w_lo, m_hi)

    @pl.when(j < n_fourier_blocks)
    def _():
        o_ref[...] = dft(c_ref).astype(BF16)

    @pl.when(jnp.logical_and(j >= n_fourier_blocks, j < 2 * n_fourier_blocks))
    def _():
        o_ref[...] = dft(s_ref).astype(BF16)

    @pl.when(j >= 2 * n_fourier_blocks)
    def _():
        o_ref[...] = w_ref[...].astype(BF16)


def _fold_in_proj(w_in_a, dft_c, dft_s):
    nb = FOURIER_WIDTH // MXU_WIDTH
    return pl.pallas_call(
        _fold_kernel,
        grid=(PROJ_WIDTH // MXU_WIDTH,),
        in_specs=[
            pl.BlockSpec((None, D_MODEL, MXU_WIDTH), lambda j: (0, 0, jnp.where(j < nb, j, j - nb))),
            pl.BlockSpec((MXU_WIDTH, MXU_WIDTH), lambda j: (0, 0)),
            pl.BlockSpec((MXU_WIDTH, MXU_WIDTH), lambda j: (0, 0)),
        ],
        out_specs=pl.BlockSpec((D_MODEL, MXU_WIDTH), lambda j: (0, j)),
        out_shape=jax.ShapeDtypeStruct((D_MODEL, PROJ_WIDTH), BF16),
        compiler_params=_params(1),
        name="fold_in_proj",
    )(w_in_a, dft_c, dft_s)


def _step_blocking(b, s):
    rows = min(s, TILES_PER_STEP * ROW_TILE)
    batches = TILES_PER_STEP * ROW_TILE // rows
    assert s % rows == 0 and b % batches == 0 and rows % ROW_TILE == 0
    return batches, rows, [(bu, ru) for bu in range(batches) for ru in range(0, rows, ROW_TILE)]


def _in_proj_kernel(*refs, rope, tiles):
    if rope:
        (x_ref, mod_ref, w_ref, hs_ref, g_ref, cos_ref, sa_ref, sb_ref,
         fcs_ref, q_ref, k_ref, vt_ref) = refs
    else:
        x_ref, mod_ref, w_ref, hs_ref, g_ref, fcs_ref, q_ref, k_ref, vt_ref, v_ref = refs
    shift = mod_ref[0:1, :]
    scale = mod_ref[1:2, :]
    for bu, ru in tiles:
        rows = slice(ru, ru + ROW_TILE)
        h = (x_ref[bu, rows, :] * (1.0 + scale) + shift).astype(BF16)
        qk_cols = slice(2 * FOURIER_WIDTH, 2 * FOURIER_WIDTH + QK_WIDTH)
        qk = _dot(h, w_ref[:, qk_cols])
        ssq = _dot((qk * qk).astype(BF16), hs_ref[...])
        fcs_ref[bu, rows, :] = _dot(h, w_ref[:, :2 * FOURIER_WIDTH]).astype(BF16)
        v = _dot(h, w_ref[:, 2 * FOURIER_WIDTH + QK_WIDTH:])
        y = qk * lax.rsqrt(ssq * (1.0 / HEAD_DIM) + EPS) * g_ref[...]
        if rope:
            pos = pl.ds(pl.multiple_of(pl.program_id(1) * x_ref.shape[1] + ru, ROW_TILE), ROW_TILE)
            parts = []
            for t in range(QK_WIDTH // LANES):
                yt = y[:, t * LANES:(t + 1) * LANES]
                lo = slice(t * LANES, (t + 1) * LANES)
                parts.append(yt * cos_ref[pos, lo]
                             + pltpu.roll(yt, LANES - 16, 1) * sa_ref[pos, lo]
                             + pltpu.roll(yt, 16, 1) * sb_ref[pos, lo])
            y = jnp.concatenate(parts, axis=1)
        q_ref[bu, rows, :] = (y[:, :ATTN_WIDTH] * (HEAD_DIM ** -0.5 * LOG2_E)).astype(q_ref.dtype)
        k_ref[bu, rows, :] = y[:, ATTN_WIDTH:].astype(k_ref.dtype)
        vt_ref[bu, :, rows] = v.T.astype(vt_ref.dtype)
        if not rope:
            v_ref[bu, rows, :] = v


def _in_proj(x, mod4, mod_row, w_cat, head_sum, gains, rope_tabs):
    b, s, _ = x.shape
    rope = rope_tabs is not None
    bb, tr, tiles = _step_blocking(b, s)
    row_spec = lambda w: pl.BlockSpec((bb, tr, w), lambda bi, i: (bi, i, 0))
    out_specs = [row_spec(2 * FOURIER_WIDTH), row_spec(ATTN_WIDTH), row_spec(KV_WIDTH),
                 pl.BlockSpec((bb, KV_WIDTH, tr), lambda bi, i: (bi, 0, i))]
    out_shape = [
        jax.ShapeDtypeStruct((b, s, 2 * FOURIER_WIDTH), BF16),
        jax.ShapeDtypeStruct((b, s, ATTN_WIDTH), BF16),
        jax.ShapeDtypeStruct((b, s, KV_WIDTH), BF16 if rope else F32),
        jax.ShapeDtypeStruct((b, KV_WIDTH, s), BF16),
    ]
    if not rope:
        out_specs.append(row_spec(KV_WIDTH))
        out_shape.append(jax.ShapeDtypeStruct((b, s, KV_WIDTH), F32))
    in_specs = [
        row_spec(D_MODEL),
        pl.BlockSpec((None, None, 6, D_MODEL), lambda bi, i: (0, mod_row(bi * bb), 0, 0)),
        _resident((D_MODEL, PROJ_WIDTH), lambda bi, i: (0, 0)),
        _resident((QK_WIDTH, QK_WIDTH), lambda bi, i: (0, 0)),
        _resident((1, QK_WIDTH), lambda bi, i: (0, 0)),
    ]
    args = [x, mod4, w_cat, head_sum, gains]
    if rope:
        in_specs += [_resident((s, QK_WIDTH), lambda bi, i: (0, 0))] * 3
        args += list(rope_tabs)
    return pl.pallas_call(
        functools.partial(_in_proj_kernel, rope=rope, tiles=tiles),
        grid=(b // bb, s // tr),
        in_specs=in_specs,
        out_specs=out_specs,
        out_shape=out_shape,
        compiler_params=_params(2),
        name="in_proj_rope" if rope else "in_proj",
    )(*args)


def _attention_scores(q_ref, bu, row0, key_sets, sub_tile):
    nt = (((1,), (1,)), ((), ()))
    q = q_ref[bu, row0:row0 + sub_tile, :]
    items = []
    for j in range(N_KV_HEADS):
        kv = slice(j * HEAD_DIM, (j + 1) * HEAD_DIM)
        qs = jnp.concatenate(
            [q[:, h * HEAD_DIM:(h + 1) * HEAD_DIM] for h in range(j * Q_PER_KV, (j + 1) * Q_PER_KV)], axis=0)
        items.append([lax.dot_general(keys_ref[bu, :, kv].astype(BF16), qs, nt, preferred_element_type=F32)
                      for keys_ref, _ in key_sets])
    return items


def _attention_finish(items, bu, key_sets, attn_ref, arow0, sub_tile):
    ones_rows = jnp.ones((BF16_SUBLANES, KEY_CHUNK), BF16)
    for j, sts in enumerate(items):
        parts = []
        for st_all, (_, vts_ref) in zip(sts, key_sets):
            for c in range(0, st_all.shape[0], KEY_CHUNK):
                st = st_all[c:c + KEY_CHUNK]
                m = jnp.max(st, axis=0, keepdims=True)
                vt_ext = jnp.concatenate(
                    [vts_ref[bu, j * HEAD_DIM:(j + 1) * HEAD_DIM, c:c + KEY_CHUNK].astype(BF16), ones_rows], axis=0)
                parts.append((m, _dot(vt_ext, jnp.exp2(st - m).astype(BF16))))
        m_all = functools.reduce(jnp.maximum, [m for m, _ in parts])
        ot = (functools.reduce(jnp.add, [jnp.exp2(m - m_all) * o for m, o in parts])
              if len(parts) > 1 else parts[0][1])
        o = ot[0:HEAD_DIM] / ot[HEAD_DIM:HEAD_DIM + 1]
        for g in range(0, Q_PER_KV, 2):
            pair = jnp.concatenate([o[:, g * sub_tile:(g + 1) * sub_tile],
                                    o[:, (g + 1) * sub_tile:(g + 2) * sub_tile]], axis=0).T
            lane0 = (j * Q_PER_KV + g) * HEAD_DIM
            attn_ref[arow0:arow0 + sub_tile, lane0:lane0 + LANES] = pair.astype(attn_ref.dtype)


def _attn_out_kernel(*refs, has_cache, tiles, sub_tile):
    if has_cache:
        (q_ref, k_ref, vt_ref, ck_ref, cvt_ref, x_ref, fcs_ref, dc_ref, ds_ref, w_ref, mod_ref, lng_ref, lnb_ref,
         o_ref, attn_ref) = refs
        key_sets = [(k_ref, vt_ref), (ck_ref, cvt_ref)]
    else:
        (q_ref, k_ref, vt_ref, x_ref, fcs_ref, dc_ref, ds_ref, w_ref, mod_ref, lng_ref, lnb_ref,
         o_ref, attn_ref) = refs
        key_sets = [(k_ref, vt_ref)]
    gate = mod_ref[2:3, :]
    subs_per_tile = ROW_TILE // sub_tile
    subs = [(n, bu, ru, t) for n, (bu, ru) in enumerate(tiles) for t in range(subs_per_tile)]
    scores = lambda n, bu, ru, t: _attention_scores(q_ref, bu, ru + t * sub_tile, key_sets, sub_tile)

    four = None
    items_next = scores(*subs[0])
    for i, (n, bu, ru, t) in enumerate(subs):
        items = items_next
        if i + 1 < len(subs):
            items_next = scores(*subs[i + 1])
        if t == 0:
            pos = pl.ds(pl.multiple_of(pl.program_id(1) * x_ref.shape[1] + ru, ROW_TILE), ROW_TILE)
            four = (_dot(dc_ref[pos, :], fcs_ref[bu, :, :FOURIER_WIDTH])
                    - _dot(ds_ref[pos, :], fcs_ref[bu, :, FOURIER_WIDTH:]))
        _attention_finish(items, bu, key_sets, attn_ref, n * ROW_TILE + t * sub_tile, sub_tile)
        if t == subs_per_tile - 1:
            rows = slice(ru, ru + ROW_TILE)
            out = (_dot(four.astype(BF16), w_ref[:FOURIER_WIDTH, :])
                   + _dot(attn_ref[n * ROW_TILE:(n + 1) * ROW_TILE, :], w_ref[FOURIER_WIDTH:, :]))
            y = DEEPNORM_ALPHA * x_ref[bu, rows, :] + gate * out
            o_ref[bu, rows, :] = _layer_norm(y, lng_ref[0:1, :], lnb_ref[0:1, :])


def _attn_out(x, fcs, q, k, vt, cache, dft_c, dft_s, w_out, mod4, mod_row, ln_g, ln_b, layer):
    b, s, _ = x.shape
    has_cache = cache is not None
    bb, tr, tiles = _step_blocking(b, s)
    row_spec = lambda w: pl.BlockSpec((bb, tr, w), lambda bi, i: (bi, i, 0))
    seq_spec = lambda n, w: pl.BlockSpec((bb, n, w), lambda bi, i: (bi, 0, 0))
    seq_t_spec = lambda n: pl.BlockSpec((bb, KV_WIDTH, n), lambda bi, i: (bi, 0, 0))
    ln_spec = pl.BlockSpec((None, 2, D_MODEL), lambda bi, i: (layer, 0, 0))
    in_specs = [row_spec(ATTN_WIDTH), seq_spec(s, KV_WIDTH), seq_t_spec(s)]
    args = [q, k, vt]
    if has_cache:
        past = cache[0].shape[1]
        in_specs += [seq_spec(past, KV_WIDTH), seq_t_spec(past)]
        args += list(cache)
    in_specs += [
        row_spec(D_MODEL),
        seq_spec(s, 2 * FOURIER_WIDTH),
        _resident((s, s), lambda bi, i: (0, 0)),
        _resident((s, s), lambda bi, i: (0, 0)),
        _resident((2 * FOURIER_WIDTH, D_MODEL), lambda bi, i: (0, 0)),
        pl.BlockSpec((None, None, 6, D_MODEL), lambda bi, i: (layer, mod_row(bi * bb), 0, 0)),
        ln_spec, ln_spec,
    ]
    args += [x, fcs, dft_c, dft_s, w_out, mod4, ln_g, ln_b]
    return pl.pallas_call(
        functools.partial(_attn_out_kernel, has_cache=has_cache, tiles=tiles, sub_tile=ATTN_SUB_TILE),
        grid=(b // bb, s // tr),
        in_specs=in_specs,
        out_specs=row_spec(D_MODEL),
        out_shape=jax.ShapeDtypeStruct((b, s, D_MODEL), F32),
        scratch_shapes=[pltpu.VMEM((bb * tr, ATTN_WIDTH), BF16)],
        compiler_params=_params(2),
        name="attn_out_cached" if has_cache else "attn_out",
    )(*args)


def _load_permuted(slab_ref):
    slabs = [jnp.concatenate([slab_ref[c, pl.ds(v, SUBLANES, stride=PERM_STRIDE), :] for v in range(PERM_STRIDE)],
                             axis=0) for c in range(N_SLABS)]
    return jnp.concatenate(slabs, axis=1)


def _store_unpermuted(y, slab_ref):
    for c in range(N_SLABS):
        for v in range(PERM_STRIDE):
            slab_ref[c, pl.ds(v, SUBLANES, stride=PERM_STRIDE), :] = (
                y[v * SUBLANES:(v + 1) * SUBLANES, c * LANES:(c + 1) * LANES])


def _perm_row(q):
    return SUBLANES * (q % PERM_STRIDE) + q // PERM_STRIDE


def _stage_block(blk, tiles_per_seq, x_ref, xp_ref, xn_ref, shift, scale, xe_ref, h_ref):
    sublane = lax.broadcasted_iota(jnp.int32, (SUBLANES, 1), 0)
    rows = lax.broadcasted_iota(jnp.int32, (BLOCK_ROWS, 1), 0)
    keep = jnp.ones((BLOCK_ROWS, 1), F32)
    tiles = [blk * SEGS_PER_BLOCK + seg for seg in range(SEGS_PER_BLOCK)]

    def in_sequence(first_tile, second_tile):
        if tiles_per_seq == 1:
            return 0.0
        tile = pl.program_id(0) * CONV_TILES_PER_STEP + second_tile
        return jnp.where(lax.rem(tile, tiles_per_seq) != 0, 1.0, 0.0)

    for seg, u in enumerate(tiles):
        lo = u * ROW_TILE
        base = seg * SEG_ROWS
        w = tiles[seg - 1]
        after_lo = (w + 1) * ROW_TILE
        for c in range(N_SLABS):
            cols = slice(c * LANES, (c + 1) * LANES)
            before = xp_ref[:, cols] if u == 0 else x_ref[lo - HALO:lo, cols]
            after = xn_ref[0:1, cols] if w == CONV_TILES_PER_STEP - 1 else x_ref[after_lo:after_lo + 1, cols]
            xe_ref[blk, c, base:base + HALO, :] = jnp.where(sublane == 0, after, before)
            xe_ref[blk, c, base + HALO:base + SEG_ROWS, :] = x_ref[lo:lo + ROW_TILE, cols]
        for slot in range(HALO):
            value = in_sequence(u - 1, u) if slot == HALO - 1 else in_sequence(w, w + 1) if slot == 0 else 0.0
            keep = jnp.where(rows == _perm_row(base + slot), value, keep)
    h = (_load_permuted(xe_ref.at[blk]) * (1.0 + scale) + shift) * keep
    h_ref[blk] = h.astype(BF16)


def _conv3_permuted(t, cw):
    first_prev = pltpu.roll(t[BLOCK_ROWS - SUBLANES:BLOCK_ROWS], 1, 0)
    prev = jnp.concatenate([first_prev, t[0:BLOCK_ROWS - SUBLANES]], axis=0)
    last_next = pltpu.roll(t[0:SUBLANES], SUBLANES - 1, 0)
    nxt = jnp.concatenate([t[SUBLANES:BLOCK_ROWS], last_next], axis=0)
    return cw[0:1, :] * prev + cw[1:2, :] * t + cw[2:3, :] * nxt


def _finish_block(blk, xe_ref, acc, gate, lng, lnb, o_ref):
    y = DEEPNORM_ALPHA * _load_permuted(xe_ref.at[blk]) + gate * acc
    _store_unpermuted(_layer_norm(y, lng, lnb), xe_ref.at[blk])
    for seg in range(SEGS_PER_BLOCK):
        lo = (blk * SEGS_PER_BLOCK + seg) * ROW_TILE
        base = seg * SEG_ROWS
        for c in range(N_SLABS):
            o_ref[lo:lo + ROW_TILE, c * LANES:(c + 1) * LANES] = xe_ref[blk, c, base + HALO:base + SEG_ROWS, :]


def _ffn_kernel(x_ref, xp_ref, xn_ref, mod_ref, wup_ref, cw_ref, wdn_ref, lng_ref, lnb_ref, o_ref,
                xe_ref, h_ref, act_ref, *, tiles_per_seq):
    for blk in range(BLOCKS_PER_STEP):
        _stage_block(blk, tiles_per_seq, x_ref, xp_ref, xn_ref, mod_ref[3:4, :], mod_ref[4:5, :], xe_ref, h_ref)
        for j in range(N_FF_CHUNKS):
            a_cols = slice(j * FF_CHUNK, (j + 1) * FF_CHUNK)
            g_cols = slice(D_FF + j * FF_CHUNK, D_FF + (j + 1) * FF_CHUNK)
            a = _conv3_permuted(_dot(h_ref[blk], wup_ref[:, a_cols]), cw_ref[:, a_cols])
            g = _conv3_permuted(_dot(h_ref[blk], wup_ref[:, g_cols]), cw_ref[:, g_cols])
            act_ref[blk, :, a_cols] = (g * _sigmoid(g) * a).astype(BF16)
    for blk in range(BLOCKS_PER_STEP):
        acc = _dot(act_ref[blk], wdn_ref[...])
        _finish_block(blk, xe_ref, acc, mod_ref[5:6, :], lng_ref[1:2, :], lnb_ref[1:2, :], o_ref)


def _mixer_kernel(x_ref, xp_ref, xn_ref, mod_ref, win_ref, cw_ref, wout_ref, lng_ref, lnb_ref, o_ref,
                  xe_ref, h_ref, act_ref, *, tiles_per_seq):
    for blk in range(BLOCKS_PER_STEP):
        _stage_block(blk, tiles_per_seq, x_ref, xp_ref, xn_ref, mod_ref[0:1, :], mod_ref[1:2, :], xe_ref, h_ref)
        for j in range(N_MIX_CHUNKS):
            cols = slice(j * FF_CHUNK, (j + 1) * FF_CHUNK)
            bg, cg, xin = (_dot(h_ref[blk], win_ref[:, k * D_MODEL + j * FF_CHUNK:k * D_MODEL + (j + 1) * FF_CHUNK])
                           for k in range(3))
            y = bg * _conv3_permuted(cg * xin, cw_ref[:, cols])
            act_ref[blk, :, cols] = y.astype(BF16)
    for blk in range(BLOCKS_PER_STEP):
        acc = _dot(act_ref[blk], wout_ref[...])
        _finish_block(blk, xe_ref, acc, mod_ref[2:3, :], lng_ref[0:1, :], lnb_ref[0:1, :], o_ref)


def _gated_conv_block(kernel_fn, name, x, mod4, mod_row, w1, cw, w2, ln_g, ln_b, layer, w_layer):
    b, s, _ = x.shape
    hidden = w2.shape[1]
    n_rows = b * s
    step_rows = CONV_TILES_PER_STEP * ROW_TILE
    assert s % ROW_TILE == 0 and (s % step_rows == 0 or step_rows % s == 0)
    halo_blocks_per_step = step_rows // HALO
    last_halo_block = n_rows // HALO - 1
    ln_spec = pl.BlockSpec((None, 2, D_MODEL), lambda i: (layer, 0, 0))
    out = pl.pallas_call(
        functools.partial(kernel_fn, tiles_per_seq=s // ROW_TILE),
        grid=(n_rows // step_rows,),
        in_specs=[
            pl.BlockSpec((step_rows, D_MODEL), lambda i: (i, 0)),
            pl.BlockSpec((HALO, D_MODEL), lambda i: (jnp.maximum(i * halo_blocks_per_step - 1, 0), 0)),
            pl.BlockSpec((HALO, D_MODEL),
                         lambda i: (jnp.minimum((i + 1) * halo_blocks_per_step, last_halo_block), 0)),
            pl.BlockSpec((None, None, 6, D_MODEL), lambda i: (layer, mod_row(i * step_rows // s), 0, 0)),
            _resident((None,) + w1.shape[1:], lambda i: (w_layer, 0, 0)),
            _resident((None,) + cw.shape[1:], lambda i: (w_layer, 0, 0)),
            _resident((None,) + w2.shape[1:], lambda i: (w_layer, 0, 0)),
            ln_spec, ln_spec,
        ],
        out_specs=pl.BlockSpec((step_rows, D_MODEL), lambda i: (i, 0)),
        out_shape=jax.ShapeDtypeStruct((n_rows, D_MODEL), F32),
        scratch_shapes=[
            pltpu.VMEM((BLOCKS_PER_STEP, N_SLABS, BLOCK_ROWS, LANES), F32),
            pltpu.VMEM((BLOCKS_PER_STEP, BLOCK_ROWS, D_MODEL), BF16),
            pltpu.VMEM((BLOCKS_PER_STEP, BLOCK_ROWS, hidden), BF16),
        ],
        compiler_params=pltpu.CompilerParams(dimension_semantics=("arbitrary",),
                                             vmem_limit_bytes=CONV_VMEM_LIMIT_BYTES),
        name=name,
    )(*([x.reshape(n_rows, D_MODEL)] * 3), mod4, w1, cw, w2, ln_g, ln_b)
    return out.reshape(b, s, D_MODEL)


def _dft_tables(n, scale):
    jk = np.outer(np.arange(n), np.arange(n)) % n
    ang = 2.0 * np.pi * jk / n
    return (np.cos(ang) * scale).astype(np.float32), (np.sin(ang) * scale).astype(np.float32)


def _channel_dft_blocks():
    c, s = _dft_tables(FOURIER_GROUP, 1.0)
    eye = np.eye(MXU_WIDTH // FOURIER_GROUP, dtype=np.float32)
    return np.kron(eye, c), np.kron(eye, s)


def _rope_tables(n):
    half = HEAD_DIM // 2
    inv = 1.0 / (ROPE_THETA ** (np.arange(0, half, 2, dtype=np.float64) / half))
    pos = np.arange(n)
    d = np.arange(QK_WIDTH) % HEAD_DIM
    coord = np.where(d < half, (pos // GRID_W)[:, None], (pos % GRID_W)[:, None])
    ang = coord * inv[d % (half // 2)][None, :]
    first = (d % half) < (half // 2)
    cos = np.cos(ang)
    sa = np.where(first[None, :], -np.sin(ang), 0.0)
    sb = np.where(first[None, :], 0.0, np.sin(ang))
    return tuple(jnp.asarray(t.astype(np.float32)) for t in (cos, sa, sb))


def _head_sum_matrix():
    return np.kron(np.eye(QK_WIDTH // HEAD_DIM, dtype=np.float32), np.ones((HEAD_DIM, HEAD_DIM), np.float32))


def kernel(x_prompt, x_sample, cache_k, cache_v, c, c_ctx, w_ada, b_ada, ln_g, ln_b, w_in_a, q_norm_g,
           k_norm_g, w_out_a, w_in_c, conv_c, w_out_c, w_up, conv_f, w_down):
    n_prompt, s_prompt, _ = x_prompt.shape
    n_sample, s_sample, _ = x_sample.shape
    past = cache_k.shape[2]

    cond = jnp.concatenate(
        [c_ctx[None, :], c, jnp.zeros((N_COND_ROWS - 1 - n_sample, D_MODEL), F32)], axis=0)
    mod4 = _modulation(cond, w_ada, b_ada).reshape(DEPTH, N_COND_ROWS, 6, D_MODEL)

    chan_c, chan_s = _channel_dft_blocks()
    w_cat = _fold_in_proj(w_in_a, jnp.asarray(chan_c), jnp.asarray(chan_s))
    head_sum = jnp.asarray(_head_sum_matrix()).astype(BF16)
    gains = jnp.concatenate([jnp.tile(q_norm_g[0], N_HEADS), jnp.tile(k_norm_g[0], N_KV_HEADS)])[None, :]
    w_out_bf = w_out_a[0].astype(BF16)

    w_up_bf, w_down_bf = w_up, w_down
    w_in_c_bf, w_out_c_bf = w_in_c, w_out_c

    def run_stream(x, mod_row, cache):
        s = x.shape[1]
        scale = (FOURIER_GROUP * s) ** -0.5
        dft_c, dft_s = (jnp.asarray(t).astype(BF16) for t in _dft_tables(s, scale))
        rope_tabs = _rope_tables(s) if cache is not None else None
        fcs, q, k, vt, *v = _in_proj(x, mod4, mod_row, w_cat, head_sum, gains, rope_tabs)
        x = _attn_out(x, fcs, q, k, vt, cache, dft_c, dft_s, w_out_bf, mod4, mod_row, ln_g, ln_b, 0)
        x = _gated_conv_block(_ffn_kernel, "conv_ffn", x, mod4, mod_row, w_up_bf, conv_f, w_down_bf,
                              ln_g, ln_b, 0, 0)
        x = _gated_conv_block(_mixer_kernel, "conv_mixer", x, mod4, mod_row, w_in_c_bf, conv_c, w_out_c_bf,
                              ln_g, ln_b, 1, 0)
        x = _gated_conv_block(_ffn_kernel, "conv_ffn", x, mod4, mod_row, w_up_bf, conv_f, w_down_bf,
                              ln_g, ln_b, 1, 1)
        return x, k, v

    y_prompt, k_new, (v_new,) = run_stream(x_prompt, lambda bi: 0, None)
    cache = (cache_k[:, 0].reshape(n_sample, past, KV_WIDTH),
             cache_v[:, 0].reshape(n_sample, past, KV_WIDTH).transpose(0, 2, 1))
    y_sample, _, _ = run_stream(x_sample, lambda bi: bi + 1, cache)

    new_shape = (n_prompt, 1, s_prompt, N_KV_HEADS, HEAD_DIM)
    return y_prompt, y_sample, k_new.reshape(new_shape), v_new.reshape(new_shape)
```

```python
import functools

import numpy as np
import jax
import jax.numpy as jnp
from jax import lax
from jax.experimental import pallas as pl
from jax.experimental.pallas import tpu as pltpu

D_MODEL = 1024
DEPTH = 2
GRID_W = 64
HEAD_DIM = 64
N_HEADS = 8
N_KV_HEADS = 2
Q_PER_KV = N_HEADS // N_KV_HEADS
ATTN_WIDTH = N_HEADS * HEAD_DIM
KV_WIDTH = N_KV_HEADS * HEAD_DIM
QK_WIDTH = ATTN_WIDTH + KV_WIDTH
FOURIER_GROUP = 64
FOURIER_WIDTH = 512
D_FF = 2816
ROPE_THETA = 10000.0
EPS = 1e-6
DEEPNORM_ALPHA = (2 * DEPTH) ** 0.25
LOG2_E = 1.4426950408889634

F32 = jnp.float32
BF16 = jnp.bfloat16

SUBLANES = 8
BF16_SUBLANES = 16
LANES = 128
MXU_WIDTH = 256

ROW_TILE = 256
ATTN_SUB_TILE = 128
KEY_CHUNK = MXU_WIDTH
HALO = SUBLANES
SEG_ROWS = HALO + ROW_TILE
SEGS_PER_BLOCK = 2
BLOCK_ROWS = SEGS_PER_BLOCK * SEG_ROWS
PERM_STRIDE = BLOCK_ROWS // SUBLANES
BLOCKS_PER_STEP = 1
N_SLABS = D_MODEL // LANES
TILES_PER_STEP = 2
CONV_TILES_PER_STEP = BLOCKS_PER_STEP * SEGS_PER_BLOCK
FF_CHUNK = MXU_WIDTH
N_FF_CHUNKS = D_FF // FF_CHUNK
N_MIX_CHUNKS = D_MODEL // FF_CHUNK
PROJ_WIDTH = 2 * FOURIER_WIDTH + QK_WIDTH + KV_WIDTH
N_COND_ROWS = 16
VMEM_LIMIT_BYTES = 48 * 1024 * 1024
CONV_VMEM_LIMIT_BYTES = 56 * 1024 * 1024


def _dot(a, b):
    return lax.dot_general(a, b, (((a.ndim - 1,), (0,)), ((), ())), preferred_element_type=F32)


def _split_bf16(a):
    hi = a.astype(BF16)
    lo = (a - hi.astype(F32)).astype(BF16)
    return hi, lo


def _sigmoid(x):
    return 1.0 / (1.0 + jnp.exp(-x))


def _layer_norm(y, g, b):
    mu = jnp.mean(y, axis=-1, keepdims=True)
    yc = y - mu
    var = jnp.mean(yc * yc, axis=-1, keepdims=True)
    return yc * lax.rsqrt(var + EPS) * g + b


def _params(n_axes):
    return pltpu.CompilerParams(dimension_semantics=("arbitrary",) * n_axes,
                                vmem_limit_bytes=VMEM_LIMIT_BYTES)


def _resident(block, index_map):
    return pl.BlockSpec(block, index_map, pipeline_mode=pl.Buffered(1))


def _mod_kernel(cond_ref, w_ref, b_ref, o_ref):
    c = cond_ref[...]
    s = (c * _sigmoid(c)).astype(BF16)
    o_ref[...] = _dot(s, w_ref[...].astype(BF16)) + b_ref[...]


def _modulation(cond, w_ada, b_ada):
    tn = 1536
    n_out = 6 * D_MODEL
    return pl.pallas_call(
        _mod_kernel,
        grid=(DEPTH, n_out // tn),
        in_specs=[
            pl.BlockSpec((N_COND_ROWS, D_MODEL), lambda l, n: (0, 0)),
            pl.BlockSpec((None, D_MODEL, tn), lambda l, n: (l, 0, n)),
            pl.BlockSpec((None, 1, tn), lambda l, n: (l, 0, n)),
        ],
        out_specs=pl.BlockSpec((None, N_COND_ROWS, tn), lambda l, n: (l, 0, n)),
        out_shape=jax.ShapeDtypeStruct((DEPTH, N_COND_ROWS, n_out), F32),
        compiler_params=_params(2),
        name="modulation",
    )(cond, w_ada, b_ada.reshape(DEPTH, 1, n_out))


def _fold_kernel(w_ref, c_ref, s_ref, o_ref):
    m_c, m_s = _split_bf16(c_ref[...]), _split_bf16(s_ref[...])
    for j in range(FOURIER_WIDTH // MXU_WIDTH):
        cols = slice(j * MXU_WIDTH, (j + 1) * MXU_WIDTH)
        w_hi, w_lo = _split_bf16(w_ref[:, cols])
        for half, (m_hi, m_lo) in enumerate((m_c, m_s)):
            out_cols = slice(half * FOURIER_WIDTH + j * MXU_WIDTH, half * FOURIER_WIDTH + (j + 1) * MXU_WIDTH)
            o_ref[:, out_cols] = (_dot(w_hi, m_hi) + _dot(w_hi, m_lo) + _dot(w_lo, m_hi)).astype(BF16)
    o_ref[:, 2 * FOURIER_WIDTH:] = w_ref[:, FOURIER_WIDTH:].astype(BF16)


def _fold_in_proj(w_in_a, dft_c, dft_s):
    return pl.pallas_call(
        _fold_kernel,
        grid=(1,),
        in_specs=[
            pl.BlockSpec((None,) + w_in_a.shape[1:], lambda j: (0, 0, 0)),
            pl.BlockSpec((MXU_WIDTH, MXU_WIDTH), lambda j: (0, 0)),
            pl.BlockSpec((MXU_WIDTH, MXU_WIDTH), lambda j: (0, 0)),
        ],
        out_specs=pl.BlockSpec((D_MODEL, PROJ_WIDTH), lambda j: (0, 0)),
        out_shape=jax.ShapeDtypeStruct((D_MODEL, PROJ_WIDTH), BF16),
        compiler_params=_params(1),
        name="fold_in_proj",
    )(w_in_a, dft_c, dft_s)


def _step_blocking(b, s):
    rows = min(s, TILES_PER_STEP * ROW_TILE)
    batches = TILES_PER_STEP * ROW_TILE // rows
    assert s % rows == 0 and b % batches == 0 and rows % ROW_TILE == 0
    return batches, rows, [(bu, ru) for bu in range(batches) for ru in range(0, rows, ROW_TILE)]


def _in_proj_kernel(*refs, rope, tiles):
    if rope:
        (x_ref, mod_ref, w_ref, hs_ref, g_ref, cos_ref, sa_ref, sb_ref,
         fcs_ref, q_ref, k_ref, vt_ref) = refs
    else:
        x_ref, mod_ref, w_ref, hs_ref, g_ref, fcs_ref, q_ref, k_ref, vt_ref, v_ref = refs
    shift = mod_ref[0:1, :]
    scale = mod_ref[1:2, :]
    for bu, ru in tiles:
        rows = slice(ru, ru + ROW_TILE)
        h = (x_ref[bu, rows, :] * (1.0 + scale) + shift).astype(BF16)
        qk_cols = slice(2 * FOURIER_WIDTH, 2 * FOURIER_WIDTH + QK_WIDTH)
        qk = _dot(h, w_ref[:, qk_cols])
        ssq = _dot((qk * qk).astype(BF16), hs_ref[...])
        fcs_ref[bu, rows, :] = _dot(h, w_ref[:, :2 * FOURIER_WIDTH]).astype(BF16)
        v = _dot(h, w_ref[:, 2 * FOURIER_WIDTH + QK_WIDTH:])
        y = qk * lax.rsqrt(ssq * (1.0 / HEAD_DIM) + EPS) * g_ref[...]
        if rope:
            pos = pl.ds(pl.multiple_of(pl.program_id(1) * x_ref.shape[1] + ru, ROW_TILE), ROW_TILE)
            parts = []
            for t in range(QK_WIDTH // LANES):
                yt = y[:, t * LANES:(t + 1) * LANES]
                lo = slice(t * LANES, (t + 1) * LANES)
                parts.append(yt * cos_ref[pos, lo]
                             + pltpu.roll(yt, LANES - 16, 1) * sa_ref[pos, lo]
                             + pltpu.roll(yt, 16, 1) * sb_ref[pos, lo])
            y = jnp.concatenate(parts, axis=1)
        q_ref[bu, rows, :] = (y[:, :ATTN_WIDTH] * (HEAD_DIM ** -0.5 * LOG2_E)).astype(q_ref.dtype)
        k_ref[bu, rows, :] = y[:, ATTN_WIDTH:].astype(k_ref.dtype)
        vt_ref[bu, :, rows] = v.T.astype(vt_ref.dtype)
        if not rope:
            v_ref[bu, rows, :] = v


def _in_proj(x, mod4, mod_row, w_cat, head_sum, gains, rope_tabs):
    b, s, _ = x.shape
    rope = rope_tabs is not None
    bb, tr, tiles = _step_blocking(b, s)
    row_spec = lambda w: pl.BlockSpec((bb, tr, w), lambda bi, i: (bi, i, 0))
    out_specs = [row_spec(2 * FOURIER_WIDTH), row_spec(ATTN_WIDTH), row_spec(KV_WIDTH),
                 pl.BlockSpec((bb, KV_WIDTH, tr), lambda bi, i: (bi, 0, i))]
    out_shape = [
        jax.ShapeDtypeStruct((b, s, 2 * FOURIER_WIDTH), BF16),
        jax.ShapeDtypeStruct((b, s, ATTN_WIDTH), BF16),
        jax.ShapeDtypeStruct((b, s, KV_WIDTH), BF16 if rope else F32),
        jax.ShapeDtypeStruct((b, KV_WIDTH, s), BF16),
    ]
    if not rope:
        out_specs.append(row_spec(KV_WIDTH))
        out_shape.append(jax.ShapeDtypeStruct((b, s, KV_WIDTH), F32))
    in_specs = [
        row_spec(D_MODEL),
        pl.BlockSpec((None, None, 6, D_MODEL), lambda bi, i: (0, mod_row(bi * bb), 0, 0)),
        _resident((D_MODEL, PROJ_WIDTH), lambda bi, i: (0, 0)),
        _resident((QK_WIDTH, QK_WIDTH), lambda bi, i: (0, 0)),
        _resident((1, QK_WIDTH), lambda bi, i: (0, 0)),
    ]
    args = [x, mod4, w_cat, head_sum, gains]
    if rope:
        in_specs += [_resident((s, QK_WIDTH), lambda bi, i: (0, 0))] * 3
        args += list(rope_tabs)
    return pl.pallas_call(
        functools.partial(_in_proj_kernel, rope=rope, tiles=tiles),
        grid=(b // bb, s // tr),
        in_specs=in_specs,
        out_specs=out_specs,
        out_shape=out_shape,
        compiler_params=_params(2),
        name="in_proj_rope" if rope else "in_proj",
    )(*args)


def _attention_scores(q_ref, bu, row0, key_sets, sub_tile):
    nt = (((1,), (1,)), ((), ()))
    q = q_ref[bu, row0:row0 + sub_tile, :]
    items = []
    for j in range(N_KV_HEADS):
        kv = slice(j * HEAD_DIM, (j + 1) * HEAD_DIM)
        qs = jnp.concatenate(
            [q[:, h * HEAD_DIM:(h + 1) * HEAD_DIM] for h in range(j * Q_PER_KV, (j + 1) * Q_PER_KV)], axis=0)
        items.append([lax.dot_general(keys_ref[bu, :, kv].astype(BF16), qs, nt, preferred_element_type=F32)
                      for keys_ref, _ in key_sets])
    return items


def _attention_finish(items, bu, key_sets, attn_ref, arow0, sub_tile):
    ones_rows = jnp.ones((BF16_SUBLANES, KEY_CHUNK), BF16)
    for j, sts in enumerate(items):
        parts = []
        for st_all, (_, vts_ref) in zip(sts, key_sets):
            for c in range(0, st_all.shape[0], KEY_CHUNK):
                st = st_all[c:c + KEY_CHUNK]
                m = jnp.max(st, axis=0, keepdims=True)
                vt_ext = jnp.concatenate(
                    [vts_ref[bu, j * HEAD_DIM:(j + 1) * HEAD_DIM, c:c + KEY_CHUNK].astype(BF16), ones_rows], axis=0)
                parts.append((m, _dot(vt_ext, jnp.exp2(st - m).astype(BF16))))
        m_all = functools.reduce(jnp.maximum, [m for m, _ in parts])
        ot = (functools.reduce(jnp.add, [jnp.exp2(m - m_all) * o for m, o in parts])
              if len(parts) > 1 else parts[0][1])
        o = ot[0:HEAD_DIM] / ot[HEAD_DIM:HEAD_DIM + 1]
        for g in range(0, Q_PER_KV, 2):
            pair = jnp.concatenate([o[:, g * sub_tile:(g + 1) * sub_tile],
                                    o[:, (g + 1) * sub_tile:(g + 2) * sub_tile]], axis=0).T
            lane0 = (j * Q_PER_KV + g) * HEAD_DIM
            attn_ref[arow0:arow0 + sub_tile, lane0:lane0 + LANES] = pair.astype(attn_ref.dtype)


def _attn_out_kernel(*refs, has_cache, tiles, sub_tile):
    if has_cache:
        (q_ref, k_ref, vt_ref, ck_ref, cvt_ref, x_ref, fcs_ref, dc_ref, ds_ref, w_ref, mod_ref, lng_ref, lnb_ref,
         o_ref, attn_ref) = refs
        key_sets = [(k_ref, vt_ref), (ck_ref, cvt_ref)]
    else:
        (q_ref, k_ref, vt_ref, x_ref, fcs_ref, dc_ref, ds_ref, w_ref, mod_ref, lng_ref, lnb_ref,
         o_ref, attn_ref) = refs
        key_sets = [(k_ref, vt_ref)]
    gate = mod_ref[2:3, :]
    subs_per_tile = ROW_TILE // sub_tile
    subs = [(n, bu, ru, t) for n, (bu, ru) in enumerate(tiles) for t in range(subs_per_tile)]
    scores = lambda n, bu, ru, t: _attention_scores(q_ref, bu, ru + t * sub_tile, key_sets, sub_tile)

    four = None
    items_next = scores(*subs[0])
    for i, (n, bu, ru, t) in enumerate(subs):
        items = items_next
        if i + 1 < len(subs):
            items_next = scores(*subs[i + 1])
        if t == 0:
            pos = pl.ds(pl.multiple_of(pl.program_id(1) * x_ref.shape[1] + ru, ROW_TILE), ROW_TILE)
            four = (_dot(dc_ref[pos, :], fcs_ref[bu, :, :FOURIER_WIDTH])
                    - _dot(ds_ref[pos, :], fcs_ref[bu, :, FOURIER_WIDTH:]))
        _attention_finish(items, bu, key_sets, attn_ref, n * ROW_TILE + t * sub_tile, sub_tile)
        if t == subs_per_tile - 1:
            rows = slice(ru, ru + ROW_TILE)
            out = (_dot(four.astype(BF16), w_ref[:FOURIER_WIDTH, :])
                   + _dot(attn_ref[n * ROW_TILE:(n + 1) * ROW_TILE, :], w_ref[FOURIER_WIDTH:, :]))
            y = DEEPNORM_ALPHA * x_ref[bu, rows, :] + gate * out
            o_ref[bu, rows, :] = _layer_norm(y, lng_ref[0:1, :], lnb_ref[0:1, :])


def _attn_out(x, fcs, q, k, vt, cache, dft_c, dft_s, w_out, mod4, mod_row, ln_g, ln_b, layer):
    b, s, _ = x.shape
    has_cache = cache is not None
    bb, tr, tiles = _step_blocking(b, s)
    row_spec = lambda w: pl.BlockSpec((bb, tr, w), lambda bi, i: (bi, i, 0))
    seq_spec = lambda n, w: pl.BlockSpec((bb, n, w), lambda bi, i: (bi, 0, 0))
    seq_t_spec = lambda n: pl.BlockSpec((bb, KV_WIDTH, n), lambda bi, i: (bi, 0, 0))
    ln_spec = pl.BlockSpec((None, 2, D_MODEL), lambda bi, i: (layer, 0, 0))
    in_specs = [row_spec(ATTN_WIDTH), seq_spec(s, KV_WIDTH), seq_t_spec(s)]
    args = [q, k, vt]
    if has_cache:
        past = cache[0].shape[1]
        in_specs += [seq_spec(past, KV_WIDTH), seq_t_spec(past)]
        args += list(cache)
    in_specs += [
        row_spec(D_MODEL),
        seq_spec(s, 2 * FOURIER_WIDTH),
        _resident((s, s), lambda bi, i: (0, 0)),
        _resident((s, s), lambda bi, i: (0, 0)),
        _resident((None, 2 * FOURIER_WIDTH, D_MODEL), lambda bi, i: (0, 0, 0)),
        pl.BlockSpec((None, None, 6, D_MODEL), lambda bi, i: (layer, mod_row(bi * bb), 0, 0)),
        ln_spec, ln_spec,
    ]
    args += [x, fcs, dft_c, dft_s, w_out, mod4, ln_g, ln_b]
    return pl.pallas_call(
        functools.partial(_attn_out_kernel, has_cache=has_cache, tiles=tiles, sub_tile=ATTN_SUB_TILE),
        grid=(b // bb, s // tr),
        in_specs=in_specs,
        out_specs=row_spec(D_MODEL),
        out_shape=jax.ShapeDtypeStruct((b, s, D_MODEL), F32),
        scratch_shapes=[pltpu.VMEM((bb * tr, ATTN_WIDTH), BF16)],
        compiler_params=_params(2),
        name="attn_out_cached" if has_cache else "attn_out",
    )(*args)


def _load_permuted(slab_ref):
    slabs = [jnp.concatenate([slab_ref[c, pl.ds(v, SUBLANES, stride=PERM_STRIDE), :] for v in range(PERM_STRIDE)],
                             axis=0) for c in range(N_SLABS)]
    return jnp.concatenate(slabs, axis=1)


def _store_unpermuted(y, slab_ref):
    for c in range(N_SLABS):
        for v in range(PERM_STRIDE):
            slab_ref[c, pl.ds(v, SUBLANES, stride=PERM_STRIDE), :] = (
                y[v * SUBLANES:(v + 1) * SUBLANES, c * LANES:(c + 1) * LANES])


def _perm_row(q):
    return SUBLANES * (q % PERM_STRIDE) + q // PERM_STRIDE


def _stage_block(blk, tiles_per_seq, x_ref, xp_ref, xn_ref, shift, scale, xe_ref, h_ref):
    sublane = lax.broadcasted_iota(jnp.int32, (SUBLANES, 1), 0)
    rows = lax.broadcasted_iota(jnp.int32, (BLOCK_ROWS, 1), 0)
    keep = jnp.ones((BLOCK_ROWS, 1), F32)
    tiles = [blk * SEGS_PER_BLOCK + seg for seg in range(SEGS_PER_BLOCK)]

    def in_sequence(first_tile, second_tile):
        if tiles_per_seq == 1:
            return 0.0
        tile = pl.program_id(0) * CONV_TILES_PER_STEP + second_tile
        return jnp.where(lax.rem(tile, tiles_per_seq) != 0, 1.0, 0.0)

    for seg, u in enumerate(tiles):
        lo = u * ROW_TILE
        base = seg * SEG_ROWS
        w = tiles[seg - 1]
        after_lo = (w + 1) * ROW_TILE
        for c in range(N_SLABS):
            cols = slice(c * LANES, (c + 1) * LANES)
            before = xp_ref[:, cols] if u == 0 else x_ref[lo - HALO:lo, cols]
            after = xn_ref[0:1, cols] if w == CONV_TILES_PER_STEP - 1 else x_ref[after_lo:after_lo + 1, cols]
            xe_ref[blk, c, base:base + HALO, :] = jnp.where(sublane == 0, after, before)
            xe_ref[blk, c, base + HALO:base + SEG_ROWS, :] = x_ref[lo:lo + ROW_TILE, cols]
        for slot in range(HALO):
            value = in_sequence(u - 1, u) if slot == HALO - 1 else in_sequence(w, w + 1) if slot == 0 else 0.0
            keep = jnp.where(rows == _perm_row(base + slot), value, keep)
    h = (_load_permuted(xe_ref.at[blk]) * (1.0 + scale) + shift) * keep
    h_ref[blk] = h.astype(BF16)


def _conv3_permuted(t, cw):
    first_prev = pltpu.roll(t[BLOCK_ROWS - SUBLANES:BLOCK_ROWS], 1, 0)
    prev = jnp.concatenate([first_prev, t[0:BLOCK_ROWS - SUBLANES]], axis=0)
    last_next = pltpu.roll(t[0:SUBLANES], SUBLANES - 1, 0)
    nxt = jnp.concatenate([t[SUBLANES:BLOCK_ROWS], last_next], axis=0)
    return cw[0:1, :] * prev + cw[1:2, :] * t + cw[2:3, :] * nxt


def _finish_block(blk, xe_ref, acc, gate, lng, lnb, o_ref):
    y = DEEPNORM_ALPHA * _load_permuted(xe_ref.at[blk]) + gate * acc
    _store_unpermuted(_layer_norm(y, lng, lnb), xe_ref.at[blk])
    for seg in range(SEGS_PER_BLOCK):
        lo = (blk * SEGS_PER_BLOCK + seg) * ROW_TILE
        base = seg * SEG_ROWS
        for c in range(N_SLABS):
            o_ref[lo:lo + ROW_TILE, c * LANES:(c + 1) * LANES] = xe_ref[blk, c, base + HALO:base + SEG_ROWS, :]


def _ffn_kernel(x_ref, xp_ref, xn_ref, mod_ref, wup_ref, cw_ref, wdn_ref, lng_ref, lnb_ref, o_ref,
                xe_ref, h_ref, act_ref, *, tiles_per_seq):
    for blk in range(BLOCKS_PER_STEP):
        _stage_block(blk, tiles_per_seq, x_ref, xp_ref, xn_ref, mod_ref[3:4, :], mod_ref[4:5, :], xe_ref, h_ref)
        for j in range(N_FF_CHUNKS):
            a_cols = slice(j * FF_CHUNK, (j + 1) * FF_CHUNK)
            g_cols = slice(D_FF + j * FF_CHUNK, D_FF + (j + 1) * FF_CHUNK)
            a = _conv3_permuted(_dot(h_ref[blk], wup_ref[:, a_cols]), cw_ref[:, a_cols])
            g = _conv3_permuted(_dot(h_ref[blk], wup_ref[:, g_cols]), cw_ref[:, g_cols])
            act_ref[blk, :, a_cols] = (g * _sigmoid(g) * a).astype(BF16)
    for blk in range(BLOCKS_PER_STEP):
        acc = _dot(act_ref[blk], wdn_ref[...])
        _finish_block(blk, xe_ref, acc, mod_ref[5:6, :], lng_ref[1:2, :], lnb_ref[1:2, :], o_ref)


def _mixer_kernel(x_ref, xp_ref, xn_ref, mod_ref, win_ref, cw_ref, wout_ref, lng_ref, lnb_ref, o_ref,
                  xe_ref, h_ref, act_ref, *, tiles_per_seq):
    for blk in range(BLOCKS_PER_STEP):
        _stage_block(blk, tiles_per_seq, x_ref, xp_ref, xn_ref, mod_ref[0:1, :], mod_ref[1:2, :], xe_ref, h_ref)
        for j in range(N_MIX_CHUNKS):
            cols = slice(j * FF_CHUNK, (j + 1) * FF_CHUNK)
            bg, cg, xin = (_dot(h_ref[blk], win_ref[:, k * D_MODEL + j * FF_CHUNK:k * D_MODEL + (j + 1) * FF_CHUNK])
                           for k in range(3))
            y = bg * _conv3_permuted(cg * xin, cw_ref[:, cols])
            act_ref[blk, :, cols] = y.astype(BF16)
    for blk in range(BLOCKS_PER_STEP):
        acc = _dot(act_ref[blk], wout_ref[...])
        _finish_block(blk, xe_ref, acc, mod_ref[2:3, :], lng_ref[0:1, :], lnb_ref[0:1, :], o_ref)


def _gated_conv_block(kernel_fn, name, x, mod4, mod_row, w1, cw, w2, ln_g, ln_b, layer, w_layer):
    b, s, _ = x.shape
    hidden = w2.shape[1]
    n_rows = b * s
    step_rows = CONV_TILES_PER_STEP * ROW_TILE
    assert s % ROW_TILE == 0 and (s % step_rows == 0 or step_rows % s == 0)
    halo_blocks_per_step = step_rows // HALO
    last_halo_block = n_rows // HALO - 1
    ln_spec = pl.BlockSpec((None, 2, D_MODEL), lambda i: (layer, 0, 0))
    out = pl.pallas_call(
        functools.partial(kernel_fn, tiles_per_seq=s // ROW_TILE),
        grid=(n_rows // step_rows,),
        in_specs=[
            pl.BlockSpec((step_rows, D_MODEL), lambda i: (i, 0)),
            pl.BlockSpec((HALO, D_MODEL), lambda i: (jnp.maximum(i * halo_blocks_per_step - 1, 0), 0)),
            pl.BlockSpec((HALO, D_MODEL),
                         lambda i: (jnp.minimum((i + 1) * halo_blocks_per_step, last_halo_block), 0)),
            pl.BlockSpec((None, None, 6, D_MODEL), lambda i: (layer, mod_row(i * step_rows // s), 0, 0)),
            _resident((None,) + w1.shape[1:], lambda i: (w_layer, 0, 0)),
            _resident((None,) + cw.shape[1:], lambda i: (w_layer, 0, 0)),
            _resident((None,) + w2.shape[1:], lambda i: (w_layer, 0, 0)),
            ln_spec, ln_spec,
        ],
        out_specs=pl.BlockSpec((step_rows, D_MODEL), lambda i: (i, 0)),
        out_shape=jax.ShapeDtypeStruct((n_rows, D_MODEL), F32),
        scratch_shapes=[
            pltpu.VMEM((BLOCKS_PER_STEP, N_SLABS, BLOCK_ROWS, LANES), F32),
            pltpu.VMEM((BLOCKS_PER_STEP, BLOCK_ROWS, D_MODEL), BF16),
            pltpu.VMEM((BLOCKS_PER_STEP, BLOCK_ROWS, hidden), BF16),
        ],
        compiler_params=pltpu.CompilerParams(dimension_semantics=("arbitrary",),
                                             vmem_limit_bytes=CONV_VMEM_LIMIT_BYTES),
        name=name,
    )(*([x.reshape(n_rows, D_MODEL)] * 3), mod4, w1, cw, w2, ln_g, ln_b)
    return out.reshape(b, s, D_MODEL)


def _dft_tables(n, scale):
    jk = np.outer(np.arange(n), np.arange(n)) % n
    ang = 2.0 * np.pi * jk / n
    return (np.cos(ang) * scale).astype(np.float32), (np.sin(ang) * scale).astype(np.float32)


def _channel_dft_blocks():
    c, s = _dft_tables(FOURIER_GROUP, 1.0)
    eye = np.eye(MXU_WIDTH // FOURIER_GROUP, dtype=np.float32)
    return np.kron(eye, c), np.kron(eye, s)


def _rope_tables(n):
    half = HEAD_DIM // 2
    inv = 1.0 / (ROPE_THETA ** (np.arange(0, half, 2, dtype=np.float64) / half))
    pos = np.arange(n)
    d = np.arange(QK_WIDTH) % HEAD_DIM
    coord = np.where(d < half, (pos // GRID_W)[:, None], (pos % GRID_W)[:, None])
    ang = coord * inv[d % (half // 2)][None, :]
    first = (d % half) < (half // 2)
    cos = np.cos(ang)
    sa = np.where(first[None, :], -np.sin(ang), 0.0)
    sb = np.where(first[None, :], 0.0, np.sin(ang))
    return tuple(jnp.asarray(t.astype(np.float32)) for t in (cos, sa, sb))


def _head_sum_matrix():
    return np.kron(np.eye(QK_WIDTH // HEAD_DIM, dtype=np.float32), np.ones((HEAD_DIM, HEAD_DIM), np.float32))


def kernel(x_prompt, x_sample, cache_k, cache_v, c, c_ctx, w_ada, b_ada, ln_g, ln_b, w_in_a, q_norm_g,
           k_norm_g, w_out_a, w_in_c, conv_c, w_out_c, w_up, conv_f, w_down):
    n_prompt, s_prompt, _ = x_prompt.shape
    n_sample, s_sample, _ = x_sample.shape
    past = cache_k.shape[2]

    cond = jnp.concatenate(
        [c_ctx[None, :], c, jnp.zeros((N_COND_ROWS - 1 - n_sample, D_MODEL), F32)], axis=0)
    mod4 = _modulation(cond, w_ada, b_ada).reshape(DEPTH, N_COND_ROWS, 6, D_MODEL)

    chan_c, chan_s = _channel_dft_blocks()
    w_cat = _fold_in_proj(w_in_a, jnp.asarray(chan_c), jnp.asarray(chan_s))
    head_sum = jnp.asarray(_head_sum_matrix()).astype(BF16)
    gains = jnp.concatenate([jnp.tile(q_norm_g[0], N_HEADS), jnp.tile(k_norm_g[0], N_KV_HEADS)])[None, :]

    def run_stream(x, mod_row, cache):
        s = x.shape[1]
        scale = (FOURIER_GROUP * s) ** -0.5
        dft_c, dft_s = (jnp.asarray(t).astype(BF16) for t in _dft_tables(s, scale))
        rope_tabs = _rope_tables(s) if cache is not None else None
        fcs, q, k, vt, *v = _in_proj(x, mod4, mod_row, w_cat, head_sum, gains, rope_tabs)
        x = _attn_out(x, fcs, q, k, vt, cache, dft_c, dft_s, w_out_a, mod4, mod_row, ln_g, ln_b, 0)
        x = _gated_conv_block(_ffn_kernel, "conv_ffn", x, mod4, mod_row, w_up, conv_f, w_down,
                              ln_g, ln_b, 0, 0)
        x = _gated_conv_block(_mixer_kernel, "conv_mixer", x, mod4, mod_row, w_in_c, conv_c, w_out_c,
                              ln_g, ln_b, 1, 0)
        x = _gated_conv_block(_ffn_kernel, "conv_ffn", x, mod4, mod_row, w_up, conv_f, w_down,
                              ln_g, ln_b, 1, 1)
        return x, k, v

    y_prompt, k_new, (v_new,) = run_stream(x_prompt, lambda bi: 0, None)
    cache = (cache_k[:, 0].reshape(n_sample, past, KV_WIDTH),
             cache_v[:, 0].reshape(n_sample, past, KV_WIDTH).transpose(0, 2, 1))
    y_sample, _, _ = run_stream(x_sample, lambda bi: bi + 1, cache)

    new_shape = (n_prompt, 1, s_prompt, N_KV_HEADS, HEAD_DIM)
    return y_prompt, y_sample, k_new.reshape(new_shape), v_new.reshape(new_shape)
```

```python
import functools

import numpy as np
import jax
import jax.numpy as jnp
from jax import lax
from jax.experimental import pallas as pl
from jax.experimental.pallas import tpu as pltpu

D_MODEL = 1024
DEPTH = 2
GRID_W = 64
HEAD_DIM = 64
N_HEADS = 8
N_KV_HEADS = 2
Q_PER_KV = N_HEADS // N_KV_HEADS
ATTN_WIDTH = N_HEADS * HEAD_DIM
KV_WIDTH = N_KV_HEADS * HEAD_DIM
QK_WIDTH = ATTN_WIDTH + KV_WIDTH
FOURIER_GROUP = 64
FOURIER_WIDTH = 512
D_FF = 2816
ROPE_THETA = 10000.0
EPS = 1e-6
DEEPNORM_ALPHA = (2 * DEPTH) ** 0.25
LOG2_E = 1.4426950408889634

F32 = jnp.float32
BF16 = jnp.bfloat16

SUBLANES = 8
BF16_SUBLANES = 16
LANES = 128
MXU_WIDTH = 256

ROW_TILE = 256
ATTN_SUB_TILE = 128
KEY_CHUNK = MXU_WIDTH
HALO = SUBLANES
SEG_ROWS = HALO + ROW_TILE
SEGS_PER_BLOCK = 2
BLOCK_ROWS = SEGS_PER_BLOCK * SEG_ROWS
PERM_STRIDE = BLOCK_ROWS // SUBLANES
BLOCKS_PER_STEP = 1
N_SLABS = D_MODEL // LANES
TILES_PER_STEP = 2
CONV_TILES_PER_STEP = BLOCKS_PER_STEP * SEGS_PER_BLOCK
FF_CHUNK = MXU_WIDTH
N_FF_CHUNKS = D_FF // FF_CHUNK
N_MIX_CHUNKS = D_MODEL // FF_CHUNK
PROJ_WIDTH = 2 * FOURIER_WIDTH + QK_WIDTH + KV_WIDTH
N_COND_ROWS = 16
VMEM_LIMIT_BYTES = 48 * 1024 * 1024
CONV_VMEM_LIMIT_BYTES = 56 * 1024 * 1024


def _dot(a, b):
    return lax.dot_general(a, b, (((a.ndim - 1,), (0,)), ((), ())), preferred_element_type=F32)


def _split_bf16(a):
    hi = a.astype(BF16)
    lo = (a - hi.astype(F32)).astype(BF16)
    return hi, lo


def _sigmoid(x):
    return 1.0 / (1.0 + jnp.exp(-x))


def _layer_norm(y, g, b):
    mu = jnp.mean(y, axis=-1, keepdims=True)
    yc = y - mu
    var = jnp.mean(yc * yc, axis=-1, keepdims=True)
    return yc * lax.rsqrt(var + EPS) * g + b


def _params(n_axes):
    return pltpu.CompilerParams(dimension_semantics=("arbitrary",) * n_axes,
                                vmem_limit_bytes=VMEM_LIMIT_BYTES)


def _resident(block, index_map):
    return pl.BlockSpec(block, index_map, pipeline_mode=pl.Buffered(1))


def _mod_kernel(cond_ref, w_ref, b_ref, o_ref):
    c = cond_ref[...]
    s = (c * _sigmoid(c)).astype(BF16)
    o_ref[...] = _dot(s, w_ref[...].astype(BF16)) + b_ref[...]


def _modulation(cond, w_ada, b_ada):
    tn = 1536
    n_out = 6 * D_MODEL
    return pl.pallas_call(
        _mod_kernel,
        grid=(DEPTH, n_out // tn),
        in_specs=[
            pl.BlockSpec((N_COND_ROWS, D_MODEL), lambda l, n: (0, 0)),
            pl.BlockSpec((None, D_MODEL, tn), lambda l, n: (l, 0, n)),
            pl.BlockSpec((None, 1, tn), lambda l, n: (l, 0, n)),
        ],
        out_specs=pl.BlockSpec((None, N_COND_ROWS, tn), lambda l, n: (l, 0, n)),
        out_shape=jax.ShapeDtypeStruct((DEPTH, N_COND_ROWS, n_out), F32),
        compiler_params=_params(2),
        name="modulation",
    )(cond, w_ada, b_ada.reshape(DEPTH, 1, n_out))


def _fold_kernel(w_ref, c_ref, s_ref, o_ref):
    m_c, m_s = _split_bf16(c_ref[...]), _split_bf16(s_ref[...])
    for j in range(FOURIER_WIDTH // MXU_WIDTH):
        cols = slice(j * MXU_WIDTH, (j + 1) * MXU_WIDTH)
        w_hi, w_lo = _split_bf16(w_ref[:, cols])
        for half, (m_hi, m_lo) in enumerate((m_c, m_s)):
            out_cols = slice(half * FOURIER_WIDTH + j * MXU_WIDTH, half * FOURIER_WIDTH + (j + 1) * MXU_WIDTH)
            o_ref[:, out_cols] = (_dot(w_hi, m_hi) + _dot(w_hi, m_lo) + _dot(w_lo, m_hi)).astype(BF16)
    o_ref[:, 2 * FOURIER_WIDTH:] = w_ref[:, FOURIER_WIDTH:].astype(BF16)


def _fold_in_proj(w_in_a, dft_c, dft_s):
    return pl.pallas_call(
        _fold_kernel,
        grid=(1,),
        in_specs=[
            pl.BlockSpec((None,) + w_in_a.shape[1:], lambda j: (0, 0, 0)),
            pl.BlockSpec((MXU_WIDTH, MXU_WIDTH), lambda j: (0, 0)),
            pl.BlockSpec((MXU_WIDTH, MXU_WIDTH), lambda j: (0, 0)),
        ],
        out_specs=pl.BlockSpec((D_MODEL, PROJ_WIDTH), lambda j: (0, 0)),
        out_shape=jax.ShapeDtypeStruct((D_MODEL, PROJ_WIDTH), BF16),
        compiler_params=_params(1),
        name="fold_in_proj",
    )(w_in_a, dft_c, dft_s)


def _sequence_blocking(b, s):
    batches = max(1, TILES_PER_STEP * ROW_TILE // s)
    assert b % batches == 0 and s % ROW_TILE == 0
    return batches, [(bu, ru) for bu in range(batches) for ru in range(0, s, ROW_TILE)]


def _project_tile(x_ref, bu, ru, shift, scale, w_ref, hs_ref, g_ref, rope_refs, fcs_ref, q_ref, k_ref, vt_ref, v_ref):
    rows = slice(ru, ru + ROW_TILE)
    h = (x_ref[bu, rows, :] * (1.0 + scale) + shift).astype(BF16)
    qk_cols = slice(2 * FOURIER_WIDTH, 2 * FOURIER_WIDTH + QK_WIDTH)
    qk = _dot(h, w_ref[:, qk_cols])
    ssq = _dot((qk * qk).astype(BF16), hs_ref[...])
    fcs_ref[bu, rows, :] = _dot(h, w_ref[:, :2 * FOURIER_WIDTH]).astype(BF16)
    v = _dot(h, w_ref[:, 2 * FOURIER_WIDTH + QK_WIDTH:])
    y = qk * lax.rsqrt(ssq * (1.0 / HEAD_DIM) + EPS) * g_ref[...]
    if rope_refs is not None:
        cos_ref, sa_ref, sb_ref = rope_refs
        parts = []
        for t in range(QK_WIDTH // LANES):
            yt = y[:, t * LANES:(t + 1) * LANES]
            parts.append(yt * cos_ref[rows, :]
                         + pltpu.roll(yt, LANES - 16, 1) * sa_ref[rows, :]
                         + pltpu.roll(yt, 16, 1) * sb_ref[rows, :])
        y = jnp.concatenate(parts, axis=1)
    q_ref[bu, rows, :] = (y[:, :ATTN_WIDTH] * (HEAD_DIM ** -0.5 * LOG2_E)).astype(q_ref.dtype)
    k_ref[bu, rows, :] = y[:, ATTN_WIDTH:].astype(k_ref.dtype)
    vt_ref[bu, :, rows] = v.T.astype(vt_ref.dtype)
    if v_ref is not None:
        v_ref[bu, rows, :] = v


def _attention_scores(q_ref, bu, row0, key_sets, sub_tile):
    nt = (((1,), (1,)), ((), ()))
    q = q_ref[bu, row0:row0 + sub_tile, :]
    items = []
    for j in range(N_KV_HEADS):
        kv = slice(j * HEAD_DIM, (j + 1) * HEAD_DIM)
        qs = jnp.concatenate(
            [q[:, h * HEAD_DIM:(h + 1) * HEAD_DIM] for h in range(j * Q_PER_KV, (j + 1) * Q_PER_KV)], axis=0)
        items.append([lax.dot_general(keys_ref[bu, :, kv].astype(BF16), qs, nt, preferred_element_type=F32)
                      for keys_ref, _ in key_sets])
    return items


def _attention_finish(items, bu, key_sets, attn_ref, arow0, sub_tile):
    ones_rows = jnp.ones((BF16_SUBLANES, KEY_CHUNK), BF16)
    for j, sts in enumerate(items):
        parts = []
        for st_all, (_, vts_ref) in zip(sts, key_sets):
            for c in range(0, st_all.shape[0], KEY_CHUNK):
                st = st_all[c:c + KEY_CHUNK]
                m = jnp.max(st, axis=0, keepdims=True)
                vt_ext = jnp.concatenate(
                    [vts_ref[bu, j * HEAD_DIM:(j + 1) * HEAD_DIM, c:c + KEY_CHUNK].astype(BF16), ones_rows], axis=0)
                parts.append((m, _dot(vt_ext, jnp.exp2(st - m).astype(BF16))))
        m_all = functools.reduce(jnp.maximum, [m for m, _ in parts])
        ot = (functools.reduce(jnp.add, [jnp.exp2(m - m_all) * o for m, o in parts])
              if len(parts) > 1 else parts[0][1])
        o = ot[0:HEAD_DIM] / ot[HEAD_DIM:HEAD_DIM + 1]
        for g in range(0, Q_PER_KV, 2):
            pair = jnp.concatenate([o[:, g * sub_tile:(g + 1) * sub_tile],
                                    o[:, (g + 1) * sub_tile:(g + 2) * sub_tile]], axis=0).T
            lane0 = (j * Q_PER_KV + g) * HEAD_DIM
            attn_ref[arow0:arow0 + sub_tile, lane0:lane0 + LANES] = pair.astype(attn_ref.dtype)


def _mixer0_kernel(*refs, rope, has_cache, tiles, sub_tile):
    refs = list(refs)
    take = lambda n: [refs.pop(0) for _ in range(n)]
    x_ref, mod_ref, w_ref, hs_ref, g_ref = take(5)
    rope_refs = take(3) if rope else None
    cache_refs = take(2) if has_cache else None
    dc_ref, ds_ref, wo_ref, lng_ref, lnb_ref, o_ref = take(6)
    if rope:
        fcs_ref, q_ref, k_ref, vt_ref, attn_ref = refs
        v_ref = None
    else:
        k_ref, v_ref, fcs_ref, q_ref, vt_ref, attn_ref = refs
    shift, scale, gate = mod_ref[0:1, :], mod_ref[1:2, :], mod_ref[2:3, :]
    for bu, ru in tiles:
        _project_tile(x_ref, bu, ru, shift, scale, w_ref, hs_ref, g_ref, rope_refs,
                      fcs_ref, q_ref, k_ref, vt_ref, v_ref)

    key_sets = [(k_ref, vt_ref)] + ([tuple(cache_refs)] if has_cache else [])
    subs_per_tile = ROW_TILE // sub_tile
    subs = [(n, bu, ru, t) for n, (bu, ru) in enumerate(tiles) for t in range(subs_per_tile)]
    scores = lambda n, bu, ru, t: _attention_scores(q_ref, bu, ru + t * sub_tile, key_sets, sub_tile)

    four = None
    items_next = scores(*subs[0])
    for i, (n, bu, ru, t) in enumerate(subs):
        items = items_next
        if i + 1 < len(subs):
            items_next = scores(*subs[i + 1])
        rows = slice(ru, ru + ROW_TILE)
        if t == 0:
            four = (_dot(dc_ref[rows, :], fcs_ref[bu, :, :FOURIER_WIDTH])
                    - _dot(ds_ref[rows, :], fcs_ref[bu, :, FOURIER_WIDTH:]))
        _attention_finish(items, bu, key_sets, attn_ref, n * ROW_TILE + t * sub_tile, sub_tile)
        if t == subs_per_tile - 1:
            out = (_dot(four.astype(BF16), wo_ref[:FOURIER_WIDTH, :])
                   + _dot(attn_ref[n * ROW_TILE:(n + 1) * ROW_TILE, :], wo_ref[FOURIER_WIDTH:, :]))
            y = DEEPNORM_ALPHA * x_ref[bu, rows, :] + gate * out
            o_ref[bu, rows, :] = _layer_norm(y, lng_ref[0:1, :], lnb_ref[0:1, :])


def _mixer0(x, mod4, mod_row, w_cat, head_sum, gains, rope_tabs, cache, dft_c, dft_s, w_out, ln_g, ln_b):
    b, s, _ = x.shape
    rope = rope_tabs is not None
    has_cache = cache is not None
    bb, tiles = _sequence_blocking(b, s)
    seq_spec = lambda n, w: pl.BlockSpec((bb, n, w), lambda bi: (bi, 0, 0))
    const = lambda shape: _resident(shape, lambda bi: (0,) * len(shape))
    ln_spec = pl.BlockSpec((None, 2, D_MODEL), lambda bi: (0, 0, 0))
    in_specs = [
        seq_spec(s, D_MODEL),
        pl.BlockSpec((None, None, 6, D_MODEL), lambda bi: (0, mod_row(bi * bb), 0, 0)),
        const((D_MODEL, PROJ_WIDTH)), const((QK_WIDTH, QK_WIDTH)), const((1, QK_WIDTH)),
    ]
    args = [x, mod4, w_cat, head_sum, gains]
    if rope:
        in_specs += [const((s, LANES))] * 3
        args += list(rope_tabs)
    if has_cache:
        past = cache[0].shape[1]
        in_specs += [seq_spec(past, KV_WIDTH), pl.BlockSpec((bb, KV_WIDTH, past), lambda bi: (bi, 0, 0))]
        args += list(cache)
    in_specs += [const((s, s)), const((s, s)), const((None, 2 * FOURIER_WIDTH, D_MODEL)), ln_spec, ln_spec]
    args += [dft_c, dft_s, w_out, ln_g, ln_b]
    out_specs = [seq_spec(s, D_MODEL)]
    out_shape = [jax.ShapeDtypeStruct((b, s, D_MODEL), F32)]
    scratch = [pltpu.VMEM((bb, s, 2 * FOURIER_WIDTH), BF16), pltpu.VMEM((bb, s, ATTN_WIDTH), BF16)]
    if rope:
        scratch.append(pltpu.VMEM((bb, s, KV_WIDTH), BF16))
    else:
        out_specs += [seq_spec(s, KV_WIDTH)] * 2
        out_shape += [jax.ShapeDtypeStruct((b, s, KV_WIDTH), F32)] * 2
    scratch += [pltpu.VMEM((bb, KV_WIDTH, s), BF16), pltpu.VMEM((bb * s, ATTN_WIDTH), BF16)]
    return pl.pallas_call(
        functools.partial(_mixer0_kernel, rope=rope, has_cache=has_cache, tiles=tiles, sub_tile=ATTN_SUB_TILE),
        grid=(b // bb,),
        in_specs=in_specs,
        out_specs=out_specs,
        out_shape=out_shape,
        scratch_shapes=scratch,
        compiler_params=pltpu.CompilerParams(dimension_semantics=("arbitrary",),
                                             vmem_limit_bytes=CONV_VMEM_LIMIT_BYTES),
        name="mixer0_cached" if has_cache else "mixer0",
    )(*args)


def _load_permuted(slab_ref):
    slabs = [jnp.concatenate([slab_ref[c, pl.ds(v, SUBLANES, stride=PERM_STRIDE), :] for v in range(PERM_STRIDE)],
                             axis=0) for c in range(N_SLABS)]
    return jnp.concatenate(slabs, axis=1)


def _store_unpermuted(y, slab_ref):
    for c in range(N_SLABS):
        for v in range(PERM_STRIDE):
            slab_ref[c, pl.ds(v, SUBLANES, stride=PERM_STRIDE), :] = (
                y[v * SUBLANES:(v + 1) * SUBLANES, c * LANES:(c + 1) * LANES])


def _perm_row(q):
    return SUBLANES * (q % PERM_STRIDE) + q // PERM_STRIDE


def _stage_block(blk, tiles_per_seq, x_ref, xp_ref, xn_ref, shift, scale, xe_ref, h_ref):
    sublane = lax.broadcasted_iota(jnp.int32, (SUBLANES, 1), 0)
    rows = lax.broadcasted_iota(jnp.int32, (BLOCK_ROWS, 1), 0)
    keep = jnp.ones((BLOCK_ROWS, 1), F32)
    tiles = [blk * SEGS_PER_BLOCK + seg for seg in range(SEGS_PER_BLOCK)]

    def in_sequence(first_tile, second_tile):
        if tiles_per_seq == 1:
            return 0.0
        tile = pl.program_id(0) * CONV_TILES_PER_STEP + second_tile
        return jnp.where(lax.rem(tile, tiles_per_seq) != 0, 1.0, 0.0)

    for seg, u in enumerate(tiles):
        lo = u * ROW_TILE
        base = seg * SEG_ROWS
        w = tiles[seg - 1]
        after_lo = (w + 1) * ROW_TILE
        for c in range(N_SLABS):
            cols = slice(c * LANES, (c + 1) * LANES)
            before = xp_ref[:, cols] if u == 0 else x_ref[lo - HALO:lo, cols]
            after = xn_ref[0:1, cols] if w == CONV_TILES_PER_STEP - 1 else x_ref[after_lo:after_lo + 1, cols]
            xe_ref[blk, c, base:base + HALO, :] = jnp.where(sublane == 0, after, before)
            xe_ref[blk, c, base + HALO:base + SEG_ROWS, :] = x_ref[lo:lo + ROW_TILE, cols]
        for slot in range(HALO):
            value = in_sequence(u - 1, u) if slot == HALO - 1 else in_sequence(w, w + 1) if slot == 0 else 0.0
            keep = jnp.where(rows == _perm_row(base + slot), value, keep)
    h = (_load_permuted(xe_ref.at[blk]) * (1.0 + scale) + shift) * keep
    h_ref[blk] = h.astype(BF16)


def _conv3_permuted(t, cw):
    first_prev = pltpu.roll(t[BLOCK_ROWS - SUBLANES:BLOCK_ROWS], 1, 0)
    prev = jnp.concatenate([first_prev, t[0:BLOCK_ROWS - SUBLANES]], axis=0)
    last_next = pltpu.roll(t[0:SUBLANES], SUBLANES - 1, 0)
    nxt = jnp.concatenate([t[SUBLANES:BLOCK_ROWS], last_next], axis=0)
    return cw[0:1, :] * prev + cw[1:2, :] * t + cw[2:3, :] * nxt


def _finish_block(blk, xe_ref, acc, gate, lng, lnb, o_ref):
    y = DEEPNORM_ALPHA * _load_permuted(xe_ref.at[blk]) + gate * acc
    _store_unpermuted(_layer_norm(y, lng, lnb), xe_ref.at[blk])
    for seg in range(SEGS_PER_BLOCK):
        lo = (blk * SEGS_PER_BLOCK + seg) * ROW_TILE
        base = seg * SEG_ROWS
        for c in range(N_SLABS):
            o_ref[lo:lo + ROW_TILE, c * LANES:(c + 1) * LANES] = xe_ref[blk, c, base + HALO:base + SEG_ROWS, :]


def _ffn_kernel(x_ref, xp_ref, xn_ref, mod_ref, wup_ref, cw_ref, wdn_ref, lng_ref, lnb_ref, o_ref,
                xe_ref, h_ref, act_ref, *, tiles_per_seq):
    for blk in range(BLOCKS_PER_STEP):
        _stage_block(blk, tiles_per_seq, x_ref, xp_ref, xn_ref, mod_ref[3:4, :], mod_ref[4:5, :], xe_ref, h_ref)
        for j in range(N_FF_CHUNKS):
            a_cols = slice(j * FF_CHUNK, (j + 1) * FF_CHUNK)
            g_cols = slice(D_FF + j * FF_CHUNK, D_FF + (j + 1) * FF_CHUNK)
            a = _conv3_permuted(_dot(h_ref[blk], wup_ref[:, a_cols]), cw_ref[:, a_cols])
            g = _conv3_permuted(_dot(h_ref[blk], wup_ref[:, g_cols]), cw_ref[:, g_cols])
            act_ref[blk, :, a_cols] = (g * _sigmoid(g) * a).astype(BF16)
    for blk in range(BLOCKS_PER_STEP):
        acc = _dot(act_ref[blk], wdn_ref[...])
        _finish_block(blk, xe_ref, acc, mod_ref[5:6, :], lng_ref[1:2, :], lnb_ref[1:2, :], o_ref)


def _mixer_kernel(x_ref, xp_ref, xn_ref, mod_ref, win_ref, cw_ref, wout_ref, lng_ref, lnb_ref, o_ref,
                  xe_ref, h_ref, act_ref, *, tiles_per_seq):
    for blk in range(BLOCKS_PER_STEP):
        _stage_block(blk, tiles_per_seq, x_ref, xp_ref, xn_ref, mod_ref[0:1, :], mod_ref[1:2, :], xe_ref, h_ref)
        for j in range(N_MIX_CHUNKS):
            cols = slice(j * FF_CHUNK, (j + 1) * FF_CHUNK)
            bg, cg, xin = (_dot(h_ref[blk], win_ref[:, k * D_MODEL + j * FF_CHUNK:k * D_MODEL + (j + 1) * FF_CHUNK])
                           for k in range(3))
            y = bg * _conv3_permuted(cg * xin, cw_ref[:, cols])
            act_ref[blk, :, cols] = y.astype(BF16)
    for blk in range(BLOCKS_PER_STEP):
        acc = _dot(act_ref[blk], wout_ref[...])
        _finish_block(blk, xe_ref, acc, mod_ref[2:3, :], lng_ref[0:1, :], lnb_ref[0:1, :], o_ref)


def _gated_conv_block(kernel_fn, name, x, mod4, mod_row, w1, cw, w2, ln_g, ln_b, layer, w_layer):
    b, s, _ = x.shape
    hidden = w2.shape[1]
    n_rows = b * s
    step_rows = CONV_TILES_PER_STEP * ROW_TILE
    assert s % ROW_TILE == 0 and (s % step_rows == 0 or step_rows % s == 0)
    halo_blocks_per_step = step_rows // HALO
    last_halo_block = n_rows // HALO - 1
    ln_spec = pl.BlockSpec((None, 2, D_MODEL), lambda i: (layer, 0, 0))
    out = pl.pallas_call(
        functools.partial(kernel_fn, tiles_per_seq=s // ROW_TILE),
        grid=(n_rows // step_rows,),
        in_specs=[
            pl.BlockSpec((step_rows, D_MODEL), lambda i: (i, 0)),
            pl.BlockSpec((HALO, D_MODEL), lambda i: (jnp.maximum(i * halo_blocks_per_step - 1, 0), 0)),
            pl.BlockSpec((HALO, D_MODEL),
                         lambda i: (jnp.minimum((i + 1) * halo_blocks_per_step, last_halo_block), 0)),
            pl.BlockSpec((None, None, 6, D_MODEL), lambda i: (layer, mod_row(i * step_rows // s), 0, 0)),
            _resident((None,) + w1.shape[1:], lambda i: (w_layer, 0, 0)),
            _resident((None,) + cw.shape[1:], lambda i: (w_layer, 0, 0)),
            _resident((None,) + w2.shape[1:], lambda i: (w_layer, 0, 0)),
            ln_spec, ln_spec,
        ],
        out_specs=pl.BlockSpec((step_rows, D_MODEL), lambda i: (i, 0)),
        out_shape=jax.ShapeDtypeStruct((n_rows, D_MODEL), F32),
        scratch_shapes=[
            pltpu.VMEM((BLOCKS_PER_STEP, N_SLABS, BLOCK_ROWS, LANES), F32),
            pltpu.VMEM((BLOCKS_PER_STEP, BLOCK_ROWS, D_MODEL), BF16),
            pltpu.VMEM((BLOCKS_PER_STEP, BLOCK_ROWS, hidden), BF16),
        ],
        compiler_params=pltpu.CompilerParams(dimension_semantics=("arbitrary",),
                                             vmem_limit_bytes=CONV_VMEM_LIMIT_BYTES),
        name=name,
    )(*([x.reshape(n_rows, D_MODEL)] * 3), mod4, w1, cw, w2, ln_g, ln_b)
    return out.reshape(b, s, D_MODEL)


def _dft_tables(n, scale):
    jk = np.outer(np.arange(n), np.arange(n)) % n
    ang = 2.0 * np.pi * jk / n
    return (np.cos(ang) * scale).astype(np.float32), (np.sin(ang) * scale).astype(np.float32)


def _channel_dft_blocks():
    c, s = _dft_tables(FOURIER_GROUP, 1.0)
    eye = np.eye(MXU_WIDTH // FOURIER_GROUP, dtype=np.float32)
    return np.kron(eye, c), np.kron(eye, s)


def _rope_tables(n):
    half = HEAD_DIM // 2
    inv = 1.0 / (ROPE_THETA ** (np.arange(0, half, 2, dtype=np.float64) / half))
    pos = np.arange(n)
    d = np.arange(LANES) % HEAD_DIM
    coord = np.where(d < half, (pos // GRID_W)[:, None], (pos % GRID_W)[:, None])
    ang = coord * inv[d % (half // 2)][None, :]
    first = (d % half) < (half // 2)
    cos = np.cos(ang)
    sa = np.where(first[None, :], -np.sin(ang), 0.0)
    sb = np.where(first[None, :], 0.0, np.sin(ang))
    return tuple(jnp.asarray(t.astype(np.float32)) for t in (cos, sa, sb))


def _head_sum_matrix():
    return np.kron(np.eye(QK_WIDTH // HEAD_DIM, dtype=np.float32), np.ones((HEAD_DIM, HEAD_DIM), np.float32))


def kernel(x_prompt, x_sample, cache_k, cache_v, c, c_ctx, w_ada, b_ada, ln_g, ln_b, w_in_a, q_norm_g,
           k_norm_g, w_out_a, w_in_c, conv_c, w_out_c, w_up, conv_f, w_down):
    n_prompt, s_prompt, _ = x_prompt.shape
    n_sample, s_sample, _ = x_sample.shape
    past = cache_k.shape[2]

    cond = jnp.concatenate(
        [c_ctx[None, :], c, jnp.zeros((N_COND_ROWS - 1 - n_sample, D_MODEL), F32)], axis=0)
    mod4 = _modulation(cond, w_ada, b_ada).reshape(DEPTH, N_COND_ROWS, 6, D_MODEL)

    chan_c, chan_s = _channel_dft_blocks()
    w_cat = _fold_in_proj(w_in_a, jnp.asarray(chan_c), jnp.asarray(chan_s))
    head_sum = jnp.asarray(_head_sum_matrix()).astype(BF16)
    gains = jnp.concatenate([jnp.tile(q_norm_g[0], N_HEADS), jnp.tile(k_norm_g[0], N_KV_HEADS)])[None, :]

    def run_stream(x, mod_row, cache):
        s = x.shape[1]
        scale = (FOURIER_GROUP * s) ** -0.5
        dft_c, dft_s = (jnp.asarray(t).astype(BF16) for t in _dft_tables(s, scale))
        rope_tabs = _rope_tables(s) if cache is not None else None
        x, *kv = _mixer0(x, mod4, mod_row, w_cat, head_sum, gains, rope_tabs, cache, dft_c, dft_s, w_out_a,
                         ln_g, ln_b)
        x = _gated_conv_block(_ffn_kernel, "conv_ffn", x, mod4, mod_row, w_up, conv_f, w_down,
                              ln_g, ln_b, 0, 0)
        x = _gated_conv_block(_mixer_kernel, "conv_mixer", x, mod4, mod_row, w_in_c, conv_c, w_out_c,
                              ln_g, ln_b, 1, 0)
        x = _gated_conv_block(_ffn_kernel, "conv_ffn", x, mod4, mod_row, w_up, conv_f, w_down,
                              ln_g, ln_b, 1, 1)
        return x, kv

    y_prompt, (k_new, v_new) = run_stream(x_prompt, lambda bi: 0, None)
    cache = (cache_k[:, 0].reshape(n_sample, past, KV_WIDTH),
             cache_v[:, 0].reshape(n_sample, past, KV_WIDTH).transpose(0, 2, 1))
    y_sample, _ = run_stream(x_sample, lambda bi: bi + 1, cache)

    new_shape = (n_prompt, 1, s_prompt, N_KV_HEADS, HEAD_DIM)
    return y_prompt, y_sample, k_new.reshape(new_shape), v_new.reshape(new_shape)
```

```python
import functools

import numpy as np
import jax
import jax.numpy as jnp
from jax import lax
from jax.experimental import pallas as pl
from jax.experimental.pallas import tpu as pltpu

D_MODEL = 1024
DEPTH = 2
GRID_W = 64
HEAD_DIM = 64
N_HEADS = 8
N_KV_HEADS = 2
Q_PER_KV = N_HEADS // N_KV_HEADS
ATTN_WIDTH = N_HEADS * HEAD_DIM
KV_WIDTH = N_KV_HEADS * HEAD_DIM
QK_WIDTH = ATTN_WIDTH + KV_WIDTH
FOURIER_GROUP = 64
FOURIER_WIDTH = 512
D_FF = 2816
ROPE_THETA = 10000.0
EPS = 1e-6
DEEPNORM_ALPHA = (2 * DEPTH) ** 0.25
LOG2_E = 1.4426950408889634

F32 = jnp.float32
BF16 = jnp.bfloat16

SUBLANES = 8
BF16_SUBLANES = 16
LANES = 128
MXU_WIDTH = 256

ROW_TILE = 256
ATTN_SUB_TILE = 128
KEY_CHUNK = MXU_WIDTH
HALO = SUBLANES
SEG_ROWS = HALO + ROW_TILE
SEGS_PER_BLOCK = 2
BLOCK_ROWS = SEGS_PER_BLOCK * SEG_ROWS
PERM_STRIDE = BLOCK_ROWS // SUBLANES
BLOCKS_PER_STEP = 1
N_SLABS = D_MODEL // LANES
TILES_PER_STEP = 2
CONV_TILES_PER_STEP = BLOCKS_PER_STEP * SEGS_PER_BLOCK
FF_CHUNK = MXU_WIDTH
N_FF_CHUNKS = D_FF // FF_CHUNK
N_MIX_CHUNKS = D_MODEL // FF_CHUNK
PROJ_WIDTH = 2 * FOURIER_WIDTH + QK_WIDTH + KV_WIDTH
N_COND_ROWS = 16
VMEM_LIMIT_BYTES = 48 * 1024 * 1024
CONV_VMEM_LIMIT_BYTES = 56 * 1024 * 1024


def _dot(a, b):
    return lax.dot_general(a, b, (((a.ndim - 1,), (0,)), ((), ())), preferred_element_type=F32)


def _split_bf16(a):
    hi = a.astype(BF16)
    lo = (a - hi.astype(F32)).astype(BF16)
    return hi, lo


def _sigmoid(x):
    return 1.0 / (1.0 + jnp.exp(-x))


def _layer_norm(y, g, b):
    mu = jnp.mean(y, axis=-1, keepdims=True)
    yc = y - mu
    var = jnp.mean(yc * yc, axis=-1, keepdims=True)
    return yc * lax.rsqrt(var + EPS) * g + b


def _params(n_axes):
    return pltpu.CompilerParams(dimension_semantics=("arbitrary",) * n_axes,
                                vmem_limit_bytes=VMEM_LIMIT_BYTES)


def _resident(block, index_map):
    return pl.BlockSpec(block, index_map, pipeline_mode=pl.Buffered(1))


def _mod_kernel(cond_ref, w_ref, b_ref, o_ref):
    c = cond_ref[...]
    s = (c * _sigmoid(c)).astype(BF16)
    o_ref[...] = _dot(s, w_ref[...].astype(BF16)) + b_ref[...]


def _modulation(cond, w_ada, b_ada):
    tn = 1536
    n_out = 6 * D_MODEL
    return pl.pallas_call(
        _mod_kernel,
        grid=(DEPTH, n_out // tn),
        in_specs=[
            pl.BlockSpec((N_COND_ROWS, D_MODEL), lambda l, n: (0, 0)),
            pl.BlockSpec((None, D_MODEL, tn), lambda l, n: (l, 0, n)),
            pl.BlockSpec((None, 1, tn), lambda l, n: (l, 0, n)),
        ],
        out_specs=pl.BlockSpec((None, N_COND_ROWS, tn), lambda l, n: (l, 0, n)),
        out_shape=jax.ShapeDtypeStruct((DEPTH, N_COND_ROWS, n_out), F32),
        compiler_params=_params(2),
        name="modulation",
    )(cond, w_ada, b_ada.reshape(DEPTH, 1, n_out))


def _fold_kernel(w_ref, c_ref, s_ref, o_ref):
    m_c, m_s = _split_bf16(c_ref[...]), _split_bf16(s_ref[...])
    for j in range(FOURIER_WIDTH // MXU_WIDTH):
        cols = slice(j * MXU_WIDTH, (j + 1) * MXU_WIDTH)
        w_hi, w_lo = _split_bf16(w_ref[:, cols])
        for half, (m_hi, m_lo) in enumerate((m_c, m_s)):
            out_cols = slice(half * FOURIER_WIDTH + j * MXU_WIDTH, half * FOURIER_WIDTH + (j + 1) * MXU_WIDTH)
            o_ref[:, out_cols] = (_dot(w_hi, m_hi) + _dot(w_hi, m_lo) + _dot(w_lo, m_hi)).astype(BF16)
    o_ref[:, 2 * FOURIER_WIDTH:] = w_ref[:, FOURIER_WIDTH:].astype(BF16)


def _fold_in_proj(w_in_a, dft_c, dft_s):
    return pl.pallas_call(
        _fold_kernel,
        grid=(1,),
        in_specs=[
            pl.BlockSpec((None,) + w_in_a.shape[1:], lambda j: (0, 0, 0)),
            pl.BlockSpec((MXU_WIDTH, MXU_WIDTH), lambda j: (0, 0)),
            pl.BlockSpec((MXU_WIDTH, MXU_WIDTH), lambda j: (0, 0)),
        ],
        out_specs=pl.BlockSpec((D_MODEL, PROJ_WIDTH), lambda j: (0, 0)),
        out_shape=jax.ShapeDtypeStruct((D_MODEL, PROJ_WIDTH), BF16),
        compiler_params=_params(1),
        name="fold_in_proj",
    )(w_in_a, dft_c, dft_s)


def _sequence_blocking(b, s):
    batches = max(1, TILES_PER_STEP * ROW_TILE // s)
    assert b % batches == 0 and s % ROW_TILE == 0
    return batches, [(bu, ru) for bu in range(batches) for ru in range(0, s, ROW_TILE)]


def _normed_heads(h, w_ref, cols, hs_ref, g_ref, rope_refs, rows):
    lo = cols.start - 2 * FOURIER_WIDTH
    qk_cols = slice(lo, lo + cols.stop - cols.start)
    y = _dot(h, w_ref[:, cols])
    ssq = _dot((y * y).astype(BF16), hs_ref[qk_cols, qk_cols])
    y = y * lax.rsqrt(ssq * (1.0 / HEAD_DIM) + EPS) * g_ref[:, qk_cols]
    if rope_refs is not None:
        cos_ref, sa_ref, sb_ref = rope_refs
        parts = []
        for t in range(y.shape[1] // LANES):
            yt = y[:, t * LANES:(t + 1) * LANES]
            parts.append(yt * cos_ref[rows, :]
                         + pltpu.roll(yt, LANES - 16, 1) * sa_ref[rows, :]
                         + pltpu.roll(yt, 16, 1) * sb_ref[rows, :])
        y = jnp.concatenate(parts, axis=1)
    return y


def _project_kv(x_ref, bu, ru, shift, scale, w_ref, hs_ref, g_ref, rope_refs, fcs_ref, k_ref, vt_ref, v_ref):
    rows = slice(ru, ru + ROW_TILE)
    h = (x_ref[bu, rows, :] * (1.0 + scale) + shift).astype(BF16)
    k_cols = slice(2 * FOURIER_WIDTH + ATTN_WIDTH, 2 * FOURIER_WIDTH + QK_WIDTH)
    k_ref[bu, rows, :] = _normed_heads(h, w_ref, k_cols, hs_ref, g_ref, rope_refs, rows).astype(k_ref.dtype)
    v = _dot(h, w_ref[:, 2 * FOURIER_WIDTH + QK_WIDTH:])
    vt_ref[bu, :, rows] = v.T.astype(vt_ref.dtype)
    if v_ref is not None:
        v_ref[bu, rows, :] = v
    fcs_ref[bu, rows, :] = _dot(h, w_ref[:, :2 * FOURIER_WIDTH]).astype(BF16)


def _project_q(x_ref, bu, ru, shift, scale, w_ref, hs_ref, g_ref, rope_refs, q_ref):
    rows = slice(ru, ru + ROW_TILE)
    h = (x_ref[bu, rows, :] * (1.0 + scale) + shift).astype(BF16)
    q_cols = slice(2 * FOURIER_WIDTH, 2 * FOURIER_WIDTH + ATTN_WIDTH)
    y = _normed_heads(h, w_ref, q_cols, hs_ref, g_ref, rope_refs, rows)
    q_ref[bu, rows, :] = (y * (HEAD_DIM ** -0.5 * LOG2_E)).astype(q_ref.dtype)


def _attention_scores(q_ref, bu, row0, key_sets, sub_tile):
    nt = (((1,), (1,)), ((), ()))
    q = q_ref[bu, row0:row0 + sub_tile, :]
    items = []
    for j in range(N_KV_HEADS):
        kv = slice(j * HEAD_DIM, (j + 1) * HEAD_DIM)
        qs = jnp.concatenate(
            [q[:, h * HEAD_DIM:(h + 1) * HEAD_DIM] for h in range(j * Q_PER_KV, (j + 1) * Q_PER_KV)], axis=0)
        items.append([lax.dot_general(keys_ref[bu, :, kv].astype(BF16), qs, nt, preferred_element_type=F32)
                      for keys_ref, _ in key_sets])
    return items


def _attention_finish(items, bu, key_sets, attn_ref, arow0, sub_tile):
    ones_rows = jnp.ones((BF16_SUBLANES, KEY_CHUNK), BF16)
    for j, sts in enumerate(items):
        parts = []
        for st_all, (_, vts_ref) in zip(sts, key_sets):
            for c in range(0, st_all.shape[0], KEY_CHUNK):
                st = st_all[c:c + KEY_CHUNK]
                m = jnp.max(st, axis=0, keepdims=True)
                vt_ext = jnp.concatenate(
                    [vts_ref[bu, j * HEAD_DIM:(j + 1) * HEAD_DIM, c:c + KEY_CHUNK].astype(BF16), ones_rows], axis=0)
                parts.append((m, _dot(vt_ext, jnp.exp2(st - m).astype(BF16))))
        m_all = functools.reduce(jnp.maximum, [m for m, _ in parts])
        ot = (functools.reduce(jnp.add, [jnp.exp2(m - m_all) * o for m, o in parts])
              if len(parts) > 1 else parts[0][1])
        o = ot[0:HEAD_DIM] / ot[HEAD_DIM:HEAD_DIM + 1]
        for g in range(0, Q_PER_KV, 2):
            pair = jnp.concatenate([o[:, g * sub_tile:(g + 1) * sub_tile],
                                    o[:, (g + 1) * sub_tile:(g + 2) * sub_tile]], axis=0).T
            lane0 = (j * Q_PER_KV + g) * HEAD_DIM
            attn_ref[arow0:arow0 + sub_tile, lane0:lane0 + LANES] = pair.astype(attn_ref.dtype)


def _mixer0_kernel(*refs, rope, has_cache, tiles, sub_tile):
    refs = list(refs)
    take = lambda n: [refs.pop(0) for _ in range(n)]
    x_ref, mod_ref, w_ref, hs_ref, g_ref = take(5)
    rope_refs = take(3) if rope else None
    cache_refs = take(2) if has_cache else None
    dc_ref, ds_ref, wo_ref, lng_ref, lnb_ref, o_ref = take(6)
    if rope:
        fcs_ref, q_ref, k_ref, vt_ref, attn_ref = refs
        v_ref = None
    else:
        k_ref, v_ref, fcs_ref, q_ref, vt_ref, attn_ref = refs
    shift, scale, gate = mod_ref[0:1, :], mod_ref[1:2, :], mod_ref[2:3, :]
    project_q = lambda bu, ru: _project_q(x_ref, bu, ru, shift, scale, w_ref, hs_ref, g_ref, rope_refs, q_ref)
    for bu, ru in tiles:
        _project_kv(x_ref, bu, ru, shift, scale, w_ref, hs_ref, g_ref, rope_refs, fcs_ref, k_ref, vt_ref, v_ref)
    project_q(*tiles[0])

    key_sets = [(k_ref, vt_ref)] + ([tuple(cache_refs)] if has_cache else [])
    subs_per_tile = ROW_TILE // sub_tile
    subs = [(n, bu, ru, t) for n, (bu, ru) in enumerate(tiles) for t in range(subs_per_tile)]
    scores = lambda n, bu, ru, t: _attention_scores(q_ref, bu, ru + t * sub_tile, key_sets, sub_tile)

    four = None
    items_next = scores(*subs[0])
    for i, (n, bu, ru, t) in enumerate(subs):
        items = items_next
        if t == 0 and n + 1 < len(tiles):
            project_q(*tiles[n + 1])
        if i + 1 < len(subs):
            items_next = scores(*subs[i + 1])
        rows = slice(ru, ru + ROW_TILE)
        if t == 0:
            four = (_dot(dc_ref[rows, :], fcs_ref[bu, :, :FOURIER_WIDTH])
                    - _dot(ds_ref[rows, :], fcs_ref[bu, :, FOURIER_WIDTH:]))
        _attention_finish(items, bu, key_sets, attn_ref, n * ROW_TILE + t * sub_tile, sub_tile)
        if t == subs_per_tile - 1:
            out = (_dot(four.astype(BF16), wo_ref[:FOURIER_WIDTH, :])
                   + _dot(attn_ref[n * ROW_TILE:(n + 1) * ROW_TILE, :], wo_ref[FOURIER_WIDTH:, :]))
            y = DEEPNORM_ALPHA * x_ref[bu, rows, :] + gate * out
            o_ref[bu, rows, :] = _layer_norm(y, lng_ref[0:1, :], lnb_ref[0:1, :])


def _mixer0(x, mod4, mod_row, w_cat, head_sum, gains, rope_tabs, cache, dft_c, dft_s, w_out, ln_g, ln_b):
    b, s, _ = x.shape
    rope = rope_tabs is not None
    has_cache = cache is not None
    bb, tiles = _sequence_blocking(b, s)
    seq_spec = lambda n, w: pl.BlockSpec((bb, n, w), lambda bi: (bi, 0, 0))
    const = lambda shape: _resident(shape, lambda bi: (0,) * len(shape))
    ln_spec = pl.BlockSpec((None, 2, D_MODEL), lambda bi: (0, 0, 0))
    in_specs = [
        seq_spec(s, D_MODEL),
        pl.BlockSpec((None, None, 6, D_MODEL), lambda bi: (0, mod_row(bi * bb), 0, 0)),
        const((D_MODEL, PROJ_WIDTH)), const((QK_WIDTH, QK_WIDTH)), const((1, QK_WIDTH)),
    ]
    args = [x, mod4, w_cat, head_sum, gains]
    if rope:
        in_specs += [const((s, LANES))] * 3
        args += list(rope_tabs)
    if has_cache:
        past = cache[0].shape[1]
        in_specs += [seq_spec(past, KV_WIDTH), pl.BlockSpec((bb, KV_WIDTH, past), lambda bi: (bi, 0, 0))]
        args += list(cache)
    in_specs += [const((s, s)), const((s, s)), const((None, 2 * FOURIER_WIDTH, D_MODEL)), ln_spec, ln_spec]
    args += [dft_c, dft_s, w_out, ln_g, ln_b]
    out_specs = [seq_spec(s, D_MODEL)]
    out_shape = [jax.ShapeDtypeStruct((b, s, D_MODEL), F32)]
    scratch = [pltpu.VMEM((bb, s, 2 * FOURIER_WIDTH), BF16), pltpu.VMEM((bb, s, ATTN_WIDTH), BF16)]
    if rope:
        scratch.append(pltpu.VMEM((bb, s, KV_WIDTH), BF16))
    else:
        out_specs += [seq_spec(s, KV_WIDTH)] * 2
        out_shape += [jax.ShapeDtypeStruct((b, s, KV_WIDTH), F32)] * 2
    scratch += [pltpu.VMEM((bb, KV_WIDTH, s), BF16), pltpu.VMEM((bb * s, ATTN_WIDTH), BF16)]
    return pl.pallas_call(
        functools.partial(_mixer0_kernel, rope=rope, has_cache=has_cache, tiles=tiles, sub_tile=ATTN_SUB_TILE),
        grid=(b // bb,),
        in_specs=in_specs,
        out_specs=out_specs,
        out_shape=out_shape,
        scratch_shapes=scratch,
        compiler_params=pltpu.CompilerParams(dimension_semantics=("arbitrary",),
                                             vmem_limit_bytes=CONV_VMEM_LIMIT_BYTES),
        name="mixer0_cached" if has_cache else "mixer0",
    )(*args)


def _load_permuted(slab_ref):
    slabs = [jnp.concatenate([slab_ref[c, pl.ds(v, SUBLANES, stride=PERM_STRIDE), :] for v in range(PERM_STRIDE)],
                             axis=0) for c in range(N_SLABS)]
    return jnp.concatenate(slabs, axis=1)


def _store_unpermuted(y, slab_ref):
    for c in range(N_SLABS):
        for v in range(PERM_STRIDE):
            slab_ref[c, pl.ds(v, SUBLANES, stride=PERM_STRIDE), :] = (
                y[v * SUBLANES:(v + 1) * SUBLANES, c * LANES:(c + 1) * LANES])


def _perm_row(q):
    return SUBLANES * (q % PERM_STRIDE) + q // PERM_STRIDE


def _stage_block(blk, tiles_per_seq, x_ref, xp_ref, xn_ref, shift, scale, xe_ref, h_ref):
    sublane = lax.broadcasted_iota(jnp.int32, (SUBLANES, 1), 0)
    rows = lax.broadcasted_iota(jnp.int32, (BLOCK_ROWS, 1), 0)
    keep = jnp.ones((BLOCK_ROWS, 1), F32)
    tiles = [blk * SEGS_PER_BLOCK + seg for seg in range(SEGS_PER_BLOCK)]

    def in_sequence(first_tile, second_tile):
        if tiles_per_seq == 1:
            return 0.0
        tile = pl.program_id(0) * CONV_TILES_PER_STEP + second_tile
        return jnp.where(lax.rem(tile, tiles_per_seq) != 0, 1.0, 0.0)

    for seg, u in enumerate(tiles):
        lo = u * ROW_TILE
        base = seg * SEG_ROWS
        w = tiles[seg - 1]
        after_lo = (w + 1) * ROW_TILE
        for c in range(N_SLABS):
            cols = slice(c * LANES, (c + 1) * LANES)
            before = xp_ref[:, cols] if u == 0 else x_ref[lo - HALO:lo, cols]
            after = xn_ref[0:1, cols] if w == CONV_TILES_PER_STEP - 1 else x_ref[after_lo:after_lo + 1, cols]
            xe_ref[blk, c, base:base + HALO, :] = jnp.where(sublane == 0, after, before)
            xe_ref[blk, c, base + HALO:base + SEG_ROWS, :] = x_ref[lo:lo + ROW_TILE, cols]
        for slot in range(HALO):
            value = in_sequence(u - 1, u) if slot == HALO - 1 else in_sequence(w, w + 1) if slot == 0 else 0.0
            keep = jnp.where(rows == _perm_row(base + slot), value, keep)
    h = (_load_permuted(xe_ref.at[blk]) * (1.0 + scale) + shift) * keep
    h_ref[blk] = h.astype(BF16)


def _conv3_permuted(t, cw):
    first_prev = pltpu.roll(t[BLOCK_ROWS - SUBLANES:BLOCK_ROWS], 1, 0)
    prev = jnp.concatenate([first_prev, t[0:BLOCK_ROWS - SUBLANES]], axis=0)
    last_next = pltpu.roll(t[0:SUBLANES], SUBLANES - 1, 0)
    nxt = jnp.concatenate([t[SUBLANES:BLOCK_ROWS], last_next], axis=0)
    return cw[0:1, :] * prev + cw[1:2, :] * t + cw[2:3, :] * nxt


def _finish_block(blk, xe_ref, acc, gate, lng, lnb, o_ref):
    y = DEEPNORM_ALPHA * _load_permuted(xe_ref.at[blk]) + gate * acc
    _store_unpermuted(_layer_norm(y, lng, lnb), xe_ref.at[blk])
    for seg in range(SEGS_PER_BLOCK):
        lo = (blk * SEGS_PER_BLOCK + seg) * ROW_TILE
        base = seg * SEG_ROWS
        for c in range(N_SLABS):
            o_ref[lo:lo + ROW_TILE, c * LANES:(c + 1) * LANES] = xe_ref[blk, c, base + HALO:base + SEG_ROWS, :]


def _ffn_kernel(x_ref, xp_ref, xn_ref, mod_ref, wup_ref, cw_ref, wdn_ref, lng_ref, lnb_ref, o_ref,
                xe_ref, h_ref, act_ref, *, tiles_per_seq):
    for blk in range(BLOCKS_PER_STEP):
        _stage_block(blk, tiles_per_seq, x_ref, xp_ref, xn_ref, mod_ref[3:4, :], mod_ref[4:5, :], xe_ref, h_ref)
        for j in range(N_FF_CHUNKS):
            a_cols = slice(j * FF_CHUNK, (j + 1) * FF_CHUNK)
            g_cols = slice(D_FF + j * FF_CHUNK, D_FF + (j + 1) * FF_CHUNK)
            a = _conv3_permuted(_dot(h_ref[blk], wup_ref[:, a_cols]), cw_ref[:, a_cols])
            g = _conv3_permuted(_dot(h_ref[blk], wup_ref[:, g_cols]), cw_ref[:, g_cols])
            act_ref[blk, :, a_cols] = (g * _sigmoid(g) * a).astype(BF16)
    for blk in range(BLOCKS_PER_STEP):
        acc = _dot(act_ref[blk], wdn_ref[...])
        _finish_block(blk, xe_ref, acc, mod_ref[5:6, :], lng_ref[1:2, :], lnb_ref[1:2, :], o_ref)


def _mixer_kernel(x_ref, xp_ref, xn_ref, mod_ref, win_ref, cw_ref, wout_ref, lng_ref, lnb_ref, o_ref,
                  xe_ref, h_ref, act_ref, *, tiles_per_seq):
    for blk in range(BLOCKS_PER_STEP):
        _stage_block(blk, tiles_per_seq, x_ref, xp_ref, xn_ref, mod_ref[0:1, :], mod_ref[1:2, :], xe_ref, h_ref)
        for j in range(N_MIX_CHUNKS):
            cols = slice(j * FF_CHUNK, (j + 1) * FF_CHUNK)
            bg, cg, xin = (_dot(h_ref[blk], win_ref[:, k * D_MODEL + j * FF_CHUNK:k * D_MODEL + (j + 1) * FF_CHUNK])
                           for k in range(3))
            y = bg * _conv3_permuted(cg * xin, cw_ref[:, cols])
            act_ref[blk, :, cols] = y.astype(BF16)
    for blk in range(BLOCKS_PER_STEP):
        acc = _dot(act_ref[blk], wout_ref[...])
        _finish_block(blk, xe_ref, acc, mod_ref[2:3, :], lng_ref[0:1, :], lnb_ref[0:1, :], o_ref)


def _gated_conv_block(kernel_fn, name, x, mod4, mod_row, w1, cw, w2, ln_g, ln_b, layer, w_layer):
    b, s, _ = x.shape
    hidden = w2.shape[1]
    n_rows = b * s
    step_rows = CONV_TILES_PER_STEP * ROW_TILE
    assert s % ROW_TILE == 0 and (s % step_rows == 0 or step_rows % s == 0)
    halo_blocks_per_step = step_rows // HALO
    last_halo_block = n_rows // HALO - 1
    ln_spec = pl.BlockSpec((None, 2, D_MODEL), lambda i: (layer, 0, 0))
    out = pl.pallas_call(
        functools.partial(kernel_fn, tiles_per_seq=s // ROW_TILE),
        grid=(n_rows // step_rows,),
        in_specs=[
            pl.BlockSpec((step_rows, D_MODEL), lambda i: (i, 0)),
            pl.BlockSpec((HALO, D_MODEL), lambda i: (jnp.maximum(i * halo_blocks_per_step - 1, 0), 0)),
            pl.BlockSpec((HALO, D_MODEL),
                         lambda i: (jnp.minimum((i + 1) * halo_blocks_per_step, last_halo_block), 0)),
            pl.BlockSpec((None, None, 6, D_MODEL), lambda i: (layer, mod_row(i * step_rows // s), 0, 0)),
            _resident((None,) + w1.shape[1:], lambda i: (w_layer, 0, 0)),
            _resident((None,) + cw.shape[1:], lambda i: (w_layer, 0, 0)),
            _resident((None,) + w2.shape[1:], lambda i: (w_layer, 0, 0)),
            ln_spec, ln_spec,
        ],
        out_specs=pl.BlockSpec((step_rows, D_MODEL), lambda i: (i, 0)),
        out_shape=jax.ShapeDtypeStruct((n_rows, D_MODEL), F32),
        scratch_shapes=[
            pltpu.VMEM((BLOCKS_PER_STEP, N_SLABS, BLOCK_ROWS, LANES), F32),
            pltpu.VMEM((BLOCKS_PER_STEP, BLOCK_ROWS, D_MODEL), BF16),
            pltpu.VMEM((BLOCKS_PER_STEP, BLOCK_ROWS, hidden), BF16),
        ],
        compiler_params=pltpu.CompilerParams(dimension_semantics=("arbitrary",),
                                             vmem_limit_bytes=CONV_VMEM_LIMIT_BYTES),
        name=name,
    )(*([x.reshape(n_rows, D_MODEL)] * 3), mod4, w1, cw, w2, ln_g, ln_b)
    return out.reshape(b, s, D_MODEL)


def _dft_tables(n, scale):
    jk = np.outer(np.arange(n), np.arange(n)) % n
    ang = 2.0 * np.pi * jk / n
    return (np.cos(ang) * scale).astype(np.float32), (np.sin(ang) * scale).astype(np.float32)


def _channel_dft_blocks():
    c, s = _dft_tables(FOURIER_GROUP, 1.0)
    eye = np.eye(MXU_WIDTH // FOURIER_GROUP, dtype=np.float32)
    return np.kron(eye, c), np.kron(eye, s)


def _rope_tables(n):
    half = HEAD_DIM // 2
    inv = 1.0 / (ROPE_THETA ** (np.arange(0, half, 2, dtype=np.float64) / half))
    pos = np.arange(n)
    d = np.arange(LANES) % HEAD_DIM
    coord = np.where(d < half, (pos // GRID_W)[:, None], (pos % GRID_W)[:, None])
    ang = coord * inv[d % (half // 2)][None, :]
    first = (d % half) < (half // 2)
    cos = np.cos(ang)
    sa = np.where(first[None, :], -np.sin(ang), 0.0)
    sb = np.where(first[None, :], 0.0, np.sin(ang))
    return tuple(jnp.asarray(t.astype(np.float32)) for t in (cos, sa, sb))


def _head_sum_matrix():
    return np.kron(np.eye(QK_WIDTH // HEAD_DIM, dtype=np.float32), np.ones((HEAD_DIM, HEAD_DIM), np.float32))


def kernel(x_prompt, x_sample, cache_k, cache_v, c, c_ctx, w_ada, b_ada, ln_g, ln_b, w_in_a, q_norm_g,
           k_norm_g, w_out_a, w_in_c, conv_c, w_out_c, w_up, conv_f, w_down):
    n_prompt, s_prompt, _ = x_prompt.shape
    n_sample, s_sample, _ = x_sample.shape
    past = cache_k.shape[2]

    cond = jnp.concatenate(
        [c_ctx[None, :], c, jnp.zeros((N_COND_ROWS - 1 - n_sample, D_MODEL), F32)], axis=0)
    mod4 = _modulation(cond, w_ada, b_ada).reshape(DEPTH, N_COND_ROWS, 6, D_MODEL)

    chan_c, chan_s = _channel_dft_blocks()
    w_cat = _fold_in_proj(w_in_a, jnp.asarray(chan_c), jnp.asarray(chan_s))
    head_sum = jnp.asarray(_head_sum_matrix()).astype(BF16)
    gains = jnp.concatenate([jnp.tile(q_norm_g[0], N_HEADS), jnp.tile(k_norm_g[0], N_KV_HEADS)])[None, :]

    def run_stream(x, mod_row, cache):
        s = x.shape[1]
        scale = (FOURIER_GROUP * s) ** -0.5
        dft_c, dft_s = (jnp.asarray(t).astype(BF16) for t in _dft_tables(s, scale))
        rope_tabs = _rope_tables(s) if cache is not None else None
        x, *kv = _mixer0(x, mod4, mod_row, w_cat, head_sum, gains, rope_tabs, cache, dft_c, dft_s, w_out_a,
                         ln_g, ln_b)
        x = _gated_conv_block(_ffn_kernel, "conv_ffn", x, mod4, mod_row, w_up, conv_f, w_down,
                              ln_g, ln_b, 0, 0)
        x = _gated_conv_block(_mixer_kernel, "conv_mixer", x, mod4, mod_row, w_in_c, conv_c, w_out_c,
                              ln_g, ln_b, 1, 0)
        x = _gated_conv_block(_ffn_kernel, "conv_ffn", x, mod4, mod_row, w_up, conv_f, w_down,
                              ln_g, ln_b, 1, 1)
        return x, kv

    y_prompt, (k_new, v_new) = run_stream(x_prompt, lambda bi: 0, None)
    cache = (cache_k[:, 0].reshape(n_sample, past, KV_WIDTH),
             cache_v[:, 0].reshape(n_sample, past, KV_WIDTH).transpose(0, 2, 1))
    y_sample, _ = run_stream(x_sample, lambda bi: bi + 1, cache)

    new_shape = (n_prompt, 1, s_prompt, N_KV_HEADS, HEAD_DIM)
    return y_prompt, y_sample, k_new.reshape(new_shape), v_new.reshape(new_shape)
```

```python
import functools

import numpy as np
import jax
import jax.numpy as jnp
from jax import lax
from jax.experimental import pallas as pl
from jax.experimental.pallas import tpu as pltpu

D_MODEL = 1024
DEPTH = 2
GRID_W = 64
HEAD_DIM = 64
N_HEADS = 8
N_KV_HEADS = 2
Q_PER_KV = N_HEADS // N_KV_HEADS
ATTN_WIDTH = N_HEADS * HEAD_DIM
KV_WIDTH = N_KV_HEADS * HEAD_DIM
QK_WIDTH = ATTN_WIDTH + KV_WIDTH
FOURIER_GROUP = 64
FOURIER_WIDTH = 512
D_FF = 2816
ROPE_THETA = 10000.0
EPS = 1e-6
DEEPNORM_ALPHA = (2 * DEPTH) ** 0.25
LOG2_E = 1.4426950408889634

F32 = jnp.float32
BF16 = jnp.bfloat16

SUBLANES = 8
BF16_SUBLANES = 16
LANES = 128
MXU_WIDTH = 256

ROW_TILE = 256
ATTN_SUB_TILE = 128
KEY_CHUNK = MXU_WIDTH
HALO = SUBLANES
SEG_ROWS = HALO + ROW_TILE
SEGS_PER_BLOCK = 2
BLOCK_ROWS = SEGS_PER_BLOCK * SEG_ROWS
PERM_STRIDE = BLOCK_ROWS // SUBLANES
BLOCKS_PER_STEP = 1
N_SLABS = D_MODEL // LANES
TILES_PER_STEP = 2
CONV_TILES_PER_STEP = BLOCKS_PER_STEP * SEGS_PER_BLOCK
FF_CHUNK = MXU_WIDTH
N_FF_CHUNKS = D_FF // FF_CHUNK
N_MIX_CHUNKS = D_MODEL // FF_CHUNK
PROJ_WIDTH = 2 * FOURIER_WIDTH + QK_WIDTH + KV_WIDTH
N_COND_ROWS = 16
VMEM_LIMIT_BYTES = 48 * 1024 * 1024
CONV_VMEM_LIMIT_BYTES = 56 * 1024 * 1024


def _dot(a, b):
    return lax.dot_general(a, b, (((a.ndim - 1,), (0,)), ((), ())), preferred_element_type=F32)


def _split_bf16(a):
    hi = a.astype(BF16)
    lo = (a - hi.astype(F32)).astype(BF16)
    return hi, lo


def _sigmoid(x):
    return 1.0 / (1.0 + jnp.exp(-x))


def _layer_norm(y, g, b):
    mu = jnp.mean(y, axis=-1, keepdims=True)
    yc = y - mu
    var = jnp.mean(yc * yc, axis=-1, keepdims=True)
    return yc * lax.rsqrt(var + EPS) * g + b


def _params(n_axes):
    return pltpu.CompilerParams(dimension_semantics=("arbitrary",) * n_axes,
                                vmem_limit_bytes=VMEM_LIMIT_BYTES)


def _resident(block, index_map):
    return pl.BlockSpec(block, index_map, pipeline_mode=pl.Buffered(1))


def _mod_kernel(cond_ref, w_ref, b_ref, o_ref):
    c = cond_ref[...]
    s = (c * _sigmoid(c)).astype(BF16)
    o_ref[...] = _dot(s, w_ref[...].astype(BF16)) + b_ref[...]


def _modulation(cond, w_ada, b_ada):
    tn = 1536
    n_out = 6 * D_MODEL
    return pl.pallas_call(
        _mod_kernel,
        grid=(DEPTH, n_out // tn),
        in_specs=[
            pl.BlockSpec((N_COND_ROWS, D_MODEL), lambda l, n: (0, 0)),
            pl.BlockSpec((None, D_MODEL, tn), lambda l, n: (l, 0, n)),
            pl.BlockSpec((None, 1, tn), lambda l, n: (l, 0, n)),
        ],
        out_specs=pl.BlockSpec((None, N_COND_ROWS, tn), lambda l, n: (l, 0, n)),
        out_shape=jax.ShapeDtypeStruct((DEPTH, N_COND_ROWS, n_out), F32),
        compiler_params=_params(2),
        name="modulation",
    )(cond, w_ada, b_ada.reshape(DEPTH, 1, n_out))


def _fold_kernel(w_ref, c_ref, s_ref, o_ref):
    m_c, m_s = _split_bf16(c_ref[...]), _split_bf16(s_ref[...])
    for j in range(FOURIER_WIDTH // MXU_WIDTH):
        cols = slice(j * MXU_WIDTH, (j + 1) * MXU_WIDTH)
        w_hi, w_lo = _split_bf16(w_ref[:, cols])
        for half, (m_hi, m_lo) in enumerate((m_c, m_s)):
            out_cols = slice(half * FOURIER_WIDTH + j * MXU_WIDTH, half * FOURIER_WIDTH + (j + 1) * MXU_WIDTH)
            o_ref[:, out_cols] = (_dot(w_hi, m_hi) + _dot(w_hi, m_lo) + _dot(w_lo, m_hi)).astype(BF16)
    o_ref[:, 2 * FOURIER_WIDTH:] = w_ref[:, FOURIER_WIDTH:].astype(BF16)


def _fold_in_proj(w_in_a, dft_c, dft_s):
    return pl.pallas_call(
        _fold_kernel,
        grid=(1,),
        in_specs=[
            pl.BlockSpec((None,) + w_in_a.shape[1:], lambda j: (0, 0, 0)),
            pl.BlockSpec((MXU_WIDTH, MXU_WIDTH), lambda j: (0, 0)),
            pl.BlockSpec((MXU_WIDTH, MXU_WIDTH), lambda j: (0, 0)),
        ],
        out_specs=pl.BlockSpec((D_MODEL, PROJ_WIDTH), lambda j: (0, 0)),
        out_shape=jax.ShapeDtypeStruct((D_MODEL, PROJ_WIDTH), BF16),
        compiler_params=_params(1),
        name="fold_in_proj",
    )(w_in_a, dft_c, dft_s)


def _step_blocking(b, s):
    rows = min(s, TILES_PER_STEP * ROW_TILE)
    batches = TILES_PER_STEP * ROW_TILE // rows
    assert s % rows == 0 and b % batches == 0 and rows % ROW_TILE == 0
    return batches, rows, [(bu, ru) for bu in range(batches) for ru in range(0, rows, ROW_TILE)]


def _in_proj_kernel(*refs, rope, tiles):
    if rope:
        (x_ref, mod_ref, w_ref, hs_ref, g_ref, cos_ref, sa_ref, sb_ref,
         fcs_ref, q_ref, k_ref, vt_ref) = refs
    else:
        x_ref, mod_ref, w_ref, hs_ref, g_ref, fcs_ref, q_ref, k_ref, vt_ref, v_ref = refs
    shift = mod_ref[0:1, :]
    scale = mod_ref[1:2, :]
    for bu, ru in tiles:
        rows = slice(ru, ru + ROW_TILE)
        h = (x_ref[bu, rows, :] * (1.0 + scale) + shift).astype(BF16)
        qk_cols = slice(2 * FOURIER_WIDTH, 2 * FOURIER_WIDTH + QK_WIDTH)
        qk = _dot(h, w_ref[:, qk_cols])
        ssq = _dot((qk * qk).astype(BF16), hs_ref[...])
        fcs_ref[bu, rows, :] = _dot(h, w_ref[:, :2 * FOURIER_WIDTH]).astype(BF16)
        v = _dot(h, w_ref[:, 2 * FOURIER_WIDTH + QK_WIDTH:])
        y = qk * lax.rsqrt(ssq * (1.0 / HEAD_DIM) + EPS) * g_ref[...]
        if rope:
            pos = pl.ds(pl.multiple_of(pl.program_id(1) * x_ref.shape[1] + ru, ROW_TILE), ROW_TILE)
            parts = []
            for t in range(QK_WIDTH // LANES):
                yt = y[:, t * LANES:(t + 1) * LANES]
                parts.append(yt * cos_ref[pos, :]
                             + pltpu.roll(yt, LANES - 16, 1) * sa_ref[pos, :]
                             + pltpu.roll(yt, 16, 1) * sb_ref[pos, :])
            y = jnp.concatenate(parts, axis=1)
        q_ref[bu, rows, :] = (y[:, :ATTN_WIDTH] * (HEAD_DIM ** -0.5 * LOG2_E)).astype(q_ref.dtype)
        k_ref[bu, rows, :] = y[:, ATTN_WIDTH:].astype(k_ref.dtype)
        vt_ref[bu, :, rows] = v.T.astype(vt_ref.dtype)
        if not rope:
            v_ref[bu, rows, :] = v


def _in_proj(x, mod4, mod_row, w_cat, head_sum, gains, rope_tabs):
    b, s, _ = x.shape
    rope = rope_tabs is not None
    bb, tr, tiles = _step_blocking(b, s)
    row_spec = lambda w: pl.BlockSpec((bb, tr, w), lambda bi, i: (bi, i, 0))
    out_specs = [row_spec(2 * FOURIER_WIDTH), row_spec(ATTN_WIDTH), row_spec(KV_WIDTH),
                 pl.BlockSpec((bb, KV_WIDTH, tr), lambda bi, i: (bi, 0, i))]
    out_shape = [
        jax.ShapeDtypeStruct((b, s, 2 * FOURIER_WIDTH), BF16),
        jax.ShapeDtypeStruct((b, s, ATTN_WIDTH), BF16),
        jax.ShapeDtypeStruct((b, s, KV_WIDTH), BF16 if rope else F32),
        jax.ShapeDtypeStruct((b, KV_WIDTH, s), BF16),
    ]
    if not rope:
        out_specs.append(row_spec(KV_WIDTH))
        out_shape.append(jax.ShapeDtypeStruct((b, s, KV_WIDTH), F32))
    in_specs = [
        row_spec(D_MODEL),
        pl.BlockSpec((None, None, 6, D_MODEL), lambda bi, i: (0, mod_row(bi * bb), 0, 0)),
        _resident((D_MODEL, PROJ_WIDTH), lambda bi, i: (0, 0)),
        _resident((QK_WIDTH, QK_WIDTH), lambda bi, i: (0, 0)),
        _resident((1, QK_WIDTH), lambda bi, i: (0, 0)),
    ]
    args = [x, mod4, w_cat, head_sum, gains]
    if rope:
        in_specs += [_resident((s, LANES), lambda bi, i: (0, 0))] * 3
        args += list(rope_tabs)
    return pl.pallas_call(
        functools.partial(_in_proj_kernel, rope=rope, tiles=tiles),
        grid=(b // bb, s // tr),
        in_specs=in_specs,
        out_specs=out_specs,
        out_shape=out_shape,
        compiler_params=_params(2),
        name="in_proj_rope" if rope else "in_proj",
    )(*args)


def _attention_scores(q_ref, bu, row0, key_sets, sub_tile):
    nt = (((1,), (1,)), ((), ()))
    q = q_ref[bu, row0:row0 + sub_tile, :]
    items = []
    for j in range(N_KV_HEADS):
        kv = slice(j * HEAD_DIM, (j + 1) * HEAD_DIM)
        qs = jnp.concatenate(
            [q[:, h * HEAD_DIM:(h + 1) * HEAD_DIM] for h in range(j * Q_PER_KV, (j + 1) * Q_PER_KV)], axis=0)
        items.append([lax.dot_general(keys_ref[bu, :, kv].astype(BF16), qs, nt, preferred_element_type=F32)
                      for keys_ref, _ in key_sets])
    return items


def _attention_finish(items, bu, key_sets, attn_ref, arow0, sub_tile):
    for j, sts in enumerate(items):
        parts = []
        for st_all, (_, vts_ref) in zip(sts, key_sets):
            n_keys = st_all.shape[0]
            chunk = min(KEY_CHUNK, n_keys)
            ones_rows = jnp.ones((BF16_SUBLANES, chunk), BF16)
            for c in range(0, n_keys, chunk):
                st = st_all[c:c + chunk]
                m = jnp.max(st, axis=0, keepdims=True)
                vt_ext = jnp.concatenate(
                    [vts_ref[bu, j * HEAD_DIM:(j + 1) * HEAD_DIM, c:c + chunk].astype(BF16), ones_rows], axis=0)
                parts.append((m, _dot(vt_ext, jnp.exp2(st - m).astype(BF16))))
        m_all = functools.reduce(jnp.maximum, [m for m, _ in parts])
        ot = (functools.reduce(jnp.add, [jnp.exp2(m - m_all) * o for m, o in parts])
              if len(parts) > 1 else parts[0][1])
        o = ot[0:HEAD_DIM] / ot[HEAD_DIM:HEAD_DIM + 1]
        for g in range(0, Q_PER_KV, 2):
            pair = jnp.concatenate([o[:, g * sub_tile:(g + 1) * sub_tile],
                                    o[:, (g + 1) * sub_tile:(g + 2) * sub_tile]], axis=0).T
            lane0 = (j * Q_PER_KV + g) * HEAD_DIM
            attn_ref[arow0:arow0 + sub_tile, lane0:lane0 + LANES] = pair.astype(attn_ref.dtype)


def _attn_out_kernel(*refs, has_cache, tiles, sub_tile):
    if has_cache:
        (q_ref, k_ref, vt_ref, ck_ref, cvt_ref, x_ref, fcs_ref, dc_ref, ds_ref, w_ref, mod_ref, lng_ref, lnb_ref,
         o_ref, attn_ref) = refs
        key_sets = [(k_ref, vt_ref), (ck_ref, cvt_ref)]
    else:
        (q_ref, k_ref, vt_ref, x_ref, fcs_ref, dc_ref, ds_ref, w_ref, mod_ref, lng_ref, lnb_ref,
         o_ref, attn_ref) = refs
        key_sets = [(k_ref, vt_ref)]
    gate = mod_ref[2:3, :]
    subs_per_tile = ROW_TILE // sub_tile
    subs = [(n, bu, ru, t) for n, (bu, ru) in enumerate(tiles) for t in range(subs_per_tile)]
    scores = lambda n, bu, ru, t: _attention_scores(q_ref, bu, ru + t * sub_tile, key_sets, sub_tile)

    four = None
    items_next = scores(*subs[0])
    for i, (n, bu, ru, t) in enumerate(subs):
        items = items_next
        if i + 1 < len(subs):
            items_next = scores(*subs[i + 1])
        if t == 0:
            pos = pl.ds(pl.multiple_of(pl.program_id(1) * x_ref.shape[1] + ru, ROW_TILE), ROW_TILE)
            four = (_dot(dc_ref[pos, :], fcs_ref[bu, :, :FOURIER_WIDTH])
                    - _dot(ds_ref[pos, :], fcs_ref[bu, :, FOURIER_WIDTH:]))
        _attention_finish(items, bu, key_sets, attn_ref, n * ROW_TILE + t * sub_tile, sub_tile)
        if t == subs_per_tile - 1:
            rows = slice(ru, ru + ROW_TILE)
            out = (_dot(four.astype(BF16), w_ref[:FOURIER_WIDTH, :])
                   + _dot(attn_ref[n * ROW_TILE:(n + 1) * ROW_TILE, :], w_ref[FOURIER_WIDTH:, :]))
            y = DEEPNORM_ALPHA * x_ref[bu, rows, :] + gate * out
            o_ref[bu, rows, :] = _layer_norm(y, lng_ref[0:1, :], lnb_ref[0:1, :])


def _attn_out(x, fcs, q, k, vt, cache, dft_c, dft_s, w_out, mod4, mod_row, ln_g, ln_b, layer):
    b, s, _ = x.shape
    has_cache = cache is not None
    bb, tr, tiles = _step_blocking(b, s)
    row_spec = lambda w: pl.BlockSpec((bb, tr, w), lambda bi, i: (bi, i, 0))
    seq_spec = lambda n, w: pl.BlockSpec((bb, n, w), lambda bi, i: (bi, 0, 0))
    seq_t_spec = lambda n: pl.BlockSpec((bb, KV_WIDTH, n), lambda bi, i: (bi, 0, 0))
    ln_spec = pl.BlockSpec((None, 2, D_MODEL), lambda bi, i: (layer, 0, 0))
    in_specs = [row_spec(ATTN_WIDTH), seq_spec(s, KV_WIDTH), seq_t_spec(s)]
    args = [q, k, vt]
    if has_cache:
        past = cache[0].shape[1]
        in_specs += [seq_spec(past, KV_WIDTH), seq_t_spec(past)]
        args += list(cache)
    in_specs += [
        row_spec(D_MODEL),
        seq_spec(s, 2 * FOURIER_WIDTH),
        _resident((s, s), lambda bi, i: (0, 0)),
        _resident((s, s), lambda bi, i: (0, 0)),
        _resident((2 * FOURIER_WIDTH, D_MODEL), lambda bi, i: (0, 0)),
        pl.BlockSpec((None, None, 6, D_MODEL), lambda bi, i: (layer, mod_row(bi * bb), 0, 0)),
        ln_spec, ln_spec,
    ]
    args += [x, fcs, dft_c, dft_s, w_out, mod4, ln_g, ln_b]
    return pl.pallas_call(
        functools.partial(_attn_out_kernel, has_cache=has_cache, tiles=tiles, sub_tile=ATTN_SUB_TILE),
        grid=(b // bb, s // tr),
        in_specs=in_specs,
        out_specs=row_spec(D_MODEL),
        out_shape=jax.ShapeDtypeStruct((b, s, D_MODEL), F32),
        scratch_shapes=[pltpu.VMEM((bb * tr, ATTN_WIDTH), BF16)],
        compiler_params=_params(2),
        name="attn_out_cached" if has_cache else "attn_out",
    )(*args)


def _load_permuted(slab_ref):
    slabs = [jnp.concatenate([slab_ref[c, pl.ds(v, SUBLANES, stride=PERM_STRIDE), :] for v in range(PERM_STRIDE)],
                             axis=0) for c in range(N_SLABS)]
    return jnp.concatenate(slabs, axis=1)


def _store_unpermuted(y, slab_ref):
    for c in range(N_SLABS):
        for v in range(PERM_STRIDE):
            slab_ref[c, pl.ds(v, SUBLANES, stride=PERM_STRIDE), :] = (
                y[v * SUBLANES:(v + 1) * SUBLANES, c * LANES:(c + 1) * LANES])


def _perm_row(q):
    return SUBLANES * (q % PERM_STRIDE) + q // PERM_STRIDE


def _stage_block(blk, tiles_per_seq, x_ref, xp_ref, xn_ref, shift, scale, xe_ref, h_ref):
    sublane = lax.broadcasted_iota(jnp.int32, (SUBLANES, 1), 0)
    rows = lax.broadcasted_iota(jnp.int32, (BLOCK_ROWS, 1), 0)
    keep = jnp.ones((BLOCK_ROWS, 1), F32)
    tiles = [blk * SEGS_PER_BLOCK + seg for seg in range(SEGS_PER_BLOCK)]

    def in_sequence(first_tile, second_tile):
        if tiles_per_seq == 1:
            return 0.0
        tile = pl.program_id(0) * CONV_TILES_PER_STEP + second_tile
        return jnp.where(lax.rem(tile, tiles_per_seq) != 0, 1.0, 0.0)

    for seg, u in enumerate(tiles):
        lo = u * ROW_TILE
        base = seg * SEG_ROWS
        w = tiles[seg - 1]
        after_lo = (w + 1) * ROW_TILE
        for c in range(N_SLABS):
            cols = slice(c * LANES, (c + 1) * LANES)
            before = xp_ref[:, cols] if u == 0 else x_ref[lo - HALO:lo, cols]
            after = xn_ref[0:1, cols] if w == CONV_TILES_PER_STEP - 1 else x_ref[after_lo:after_lo + 1, cols]
            xe_ref[blk, c, base:base + HALO, :] = jnp.where(sublane == 0, after, before)
            xe_ref[blk, c, base + HALO:base + SEG_ROWS, :] = x_ref[lo:lo + ROW_TILE, cols]
        for slot in range(HALO):
            value = in_sequence(u - 1, u) if slot == HALO - 1 else in_sequence(w, w + 1) if slot == 0 else 0.0
            keep = jnp.where(rows == _perm_row(base + slot), value, keep)
    h = (_load_permuted(xe_ref.at[blk]) * (1.0 + scale) + shift) * keep
    h_ref[blk] = h.astype(BF16)


def _conv3_permuted(t, cw):
    first_prev = pltpu.roll(t[BLOCK_ROWS - SUBLANES:BLOCK_ROWS], 1, 0)
    prev = jnp.concatenate([first_prev, t[0:BLOCK_ROWS - SUBLANES]], axis=0)
    last_next = pltpu.roll(t[0:SUBLANES], SUBLANES - 1, 0)
    nxt = jnp.concatenate([t[SUBLANES:BLOCK_ROWS], last_next], axis=0)
    return cw[0:1, :] * prev + cw[1:2, :] * t + cw[2:3, :] * nxt


def _finish_block(blk, xe_ref, acc, gate, lng, lnb, o_ref):
    y = DEEPNORM_ALPHA * _load_permuted(xe_ref.at[blk]) + gate * acc
    _store_unpermuted(_layer_norm(y, lng, lnb), xe_ref.at[blk])
    for seg in range(SEGS_PER_BLOCK):
        lo = (blk * SEGS_PER_BLOCK + seg) * ROW_TILE
        base = seg * SEG_ROWS
        for c in range(N_SLABS):
            o_ref[lo:lo + ROW_TILE, c * LANES:(c + 1) * LANES] = xe_ref[blk, c, base + HALO:base + SEG_ROWS, :]


def _ffn_kernel(x_ref, xp_ref, xn_ref, mod_ref, wup_ref, cw_ref, wdn_ref, lng_ref, lnb_ref, o_ref,
                xe_ref, h_ref, act_ref, *, tiles_per_seq):
    for blk in range(BLOCKS_PER_STEP):
        _stage_block(blk, tiles_per_seq, x_ref, xp_ref, xn_ref, mod_ref[3:4, :], mod_ref[4:5, :], xe_ref, h_ref)
        for j in range(N_FF_CHUNKS):
            a_cols = slice(j * FF_CHUNK, (j + 1) * FF_CHUNK)
            g_cols = slice(D_FF + j * FF_CHUNK, D_FF + (j + 1) * FF_CHUNK)
            a = _conv3_permuted(_dot(h_ref[blk], wup_ref[:, a_cols]), cw_ref[:, a_cols])
            g = _conv3_permuted(_dot(h_ref[blk], wup_ref[:, g_cols]), cw_ref[:, g_cols])
            act_ref[blk, :, a_cols] = (g * _sigmoid(g) * a).astype(BF16)
    for blk in range(BLOCKS_PER_STEP):
        acc = _dot(act_ref[blk], wdn_ref[...])
        _finish_block(blk, xe_ref, acc, mod_ref[5:6, :], lng_ref[1:2, :], lnb_ref[1:2, :], o_ref)


def _mixer_kernel(x_ref, xp_ref, xn_ref, mod_ref, win_ref, cw_ref, wout_ref, lng_ref, lnb_ref, o_ref,
                  xe_ref, h_ref, act_ref, *, tiles_per_seq):
    for blk in range(BLOCKS_PER_STEP):
        _stage_block(blk, tiles_per_seq, x_ref, xp_ref, xn_ref, mod_ref[0:1, :], mod_ref[1:2, :], xe_ref, h_ref)
        for j in range(N_MIX_CHUNKS):
            cols = slice(j * FF_CHUNK, (j + 1) * FF_CHUNK)
            bg, cg, xin = (_dot(h_ref[blk], win_ref[:, k * D_MODEL + j * FF_CHUNK:k * D_MODEL + (j + 1) * FF_CHUNK])
                           for k in range(3))
            y = bg * _conv3_permuted(cg * xin, cw_ref[:, cols])
            act_ref[blk, :, cols] = y.astype(BF16)
    for blk in range(BLOCKS_PER_STEP):
        acc = _dot(act_ref[blk], wout_ref[...])
        _finish_block(blk, xe_ref, acc, mod_ref[2:3, :], lng_ref[0:1, :], lnb_ref[0:1, :], o_ref)


def _gated_conv_block(kernel_fn, name, x, mod4, mod_row, w1, cw, w2, ln_g, ln_b, layer, w_layer):
    b, s, _ = x.shape
    hidden = w2.shape[1]
    n_rows = b * s
    step_rows = CONV_TILES_PER_STEP * ROW_TILE
    assert s % ROW_TILE == 0 and (s % step_rows == 0 or step_rows % s == 0)
    halo_blocks_per_step = step_rows // HALO
    last_halo_block = n_rows // HALO - 1
    ln_spec = pl.BlockSpec((None, 2, D_MODEL), lambda i: (layer, 0, 0))
    out = pl.pallas_call(
        functools.partial(kernel_fn, tiles_per_seq=s // ROW_TILE),
        grid=(n_rows // step_rows,),
        in_specs=[
            pl.BlockSpec((step_rows, D_MODEL), lambda i: (i, 0)),
            pl.BlockSpec((HALO, D_MODEL), lambda i: (jnp.maximum(i * halo_blocks_per_step - 1, 0), 0)),
            pl.BlockSpec((HALO, D_MODEL),
                         lambda i: (jnp.minimum((i + 1) * halo_blocks_per_step, last_halo_block), 0)),
            pl.BlockSpec((None, None, 6, D_MODEL), lambda i: (layer, mod_row(i * step_rows // s), 0, 0)),
            _resident((None,) + w1.shape[1:], lambda i: (w_layer, 0, 0)),
            _resident((None,) + cw.shape[1:], lambda i: (w_layer, 0, 0)),
            _resident((None,) + w2.shape[1:], lambda i: (w_layer, 0, 0)),
            ln_spec, ln_spec,
        ],
        out_specs=pl.BlockSpec((step_rows, D_MODEL), lambda i: (i, 0)),
        out_shape=jax.ShapeDtypeStruct((n_rows, D_MODEL), F32),
        scratch_shapes=[
            pltpu.VMEM((BLOCKS_PER_STEP, N_SLABS, BLOCK_ROWS, LANES), F32),
            pltpu.VMEM((BLOCKS_PER_STEP, BLOCK_ROWS, D_MODEL), BF16),
            pltpu.VMEM((BLOCKS_PER_STEP, BLOCK_ROWS, hidden), BF16),
        ],
        compiler_params=pltpu.CompilerParams(dimension_semantics=("arbitrary",),
                                             vmem_limit_bytes=CONV_VMEM_LIMIT_BYTES),
        name=name,
    )(*([x.reshape(n_rows, D_MODEL)] * 3), mod4, w1, cw, w2, ln_g, ln_b)
    return out.reshape(b, s, D_MODEL)


def _dft_tables(n, scale):
    jk = np.outer(np.arange(n), np.arange(n)) % n
    ang = 2.0 * np.pi * jk / n
    return (np.cos(ang) * scale).astype(np.float32), (np.sin(ang) * scale).astype(np.float32)


def _channel_dft_blocks():
    c, s = _dft_tables(FOURIER_GROUP, 1.0)
    eye = np.eye(MXU_WIDTH // FOURIER_GROUP, dtype=np.float32)
    return np.kron(eye, c), np.kron(eye, s)


def _rope_tables(n):
    half = HEAD_DIM // 2
    inv = 1.0 / (ROPE_THETA ** (np.arange(0, half, 2, dtype=np.float64) / half))
    pos = np.arange(n)
    d = np.arange(LANES) % HEAD_DIM
    coord = np.where(d < half, (pos // GRID_W)[:, None], (pos % GRID_W)[:, None])
    ang = coord * inv[d % (half // 2)][None, :]
    first = (d % half) < (half // 2)
    cos = np.cos(ang)
    sa = np.where(first[None, :], -np.sin(ang), 0.0)
    sb = np.where(first[None, :], 0.0, np.sin(ang))
    return tuple(jnp.asarray(t.astype(np.float32)) for t in (cos, sa, sb))


def _head_sum_matrix():
    return np.kron(np.eye(QK_WIDTH // HEAD_DIM, dtype=np.float32), np.ones((HEAD_DIM, HEAD_DIM), np.float32))


def kernel(x_prompt, x_sample, cache_k, cache_v, c, c_ctx, w_ada, b_ada, ln_g, ln_b, w_in_a, q_norm_g,
           k_norm_g, w_out_a, w_in_c, conv_c, w_out_c, w_up, conv_f, w_down):
    n_prompt, s_prompt, _ = x_prompt.shape
    n_sample, s_sample, _ = x_sample.shape
    past = cache_k.shape[2]

    cond = jnp.concatenate(
        [c_ctx[None, :], c, jnp.zeros((N_COND_ROWS - 1 - n_sample, D_MODEL), F32)], axis=0)
    mod4 = _modulation(cond, w_ada, b_ada).reshape(DEPTH, N_COND_ROWS, 6, D_MODEL)

    chan_c, chan_s = _channel_dft_blocks()
    w_cat = _fold_in_proj(w_in_a, jnp.asarray(chan_c), jnp.asarray(chan_s))
    head_sum = jnp.asarray(_head_sum_matrix()).astype(BF16)
    gains = jnp.concatenate([jnp.tile(q_norm_g[0], N_HEADS), jnp.tile(k_norm_g[0], N_KV_HEADS)])[None, :]
    w_out_bf = w_out_a[0].astype(BF16)

    def run_stream(x, mod_row, cache):
        s = x.shape[1]
        scale = (FOURIER_GROUP * s) ** -0.5
        dft_c, dft_s = (jnp.asarray(t).astype(BF16) for t in _dft_tables(s, scale))
        rope_tabs = _rope_tables(s) if cache is not None else None
        fcs, q, k, vt, *v = _in_proj(x, mod4, mod_row, w_cat, head_sum, gains, rope_tabs)
        x = _attn_out(x, fcs, q, k, vt, cache, dft_c, dft_s, w_out_bf, mod4, mod_row, ln_g, ln_b, 0)
        x = _gated_conv_block(_ffn_kernel, "conv_ffn", x, mod4, mod_row, w_up, conv_f, w_down, ln_g, ln_b, 0, 0)
        x = _gated_conv_block(_mixer_kernel, "conv_mixer", x, mod4, mod_row, w_in_c, conv_c, w_out_c,
                              ln_g, ln_b, 1, 0)
        x = _gated_conv_block(_ffn_kernel, "conv_ffn", x, mod4, mod_row, w_up, conv_f, w_down, ln_g, ln_b, 1, 1)
        return x, k, v

    y_prompt, k_new, (v_new,) = run_stream(x_prompt, lambda bi: 0, None)
    cache = (cache_k[:, 0].reshape(n_sample, past, KV_WIDTH),
             cache_v[:, 0].reshape(n_sample, past, KV_WIDTH).transpose(0, 2, 1))
    y_sample, _, _ = run_stream(x_sample, lambda bi: bi + 1, cache)

    new_shape = (n_prompt, 1, s_prompt, N_KV_HEADS, HEAD_DIM)
    return y_prompt, y_sample, k_new.reshape(new_shape), v_new.reshape(new_shape)
```

```python
import functools

import numpy as np
import jax
import jax.numpy as jnp
from jax import lax
from jax.experimental import pallas as pl
from jax.experimental.pallas import tpu as pltpu

D_MODEL = 1024
DEPTH = 2
GRID_W = 64
HEAD_DIM = 64
N_HEADS = 8
N_KV_HEADS = 2
Q_PER_KV = N_HEADS // N_KV_HEADS
ATTN_WIDTH = N_HEADS * HEAD_DIM
KV_WIDTH = N_KV_HEADS * HEAD_DIM
QK_WIDTH = ATTN_WIDTH + KV_WIDTH
FOURIER_GROUP = 64
FOURIER_WIDTH = 512
D_FF = 2816
ROPE_THETA = 10000.0
EPS = 1e-6
DEEPNORM_ALPHA = (2 * DEPTH) ** 0.25
LOG2_E = 1.4426950408889634

F32 = jnp.float32
BF16 = jnp.bfloat16

SUBLANES = 8
BF16_SUBLANES = 16
LANES = 128
MXU_WIDTH = 256

ROW_TILE = 256
ATTN_SUB_TILE = 128
KEY_CHUNK = MXU_WIDTH
HALO = SUBLANES
SEG_ROWS = HALO + ROW_TILE
SEGS_PER_BLOCK = 2
BLOCK_ROWS = SEGS_PER_BLOCK * SEG_ROWS
PERM_STRIDE = BLOCK_ROWS // SUBLANES
BLOCKS_PER_STEP = 1
N_SLABS = D_MODEL // LANES
TILES_PER_STEP = 4
CONV_TILES_PER_STEP = BLOCKS_PER_STEP * SEGS_PER_BLOCK
FF_CHUNK = MXU_WIDTH
N_FF_CHUNKS = D_FF // FF_CHUNK
N_MIX_CHUNKS = D_MODEL // FF_CHUNK
PROJ_WIDTH = 2 * FOURIER_WIDTH + QK_WIDTH + KV_WIDTH
N_COND_ROWS = 16
VMEM_LIMIT_BYTES = 48 * 1024 * 1024
CONV_VMEM_LIMIT_BYTES = 56 * 1024 * 1024


def _dot(a, b):
    return lax.dot_general(a, b, (((a.ndim - 1,), (0,)), ((), ())), preferred_element_type=F32)


def _split_bf16(a):
    hi = a.astype(BF16)
    lo = (a - hi.astype(F32)).astype(BF16)
    return hi, lo


def _sigmoid(x):
    return 1.0 / (1.0 + jnp.exp(-x))


def _layer_norm(y, g, b):
    mu = jnp.mean(y, axis=-1, keepdims=True)
    yc = y - mu
    var = jnp.mean(yc * yc, axis=-1, keepdims=True)
    return yc * lax.rsqrt(var + EPS) * g + b


def _params(n_axes):
    return pltpu.CompilerParams(dimension_semantics=("arbitrary",) * n_axes,
                                vmem_limit_bytes=VMEM_LIMIT_BYTES)


def _resident(block, index_map):
    return pl.BlockSpec(block, index_map, pipeline_mode=pl.Buffered(1))


def _mod_kernel(cond_ref, w_ref, b_ref, o_ref):
    c = cond_ref[...]
    s = (c * _sigmoid(c)).astype(BF16)
    o_ref[...] = _dot(s, w_ref[...].astype(BF16)) + b_ref[...]


def _modulation(cond, w_ada, b_ada):
    tn = 1536
    n_out = 6 * D_MODEL
    return pl.pallas_call(
        _mod_kernel,
        grid=(DEPTH, n_out // tn),
        in_specs=[
            pl.BlockSpec((N_COND_ROWS, D_MODEL), lambda l, n: (0, 0)),
            pl.BlockSpec((None, D_MODEL, tn), lambda l, n: (l, 0, n)),
            pl.BlockSpec((None, 1, tn), lambda l, n: (l, 0, n)),
        ],
        out_specs=pl.BlockSpec((None, N_COND_ROWS, tn), lambda l, n: (l, 0, n)),
        out_shape=jax.ShapeDtypeStruct((DEPTH, N_COND_ROWS, n_out), F32),
        compiler_params=_params(2),
        name="modulation",
    )(cond, w_ada, b_ada.reshape(DEPTH, 1, n_out))


def _fold_kernel(w_ref, c_ref, s_ref, o_ref):
    m_c, m_s = _split_bf16(c_ref[...]), _split_bf16(s_ref[...])
    for j in range(FOURIER_WIDTH // MXU_WIDTH):
        cols = slice(j * MXU_WIDTH, (j + 1) * MXU_WIDTH)
        w_hi, w_lo = _split_bf16(w_ref[:, cols])
        for half, (m_hi, m_lo) in enumerate((m_c, m_s)):
            out_cols = slice(half * FOURIER_WIDTH + j * MXU_WIDTH, half * FOURIER_WIDTH + (j + 1) * MXU_WIDTH)
            o_ref[:, out_cols] = (_dot(w_hi, m_hi) + _dot(w_hi, m_lo) + _dot(w_lo, m_hi)).astype(BF16)
    o_ref[:, 2 * FOURIER_WIDTH:] = w_ref[:, FOURIER_WIDTH:].astype(BF16)


def _fold_in_proj(w_in_a, dft_c, dft_s):
    return pl.pallas_call(
        _fold_kernel,
        grid=(1,),
        in_specs=[
            pl.BlockSpec((None,) + w_in_a.shape[1:], lambda j: (0, 0, 0)),
            pl.BlockSpec((MXU_WIDTH, MXU_WIDTH), lambda j: (0, 0)),
            pl.BlockSpec((MXU_WIDTH, MXU_WIDTH), lambda j: (0, 0)),
        ],
        out_specs=pl.BlockSpec((D_MODEL, PROJ_WIDTH), lambda j: (0, 0)),
        out_shape=jax.ShapeDtypeStruct((D_MODEL, PROJ_WIDTH), BF16),
        compiler_params=_params(1),
        name="fold_in_proj",
    )(w_in_a, dft_c, dft_s)


def _step_blocking(b, s):
    rows = min(s, TILES_PER_STEP * ROW_TILE)
    batches = TILES_PER_STEP * ROW_TILE // rows
    assert s % rows == 0 and b % batches == 0 and rows % ROW_TILE == 0
    return batches, rows, [(bu, ru) for bu in range(batches) for ru in range(0, rows, ROW_TILE)]


def _in_proj_kernel(*refs, rope, tiles):
    if rope:
        (x_ref, mod_ref, w_ref, hs_ref, g_ref, cos_ref, sa_ref, sb_ref,
         fcs_ref, q_ref, k_ref, vt_ref) = refs
    else:
        x_ref, mod_ref, w_ref, hs_ref, g_ref, fcs_ref, q_ref, k_ref, vt_ref, v_ref = refs
    shift = mod_ref[0:1, :]
    scale = mod_ref[1:2, :]
    for bu, ru in tiles:
        rows = slice(ru, ru + ROW_TILE)
        h = (x_ref[bu, rows, :] * (1.0 + scale) + shift).astype(BF16)
        qk_cols = slice(2 * FOURIER_WIDTH, 2 * FOURIER_WIDTH + QK_WIDTH)
        qk = _dot(h, w_ref[:, qk_cols])
        ssq = _dot((qk * qk).astype(BF16), hs_ref[...])
        fcs_ref[bu, rows, :] = _dot(h, w_ref[:, :2 * FOURIER_WIDTH]).astype(BF16)
        v = _dot(h, w_ref[:, 2 * FOURIER_WIDTH + QK_WIDTH:])
        y = qk * lax.rsqrt(ssq * (1.0 / HEAD_DIM) + EPS) * g_ref[...]
        if rope:
            pos = pl.ds(pl.multiple_of(pl.program_id(1) * x_ref.shape[1] + ru, ROW_TILE), ROW_TILE)
            parts = []
            for t in range(QK_WIDTH // LANES):
                yt = y[:, t * LANES:(t + 1) * LANES]
                parts.append(yt * cos_ref[pos, :]
                             + pltpu.roll(yt, LANES - 16, 1) * sa_ref[pos, :]
                             + pltpu.roll(yt, 16, 1) * sb_ref[pos, :])
            y = jnp.concatenate(parts, axis=1)
        q_ref[bu, rows, :] = (y[:, :ATTN_WIDTH] * (HEAD_DIM ** -0.5 * LOG2_E)).astype(q_ref.dtype)
        k_ref[bu, rows, :] = y[:, ATTN_WIDTH:].astype(k_ref.dtype)
        vt_ref[bu, :, rows] = v.T.astype(vt_ref.dtype)
        if not rope:
            v_ref[bu, rows, :] = v


def _in_proj(x, mod4, mod_row, w_cat, head_sum, gains, rope_tabs):
    b, s, _ = x.shape
    rope = rope_tabs is not None
    bb, tr, tiles = _step_blocking(b, s)
    row_spec = lambda w: pl.BlockSpec((bb, tr, w), lambda bi, i: (bi, i, 0))
    out_specs = [row_spec(2 * FOURIER_WIDTH), row_spec(ATTN_WIDTH), row_spec(KV_WIDTH),
                 pl.BlockSpec((bb, KV_WIDTH, tr), lambda bi, i: (bi, 0, i))]
    out_shape = [
        jax.ShapeDtypeStruct((b, s, 2 * FOURIER_WIDTH), BF16),
        jax.ShapeDtypeStruct((b, s, ATTN_WIDTH), BF16),
        jax.ShapeDtypeStruct((b, s, KV_WIDTH), BF16 if rope else F32),
        jax.ShapeDtypeStruct((b, KV_WIDTH, s), BF16),
    ]
    if not rope:
        out_specs.append(row_spec(KV_WIDTH))
        out_shape.append(jax.ShapeDtypeStruct((b, s, KV_WIDTH), F32))
    in_specs = [
        row_spec(D_MODEL),
        pl.BlockSpec((None, None, 6, D_MODEL), lambda bi, i: (0, mod_row(bi * bb), 0, 0)),
        _resident((D_MODEL, PROJ_WIDTH), lambda bi, i: (0, 0)),
        _resident((QK_WIDTH, QK_WIDTH), lambda bi, i: (0, 0)),
        _resident((1, QK_WIDTH), lambda bi, i: (0, 0)),
    ]
    args = [x, mod4, w_cat, head_sum, gains]
    if rope:
        in_specs += [_resident((s, LANES), lambda bi, i: (0, 0))] * 3
        args += list(rope_tabs)
    return pl.pallas_call(
        functools.partial(_in_proj_kernel, rope=rope, tiles=tiles),
        grid=(b // bb, s // tr),
        in_specs=in_specs,
        out_specs=out_specs,
        out_shape=out_shape,
        compiler_params=_params(2),
        name="in_proj_rope" if rope else "in_proj",
    )(*args)


def _attention_scores(q_ref, bu, row0, key_sets, sub_tile):
    nt = (((1,), (1,)), ((), ()))
    q = q_ref[bu, row0:row0 + sub_tile, :]
    items = []
    for j in range(N_KV_HEADS):
        kv = slice(j * HEAD_DIM, (j + 1) * HEAD_DIM)
        qs = jnp.concatenate(
            [q[:, h * HEAD_DIM:(h + 1) * HEAD_DIM] for h in range(j * Q_PER_KV, (j + 1) * Q_PER_KV)], axis=0)
        items.append([lax.dot_general(keys_ref[bu, :, kv].astype(BF16), qs, nt, preferred_element_type=F32)
                      for keys_ref, _ in key_sets])
    return items


def _attention_finish(items, bu, key_sets, attn_ref, arow0, sub_tile):
    for j, sts in enumerate(items):
        parts = []
        for st_all, (_, vts_ref) in zip(sts, key_sets):
            n_keys = st_all.shape[0]
            chunk = min(KEY_CHUNK, n_keys)
            ones_rows = jnp.ones((BF16_SUBLANES, chunk), BF16)
            for c in range(0, n_keys, chunk):
                st = st_all[c:c + chunk]
                m = jnp.max(st, axis=0, keepdims=True)
                vt_ext = jnp.concatenate(
                    [vts_ref[bu, j * HEAD_DIM:(j + 1) * HEAD_DIM, c:c + chunk].astype(BF16), ones_rows], axis=0)
                parts.append((m, _dot(vt_ext, jnp.exp2(st - m).astype(BF16))))
        m_all = functools.reduce(jnp.maximum, [m for m, _ in parts])
        ot = (functools.reduce(jnp.add, [jnp.exp2(m - m_all) * o for m, o in parts])
              if len(parts) > 1 else parts[0][1])
        o = ot[0:HEAD_DIM] / ot[HEAD_DIM:HEAD_DIM + 1]
        for g in range(0, Q_PER_KV, 2):
            pair = jnp.concatenate([o[:, g * sub_tile:(g + 1) * sub_tile],
                                    o[:, (g + 1) * sub_tile:(g + 2) * sub_tile]], axis=0).T
            lane0 = (j * Q_PER_KV + g) * HEAD_DIM
            attn_ref[arow0:arow0 + sub_tile, lane0:lane0 + LANES] = pair.astype(attn_ref.dtype)


def _attn_out_kernel(*refs, has_cache, tiles, sub_tile):
    if has_cache:
        (q_ref, k_ref, vt_ref, ck_ref, cvt_ref, x_ref, fcs_ref, dc_ref, ds_ref, w_ref, mod_ref, lng_ref, lnb_ref,
         o_ref, attn_ref) = refs
        key_sets = [(k_ref, vt_ref), (ck_ref, cvt_ref)]
    else:
        (q_ref, k_ref, vt_ref, x_ref, fcs_ref, dc_ref, ds_ref, w_ref, mod_ref, lng_ref, lnb_ref,
         o_ref, attn_ref) = refs
        key_sets = [(k_ref, vt_ref)]
    gate = mod_ref[2:3, :]
    subs_per_tile = ROW_TILE // sub_tile
    subs = [(n, bu, ru, t) for n, (bu, ru) in enumerate(tiles) for t in range(subs_per_tile)]
    scores = lambda n, bu, ru, t: _attention_scores(q_ref, bu, ru + t * sub_tile, key_sets, sub_tile)

    four = None
    items_next = scores(*subs[0])
    for i, (n, bu, ru, t) in enumerate(subs):
        items = items_next
        if i + 1 < len(subs):
            items_next = scores(*subs[i + 1])
        if t == 0:
            pos = pl.ds(pl.multiple_of(pl.program_id(1) * x_ref.shape[1] + ru, ROW_TILE), ROW_TILE)
            four = (_dot(dc_ref[pos, :], fcs_ref[bu, :, :FOURIER_WIDTH])
                    - _dot(ds_ref[pos, :], fcs_ref[bu, :, FOURIER_WIDTH:]))
        _attention_finish(items, bu, key_sets, attn_ref, n * ROW_TILE + t * sub_tile, sub_tile)
        if t == subs_per_tile - 1:
            rows = slice(ru, ru + ROW_TILE)
            out = (_dot(four.astype(BF16), w_ref[:FOURIER_WIDTH, :])
                   + _dot(attn_ref[n * ROW_TILE:(n + 1) * ROW_TILE, :], w_ref[FOURIER_WIDTH:, :]))
            y = DEEPNORM_ALPHA * x_ref[bu, rows, :] + gate * out
            o_ref[bu, rows, :] = _layer_norm(y, lng_ref[0:1, :], lnb_ref[0:1, :])


def _attn_out(x, fcs, q, k, vt, cache, dft_c, dft_s, w_out, mod4, mod_row, ln_g, ln_b, layer):
    b, s, _ = x.shape
    has_cache = cache is not None
    bb, tr, tiles = _step_blocking(b, s)
    row_spec = lambda w: pl.BlockSpec((bb, tr, w), lambda bi, i: (bi, i, 0))
    seq_spec = lambda n, w: pl.BlockSpec((bb, n, w), lambda bi, i: (bi, 0, 0))
    seq_t_spec = lambda n: pl.BlockSpec((bb, KV_WIDTH, n), lambda bi, i: (bi, 0, 0))
    ln_spec = pl.BlockSpec((None, 2, D_MODEL), lambda bi, i: (layer, 0, 0))
    in_specs = [row_spec(ATTN_WIDTH), seq_spec(s, KV_WIDTH), seq_t_spec(s)]
    args = [q, k, vt]
    if has_cache:
        past = cache[0].shape[1]
        in_specs += [seq_spec(past, KV_WIDTH), seq_t_spec(past)]
        args += list(cache)
    in_specs += [
        row_spec(D_MODEL),
        seq_spec(s, 2 * FOURIER_WIDTH),
        _resident((s, s), lambda bi, i: (0, 0)),
        _resident((s, s), lambda bi, i: (0, 0)),
        _resident((2 * FOURIER_WIDTH, D_MODEL), lambda bi, i: (0, 0)),
        pl.BlockSpec((None, None, 6, D_MODEL), lambda bi, i: (layer, mod_row(bi * bb), 0, 0)),
        ln_spec, ln_spec,
    ]
    args += [x, fcs, dft_c, dft_s, w_out, mod4, ln_g, ln_b]
    return pl.pallas_call(
        functools.partial(_attn_out_kernel, has_cache=has_cache, tiles=tiles, sub_tile=ATTN_SUB_TILE),
        grid=(b // bb, s // tr),
        in_specs=in_specs,
        out_specs=row_spec(D_MODEL),
        out_shape=jax.ShapeDtypeStruct((b, s, D_MODEL), F32),
        scratch_shapes=[pltpu.VMEM((bb * tr, ATTN_WIDTH), BF16)],
        compiler_params=pltpu.CompilerParams(dimension_semantics=("arbitrary",) * 2,
                                             vmem_limit_bytes=CONV_VMEM_LIMIT_BYTES),
        name="attn_out_cached" if has_cache else "attn_out",
    )(*args)


def _load_permuted(slab_ref):
    slabs = [jnp.concatenate([slab_ref[c, pl.ds(v, SUBLANES, stride=PERM_STRIDE), :] for v in range(PERM_STRIDE)],
                             axis=0) for c in range(N_SLABS)]
    return jnp.concatenate(slabs, axis=1)


def _store_unpermuted(y, slab_ref):
    for c in range(N_SLABS):
        for v in range(PERM_STRIDE):
            slab_ref[c, pl.ds(v, SUBLANES, stride=PERM_STRIDE), :] = (
                y[v * SUBLANES:(v + 1) * SUBLANES, c * LANES:(c + 1) * LANES])


def _perm_row(q):
    return SUBLANES * (q % PERM_STRIDE) + q // PERM_STRIDE


def _stage_block(blk, tiles_per_seq, x_ref, xp_ref, xn_ref, shift, scale, xe_ref, h_ref):
    sublane = lax.broadcasted_iota(jnp.int32, (SUBLANES, 1), 0)
    rows = lax.broadcasted_iota(jnp.int32, (BLOCK_ROWS, 1), 0)
    keep = jnp.ones((BLOCK_ROWS, 1), F32)
    tiles = [blk * SEGS_PER_BLOCK + seg for seg in range(SEGS_PER_BLOCK)]

    def in_sequence(first_tile, second_tile):
        if tiles_per_seq == 1:
            return 0.0
        tile = pl.program_id(0) * CONV_TILES_PER_STEP + second_tile
        return jnp.where(lax.rem(tile, tiles_per_seq) != 0, 1.0, 0.0)

    for seg, u in enumerate(tiles):
        lo = u * ROW_TILE
        base = seg * SEG_ROWS
        w = tiles[seg - 1]
        after_lo = (w + 1) * ROW_TILE
        for c in range(N_SLABS):
            cols = slice(c * LANES, (c + 1) * LANES)
            before = xp_ref[:, cols] if u == 0 else x_ref[lo - HALO:lo, cols]
            after = xn_ref[0:1, cols] if w == CONV_TILES_PER_STEP - 1 else x_ref[after_lo:after_lo + 1, cols]
            xe_ref[blk, c, base:base + HALO, :] = jnp.where(sublane == 0, after, before)
            xe_ref[blk, c, base + HALO:base + SEG_ROWS, :] = x_ref[lo:lo + ROW_TILE, cols]
        for slot in range(HALO):
            value = in_sequence(u - 1, u) if slot == HALO - 1 else in_sequence(w, w + 1) if slot == 0 else 0.0
            keep = jnp.where(rows == _perm_row(base + slot), value, keep)
    h = (_load_permuted(xe_ref.at[blk]) * (1.0 + scale) + shift) * keep
    h_ref[blk] = h.astype(BF16)


def _conv3_permuted(t, cw):
    first_prev = pltpu.roll(t[BLOCK_ROWS - SUBLANES:BLOCK_ROWS], 1, 0)
    prev = jnp.concatenate([first_prev, t[0:BLOCK_ROWS - SUBLANES]], axis=0)
    last_next = pltpu.roll(t[0:SUBLANES], SUBLANES - 1, 0)
    nxt = jnp.concatenate([t[SUBLANES:BLOCK_ROWS], last_next], axis=0)
    return cw[0:1, :] * prev + cw[1:2, :] * t + cw[2:3, :] * nxt


def _finish_block(blk, xe_ref, acc, gate, lng, lnb, o_ref):
    y = DEEPNORM_ALPHA * _load_permuted(xe_ref.at[blk]) + gate * acc
    _store_unpermuted(_layer_norm(y, lng, lnb), xe_ref.at[blk])
    for seg in range(SEGS_PER_BLOCK):
        lo = (blk * SEGS_PER_BLOCK + seg) * ROW_TILE
        base = seg * SEG_ROWS
        for c in range(N_SLABS):
            o_ref[lo:lo + ROW_TILE, c * LANES:(c + 1) * LANES] = xe_ref[blk, c, base + HALO:base + SEG_ROWS, :]


def _ffn_kernel(x_ref, xp_ref, xn_ref, mod_ref, wup_ref, cw_ref, wdn_ref, lng_ref, lnb_ref, o_ref,
                xe_ref, h_ref, act_ref, *, tiles_per_seq):
    for blk in range(BLOCKS_PER_STEP):
        _stage_block(blk, tiles_per_seq, x_ref, xp_ref, xn_ref, mod_ref[3:4, :], mod_ref[4:5, :], xe_ref, h_ref)
        for j in range(N_FF_CHUNKS):
            a_cols = slice(j * FF_CHUNK, (j + 1) * FF_CHUNK)
            g_cols = slice(D_FF + j * FF_CHUNK, D_FF + (j + 1) * FF_CHUNK)
            a = _conv3_permuted(_dot(h_ref[blk], wup_ref[:, a_cols]), cw_ref[:, a_cols])
            g = _conv3_permuted(_dot(h_ref[blk], wup_ref[:, g_cols]), cw_ref[:, g_cols])
            act_ref[blk, :, a_cols] = (g * _sigmoid(g) * a).astype(BF16)
    for blk in range(BLOCKS_PER_STEP):
        acc = _dot(act_ref[blk], wdn_ref[...])
        _finish_block(blk, xe_ref, acc, mod_ref[5:6, :], lng_ref[1:2, :], lnb_ref[1:2, :], o_ref)


def _mixer_kernel(x_ref, xp_ref, xn_ref, mod_ref, win_ref, cw_ref, wout_ref, lng_ref, lnb_ref, o_ref,
                  xe_ref, h_ref, act_ref, *, tiles_per_seq):
    for blk in range(BLOCKS_PER_STEP):
        _stage_block(blk, tiles_per_seq, x_ref, xp_ref, xn_ref, mod_ref[0:1, :], mod_ref[1:2, :], xe_ref, h_ref)
        for j in range(N_MIX_CHUNKS):
            cols = slice(j * FF_CHUNK, (j + 1) * FF_CHUNK)
            bg, cg, xin = (_dot(h_ref[blk], win_ref[:, k * D_MODEL + j * FF_CHUNK:k * D_MODEL + (j + 1) * FF_CHUNK])
                           for k in range(3))
            y = bg * _conv3_permuted(cg * xin, cw_ref[:, cols])
            act_ref[blk, :, cols] = y.astype(BF16)
    for blk in range(BLOCKS_PER_STEP):
        acc = _dot(act_ref[blk], wout_ref[...])
        _finish_block(blk, xe_ref, acc, mod_ref[2:3, :], lng_ref[0:1, :], lnb_ref[0:1, :], o_ref)


def _gated_conv_block(kernel_fn, name, x, mod4, mod_row, w1, cw, w2, ln_g, ln_b, layer, w_layer):
    b, s, _ = x.shape
    hidden = w2.shape[1]
    n_rows = b * s
    step_rows = CONV_TILES_PER_STEP * ROW_TILE
    assert s % ROW_TILE == 0 and (s % step_rows == 0 or step_rows % s == 0)
    halo_blocks_per_step = step_rows // HALO
    last_halo_block = n_rows // HALO - 1
    ln_spec = pl.BlockSpec((None, 2, D_MODEL), lambda i: (layer, 0, 0))
    out = pl.pallas_call(
        functools.partial(kernel_fn, tiles_per_seq=s // ROW_TILE),
        grid=(n_rows // step_rows,),
        in_specs=[
            pl.BlockSpec((step_rows, D_MODEL), lambda i: (i, 0)),
            pl.BlockSpec((HALO, D_MODEL), lambda i: (jnp.maximum(i * halo_blocks_per_step - 1, 0), 0)),
            pl.BlockSpec((HALO, D_MODEL),
                         lambda i: (jnp.minimum((i + 1) * halo_blocks_per_step, last_halo_block), 0)),
            pl.BlockSpec((None, None, 6, D_MODEL), lambda i: (layer, mod_row(i * step_rows // s), 0, 0)),
            _resident((None,) + w1.shape[1:], lambda i: (w_layer, 0, 0)),
            _resident((None,) + cw.shape[1:], lambda i: (w_layer, 0, 0)),
            _resident((None,) + w2.shape[1:], lambda i: (w_layer, 0, 0)),
            ln_spec, ln_spec,
        ],
        out_specs=pl.BlockSpec((step_rows, D_MODEL), lambda i: (i, 0)),
        out_shape=jax.ShapeDtypeStruct((n_rows, D_MODEL), F32),
        scratch_shapes=[
            pltpu.VMEM((BLOCKS_PER_STEP, N_SLABS, BLOCK_ROWS, LANES), F32),
            pltpu.VMEM((BLOCKS_PER_STEP, BLOCK_ROWS, D_MODEL), BF16),
            pltpu.VMEM((BLOCKS_PER_STEP, BLOCK_ROWS, hidden), BF16),
        ],
        compiler_params=pltpu.CompilerParams(dimension_semantics=("arbitrary",),
                                             vmem_limit_bytes=CONV_VMEM_LIMIT_BYTES),
        name=name,
    )(*([x.reshape(n_rows, D_MODEL)] * 3), mod4, w1, cw, w2, ln_g, ln_b)
    return out.reshape(b, s, D_MODEL)


def _dft_tables(n, scale):
    jk = np.outer(np.arange(n), np.arange(n)) % n
    ang = 2.0 * np.pi * jk / n
    return (np.cos(ang) * scale).astype(np.float32), (np.sin(ang) * scale).astype(np.float32)


def _channel_dft_blocks():
    c, s = _dft_tables(FOURIER_GROUP, 1.0)
    eye = np.eye(MXU_WIDTH // FOURIER_GROUP, dtype=np.float32)
    return np.kron(eye, c), np.kron(eye, s)


def _rope_tables(n):
    half = HEAD_DIM // 2
    inv = 1.0 / (ROPE_THETA ** (np.arange(0, half, 2, dtype=np.float64) / half))
    pos = np.arange(n)
    d = np.arange(LANES) % HEAD_DIM
    coord = np.where(d < half, (pos // GRID_W)[:, None], (pos % GRID_W)[:, None])
    ang = coord * inv[d % (half // 2)][None, :]
    first = (d % half) < (half // 2)
    cos = np.cos(ang)
    sa = np.where(first[None, :], -np.sin(ang), 0.0)
    sb = np.where(first[None, :], 0.0, np.sin(ang))
    return tuple(jnp.asarray(t.astype(np.float32)) for t in (cos, sa, sb))


def _head_sum_matrix():
    return np.kron(np.eye(QK_WIDTH // HEAD_DIM, dtype=np.float32), np.ones((HEAD_DIM, HEAD_DIM), np.float32))


def kernel(x_prompt, x_sample, cache_k, cache_v, c, c_ctx, w_ada, b_ada, ln_g, ln_b, w_in_a, q_norm_g,
           k_norm_g, w_out_a, w_in_c, conv_c, w_out_c, w_up, conv_f, w_down):
    n_prompt, s_prompt, _ = x_prompt.shape
    n_sample, s_sample, _ = x_sample.shape
    past = cache_k.shape[2]

    cond = jnp.concatenate(
        [c_ctx[None, :], c, jnp.zeros((N_COND_ROWS - 1 - n_sample, D_MODEL), F32)], axis=0)
    mod4 = _modulation(cond, w_ada, b_ada).reshape(DEPTH, N_COND_ROWS, 6, D_MODEL)

    chan_c, chan_s = _channel_dft_blocks()
    w_cat = _fold_in_proj(w_in_a, jnp.asarray(chan_c), jnp.asarray(chan_s))
    head_sum = jnp.asarray(_head_sum_matrix()).astype(BF16)
    gains = jnp.concatenate([jnp.tile(q_norm_g[0], N_HEADS), jnp.tile(k_norm_g[0], N_KV_HEADS)])[None, :]
    w_out_bf = w_out_a[0].astype(BF16)

    def run_stream(x, mod_row, cache):
        s = x.shape[1]
        scale = (FOURIER_GROUP * s) ** -0.5
        dft_c, dft_s = (jnp.asarray(t).astype(BF16) for t in _dft_tables(s, scale))
        rope_tabs = _rope_tables(s) if cache is not None else None
        fcs, q, k, vt, *v = _in_proj(x, mod4, mod_row, w_cat, head_sum, gains, rope_tabs)
        x = _attn_out(x, fcs, q, k, vt, cache, dft_c, dft_s, w_out_bf, mod4, mod_row, ln_g, ln_b, 0)
        x = _gated_conv_block(_ffn_kernel, "conv_ffn", x, mod4, mod_row, w_up, conv_f, w_down, ln_g, ln_b, 0, 0)
        x = _gated_conv_block(_mixer_kernel, "conv_mixer", x, mod4, mod_row, w_in_c, conv_c, w_out_c,
                              ln_g, ln_b, 1, 0)
        x = _gated_conv_block(_ffn_kernel, "conv_ffn", x, mod4, mod_row, w_up, conv_f, w_down, ln_g, ln_b, 1, 1)
        return x, k, v

    y_prompt, k_new, (v_new,) = run_stream(x_prompt, lambda bi: 0, None)
    cache = (cache_k[:, 0].reshape(n_sample, past, KV_WIDTH),
             cache_v[:, 0].reshape(n_sample, past, KV_WIDTH).transpose(0, 2, 1))
    y_sample, _, _ = run_stream(x_sample, lambda bi: bi + 1, cache)

    new_shape = (n_prompt, 1, s_prompt, N_KV_HEADS, HEAD_DIM)
    return y_prompt, y_sample, k_new.reshape(new_shape), v_new.reshape(new_shape)
```

```python
import functools

import numpy as np
import jax
import jax.numpy as jnp
from jax import lax
from jax.experimental import pallas as pl
from jax.experimental.pallas import tpu as pltpu

D_MODEL = 1024
DEPTH = 2
GRID_W = 64
HEAD_DIM = 64
N_HEADS = 8
N_KV_HEADS = 2
Q_PER_KV = N_HEADS // N_KV_HEADS
ATTN_WIDTH = N_HEADS * HEAD_DIM
KV_WIDTH = N_KV_HEADS * HEAD_DIM
QK_WIDTH = ATTN_WIDTH + KV_WIDTH
FOURIER_GROUP = 64
FOURIER_WIDTH = 512
D_FF = 2816
ROPE_THETA = 10000.0
EPS = 1e-6
DEEPNORM_ALPHA = (2 * DEPTH) ** 0.25
LOG2_E = 1.4426950408889634

F32 = jnp.float32
BF16 = jnp.bfloat16

SUBLANES = 8
BF16_SUBLANES = 16
LANES = 128
MXU_WIDTH = 256

ROW_TILE = 256
ATTN_SUB_TILE = 128
KEY_CHUNK = MXU_WIDTH
HALO = SUBLANES
SEG_ROWS = HALO + ROW_TILE
SEGS_PER_BLOCK = 2
BLOCK_ROWS = SEGS_PER_BLOCK * SEG_ROWS
PERM_STRIDE = BLOCK_ROWS // SUBLANES
BLOCKS_PER_STEP = 1
N_SLABS = D_MODEL // LANES
TILES_PER_STEP = 2
CONV_TILES_PER_STEP = BLOCKS_PER_STEP * SEGS_PER_BLOCK
FF_CHUNK = MXU_WIDTH
N_FF_CHUNKS = D_FF // FF_CHUNK
N_MIX_CHUNKS = D_MODEL // FF_CHUNK
PROJ_WIDTH = 2 * FOURIER_WIDTH + QK_WIDTH + KV_WIDTH
N_COND_ROWS = 16
VMEM_LIMIT_BYTES = 48 * 1024 * 1024
CONV_VMEM_LIMIT_BYTES = 56 * 1024 * 1024


def _dot(a, b):
    return lax.dot_general(a, b, (((a.ndim - 1,), (0,)), ((), ())), preferred_element_type=F32)


def _split_bf16(a):
    hi = a.astype(BF16)
    lo = (a - hi.astype(F32)).astype(BF16)
    return hi, lo


def _sigmoid(x):
    return 1.0 / (1.0 + jnp.exp(-x))


def _layer_norm(y, g, b):
    mu = jnp.mean(y, axis=-1, keepdims=True)
    yc = y - mu
    var = jnp.mean(yc * yc, axis=-1, keepdims=True)
    return yc * lax.rsqrt(var + EPS) * g + b


def _params(n_axes):
    return pltpu.CompilerParams(dimension_semantics=("arbitrary",) * n_axes,
                                vmem_limit_bytes=VMEM_LIMIT_BYTES)


def _resident(block, index_map):
    return pl.BlockSpec(block, index_map, pipeline_mode=pl.Buffered(1))


def _mod_kernel(cond_ref, w_ref, b_ref, o_ref):
    c = cond_ref[...]
    s = (c * _sigmoid(c)).astype(BF16)
    o_ref[...] = _dot(s, w_ref[...].astype(BF16)) + b_ref[...]


def _modulation(cond, w_ada, b_ada):
    tn = 1536
    n_out = 6 * D_MODEL
    return pl.pallas_call(
        _mod_kernel,
        grid=(DEPTH, n_out // tn),
        in_specs=[
            pl.BlockSpec((N_COND_ROWS, D_MODEL), lambda l, n: (0, 0)),
            pl.BlockSpec((None, D_MODEL, tn), lambda l, n: (l, 0, n)),
            pl.BlockSpec((None, 1, tn), lambda l, n: (l, 0, n)),
        ],
        out_specs=pl.BlockSpec((None, N_COND_ROWS, tn), lambda l, n: (l, 0, n)),
        out_shape=jax.ShapeDtypeStruct((DEPTH, N_COND_ROWS, n_out), F32),
        compiler_params=_params(2),
        name="modulation",
    )(cond, w_ada, b_ada.reshape(DEPTH, 1, n_out))


def _fold_kernel(w_ref, c_ref, s_ref, o_ref):
    m_c, m_s = _split_bf16(c_ref[...]), _split_bf16(s_ref[...])
    for j in range(FOURIER_WIDTH // MXU_WIDTH):
        cols = slice(j * MXU_WIDTH, (j + 1) * MXU_WIDTH)
        w_hi, w_lo = _split_bf16(w_ref[:, cols])
        for half, (m_hi, m_lo) in enumerate((m_c, m_s)):
            out_cols = slice(half * FOURIER_WIDTH + j * MXU_WIDTH, half * FOURIER_WIDTH + (j + 1) * MXU_WIDTH)
            o_ref[:, out_cols] = (_dot(w_hi, m_hi) + _dot(w_hi, m_lo) + _dot(w_lo, m_hi)).astype(BF16)
    o_ref[:, 2 * FOURIER_WIDTH:] = w_ref[:, FOURIER_WIDTH:].astype(BF16)


def _fold_in_proj(w_in_a, dft_c, dft_s):
    return pl.pallas_call(
        _fold_kernel,
        grid=(1,),
        in_specs=[
            pl.BlockSpec((None,) + w_in_a.shape[1:], lambda j: (0, 0, 0)),
            pl.BlockSpec((MXU_WIDTH, MXU_WIDTH), lambda j: (0, 0)),
            pl.BlockSpec((MXU_WIDTH, MXU_WIDTH), lambda j: (0, 0)),
        ],
        out_specs=pl.BlockSpec((D_MODEL, PROJ_WIDTH), lambda j: (0, 0)),
        out_shape=jax.ShapeDtypeStruct((D_MODEL, PROJ_WIDTH), BF16),
        compiler_params=_params(1),
        name="fold_in_proj",
    )(w_in_a, dft_c, dft_s)


def _step_blocking(b, s):
    rows = min(s, TILES_PER_STEP * ROW_TILE)
    batches = TILES_PER_STEP * ROW_TILE // rows
    assert s % rows == 0 and b % batches == 0 and rows % ROW_TILE == 0
    return batches, rows, [(bu, ru) for bu in range(batches) for ru in range(0, rows, ROW_TILE)]


def _in_proj_kernel(*refs, rope, tiles):
    if rope:
        (x_ref, mod_ref, w_ref, hs_ref, g_ref, cos_ref, sa_ref, sb_ref,
         fcs_ref, q_ref, k_ref, vt_ref) = refs
    else:
        x_ref, mod_ref, w_ref, hs_ref, g_ref, fcs_ref, q_ref, k_ref, vt_ref, v_ref = refs
    shift = mod_ref[0:1, :]
    scale = mod_ref[1:2, :]
    for bu, ru in tiles:
        rows = slice(ru, ru + ROW_TILE)
        h = (x_ref[bu, rows, :] * (1.0 + scale) + shift).astype(BF16)
        qk_cols = slice(2 * FOURIER_WIDTH, 2 * FOURIER_WIDTH + QK_WIDTH)
        qk = _dot(h, w_ref[:, qk_cols])
        sq = (qk * qk).astype(BF16)
        ssq = jnp.concatenate(
            [_dot(sq[:, lo:min(lo + MXU_WIDTH, QK_WIDTH)],
                  hs_ref[lo:min(lo + MXU_WIDTH, QK_WIDTH), lo:min(lo + MXU_WIDTH, QK_WIDTH)])
             for lo in range(0, QK_WIDTH, MXU_WIDTH)], axis=1)
        fcs_ref[bu, rows, :] = _dot(h, w_ref[:, :2 * FOURIER_WIDTH]).astype(BF16)
        v = _dot(h, w_ref[:, 2 * FOURIER_WIDTH + QK_WIDTH:])
        y = qk * lax.rsqrt(ssq * (1.0 / HEAD_DIM) + EPS) * g_ref[...]
        if rope:
            pos = pl.ds(pl.multiple_of(pl.program_id(1) * x_ref.shape[1] + ru, ROW_TILE), ROW_TILE)
            parts = []
            for t in range(QK_WIDTH // LANES):
                yt = y[:, t * LANES:(t + 1) * LANES]
                parts.append(yt * cos_ref[pos, :]
                             + pltpu.roll(yt, LANES - 16, 1) * sa_ref[pos, :]
                             + pltpu.roll(yt, 16, 1) * sb_ref[pos, :])
            y = jnp.concatenate(parts, axis=1)
        q_ref[bu, rows, :] = (y[:, :ATTN_WIDTH] * (HEAD_DIM ** -0.5 * LOG2_E)).astype(q_ref.dtype)
        k_ref[bu, rows, :] = y[:, ATTN_WIDTH:].astype(k_ref.dtype)
        vt_ref[bu, :, rows] = v.T.astype(vt_ref.dtype)
        if not rope:
            v_ref[bu, rows, :] = v


def _in_proj(x, mod4, mod_row, w_cat, head_sum, gains, rope_tabs):
    b, s, _ = x.shape
    rope = rope_tabs is not None
    bb, tr, tiles = _step_blocking(b, s)
    row_spec = lambda w: pl.BlockSpec((bb, tr, w), lambda bi, i: (bi, i, 0))
    out_specs = [row_spec(2 * FOURIER_WIDTH), row_spec(ATTN_WIDTH), row_spec(KV_WIDTH),
                 pl.BlockSpec((bb, KV_WIDTH, tr), lambda bi, i: (bi, 0, i))]
    out_shape = [
        jax.ShapeDtypeStruct((b, s, 2 * FOURIER_WIDTH), BF16),
        jax.ShapeDtypeStruct((b, s, ATTN_WIDTH), BF16),
        jax.ShapeDtypeStruct((b, s, KV_WIDTH), BF16 if rope else F32),
        jax.ShapeDtypeStruct((b, KV_WIDTH, s), BF16),
    ]
    if not rope:
        out_specs.append(row_spec(KV_WIDTH))
        out_shape.append(jax.ShapeDtypeStruct((b, s, KV_WIDTH), F32))
    in_specs = [
        row_spec(D_MODEL),
        pl.BlockSpec((None, None, 6, D_MODEL), lambda bi, i: (0, mod_row(bi * bb), 0, 0)),
        _resident((D_MODEL, PROJ_WIDTH), lambda bi, i: (0, 0)),
        _resident((QK_WIDTH, QK_WIDTH), lambda bi, i: (0, 0)),
        _resident((1, QK_WIDTH), lambda bi, i: (0, 0)),
    ]
    args = [x, mod4, w_cat, head_sum, gains]
    if rope:
        in_specs += [_resident((s, LANES), lambda bi, i: (0, 0))] * 3
        args += list(rope_tabs)
    return pl.pallas_call(
        functools.partial(_in_proj_kernel, rope=rope, tiles=tiles),
        grid=(b // bb, s // tr),
        in_specs=in_specs,
        out_specs=out_specs,
        out_shape=out_shape,
        compiler_params=_params(2),
        name="in_proj_rope" if rope else "in_proj",
    )(*args)


def _attention_scores(q_ref, bu, row0, key_sets, sub_tile):
    nt = (((1,), (1,)), ((), ()))
    q = q_ref[bu, row0:row0 + sub_tile, :]
    items = []
    for j in range(N_KV_HEADS):
        kv = slice(j * HEAD_DIM, (j + 1) * HEAD_DIM)
        qs = jnp.concatenate(
            [q[:, h * HEAD_DIM:(h + 1) * HEAD_DIM] for h in range(j * Q_PER_KV, (j + 1) * Q_PER_KV)], axis=0)
        items.append([lax.dot_general(keys_ref[bu, :, kv].astype(BF16), qs, nt, preferred_element_type=F32)
                      for keys_ref, _ in key_sets])
    return items


def _attention_finish(items, bu, key_sets, attn_ref, arow0, sub_tile):
    for j, sts in enumerate(items):
        parts = []
        for st_all, (_, vts_ref) in zip(sts, key_sets):
            n_keys = st_all.shape[0]
            chunk = min(KEY_CHUNK, n_keys)
            ones_rows = jnp.ones((BF16_SUBLANES, chunk), BF16)
            for c in range(0, n_keys, chunk):
                st = st_all[c:c + chunk]
                m = jnp.max(st, axis=0, keepdims=True)
                vt_ext = jnp.concatenate(
                    [vts_ref[bu, j * HEAD_DIM:(j + 1) * HEAD_DIM, c:c + chunk].astype(BF16), ones_rows], axis=0)
                parts.append((m, _dot(vt_ext, jnp.exp2(st - m).astype(BF16))))
        m_all = functools.reduce(jnp.maximum, [m for m, _ in parts])
        ot = (functools.reduce(jnp.add, [jnp.exp2(m - m_all) * o for m, o in parts])
              if len(parts) > 1 else parts[0][1])
        o = ot[0:HEAD_DIM] / ot[HEAD_DIM:HEAD_DIM + 1]
        for g in range(0, Q_PER_KV, 2):
            pair = jnp.concatenate([o[:, g * sub_tile:(g + 1) * sub_tile],
                                    o[:, (g + 1) * sub_tile:(g + 2) * sub_tile]], axis=0).T
            lane0 = (j * Q_PER_KV + g) * HEAD_DIM
            attn_ref[arow0:arow0 + sub_tile, lane0:lane0 + LANES] = pair.astype(attn_ref.dtype)


def _attn_out_kernel(*refs, has_cache, tiles, sub_tile):
    if has_cache:
        (q_ref, k_ref, vt_ref, ck_ref, cvt_ref, x_ref, fcs_ref, dc_ref, ds_ref, w_ref, mod_ref, lng_ref, lnb_ref,
         o_ref, attn_ref) = refs
        key_sets = [(k_ref, vt_ref), (ck_ref, cvt_ref)]
    else:
        (q_ref, k_ref, vt_ref, x_ref, fcs_ref, dc_ref, ds_ref, w_ref, mod_ref, lng_ref, lnb_ref,
         o_ref, attn_ref) = refs
        key_sets = [(k_ref, vt_ref)]
    gate = mod_ref[2:3, :]
    subs_per_tile = ROW_TILE // sub_tile
    subs = [(n, bu, ru, t) for n, (bu, ru) in enumerate(tiles) for t in range(subs_per_tile)]
    scores = lambda n, bu, ru, t: _attention_scores(q_ref, bu, ru + t * sub_tile, key_sets, sub_tile)

    four = None
    items_next = scores(*subs[0])
    for i, (n, bu, ru, t) in enumerate(subs):
        items = items_next
        if i + 1 < len(subs):
            items_next = scores(*subs[i + 1])
        if t == 0:
            pos = pl.ds(pl.multiple_of(pl.program_id(1) * x_ref.shape[1] + ru, ROW_TILE), ROW_TILE)
            four = (_dot(dc_ref[pos, :], fcs_ref[bu, :, :FOURIER_WIDTH])
                    - _dot(ds_ref[pos, :], fcs_ref[bu, :, FOURIER_WIDTH:]))
        _attention_finish(items, bu, key_sets, attn_ref, n * ROW_TILE + t * sub_tile, sub_tile)
        if t == subs_per_tile - 1:
            rows = slice(ru, ru + ROW_TILE)
            out = (_dot(four.astype(BF16), w_ref[:FOURIER_WIDTH, :])
                   + _dot(attn_ref[n * ROW_TILE:(n + 1) * ROW_TILE, :], w_ref[FOURIER_WIDTH:, :]))
            y = DEEPNORM_ALPHA * x_ref[bu, rows, :] + gate * out
            o_ref[bu, rows, :] = _layer_norm(y, lng_ref[0:1, :], lnb_ref[0:1, :])


def _attn_out(x, fcs, q, k, vt, cache, dft_c, dft_s, w_out, mod4, mod_row, ln_g, ln_b, layer):
    b, s, _ = x.shape
    has_cache = cache is not None
    bb, tr, tiles = _step_blocking(b, s)
    row_spec = lambda w: pl.BlockSpec((bb, tr, w), lambda bi, i: (bi, i, 0))
    seq_spec = lambda n, w: pl.BlockSpec((bb, n, w), lambda bi, i: (bi, 0, 0))
    seq_t_spec = lambda n: pl.BlockSpec((bb, KV_WIDTH, n), lambda bi, i: (bi, 0, 0))
    ln_spec = pl.BlockSpec((None, 2, D_MODEL), lambda bi, i: (layer, 0, 0))
    in_specs = [row_spec(ATTN_WIDTH), seq_spec(s, KV_WIDTH), seq_t_spec(s)]
    args = [q, k, vt]
    if has_cache:
        past = cache[0].shape[1]
        in_specs += [seq_spec(past, KV_WIDTH), seq_t_spec(past)]
        args += list(cache)
    in_specs += [
        row_spec(D_MODEL),
        seq_spec(s, 2 * FOURIER_WIDTH),
        _resident((s, s), lambda bi, i: (0, 0)),
        _resident((s, s), lambda bi, i: (0, 0)),
        _resident((2 * FOURIER_WIDTH, D_MODEL), lambda bi, i: (0, 0)),
        pl.BlockSpec((None, None, 6, D_MODEL), lambda bi, i: (layer, mod_row(bi * bb), 0, 0)),
        ln_spec, ln_spec,
    ]
    args += [x, fcs, dft_c, dft_s, w_out, mod4, ln_g, ln_b]
    return pl.pallas_call(
        functools.partial(_attn_out_kernel, has_cache=has_cache, tiles=tiles, sub_tile=ATTN_SUB_TILE),
        grid=(b // bb, s // tr),
        in_specs=in_specs,
        out_specs=row_spec(D_MODEL),
        out_shape=jax.ShapeDtypeStruct((b, s, D_MODEL), F32),
        scratch_shapes=[pltpu.VMEM((bb * tr, ATTN_WIDTH), BF16)],
        compiler_params=_params(2),
        name="attn_out_cached" if has_cache else "attn_out",
    )(*args)


def _load_permuted(slab_ref):
    slabs = [jnp.concatenate([slab_ref[c, pl.ds(v, SUBLANES, stride=PERM_STRIDE), :] for v in range(PERM_STRIDE)],
                             axis=0) for c in range(N_SLABS)]
    return jnp.concatenate(slabs, axis=1)


def _store_unpermuted(y, slab_ref):
    for c in range(N_SLABS):
        for v in range(PERM_STRIDE):
            slab_ref[c, pl.ds(v, SUBLANES, stride=PERM_STRIDE), :] = (
                y[v * SUBLANES:(v + 1) * SUBLANES, c * LANES:(c + 1) * LANES])


def _perm_row(q):
    return SUBLANES * (q % PERM_STRIDE) + q // PERM_STRIDE


def _stage_block(blk, stream, streams, shift, scale, xe_ref, h_ref):
    sublane = lax.broadcasted_iota(jnp.int32, (SUBLANES, 1), 0)
    rows = lax.broadcasted_iota(jnp.int32, (BLOCK_ROWS, 1), 0)
    tiles = [blk * SEGS_PER_BLOCK + seg for seg in range(SEGS_PER_BLOCK)]

    def of_stream(pick):
        value = pick(*streams[0][0])
        for n in range(1, len(streams)):
            value = jnp.where(stream == n, pick(*streams[n][0]), value)
        return value

    for seg, u in enumerate(tiles):
        lo = u * ROW_TILE
        base = seg * SEG_ROWS
        w = tiles[seg - 1]
        after_lo = (w + 1) * ROW_TILE
        for c in range(N_SLABS):
            cols = slice(c * LANES, (c + 1) * LANES)
            before = of_stream(lambda x_ref, xp_ref, xn_ref:
                               xp_ref[:, cols] if u == 0 else x_ref[lo - HALO:lo, cols])
            after = of_stream(lambda x_ref, xp_ref, xn_ref:
                              xn_ref[0:1, cols] if w == CONV_TILES_PER_STEP - 1
                              else x_ref[after_lo:after_lo + 1, cols])
            xe_ref[blk, c, base:base + HALO, :] = jnp.where(sublane == 0, after, before)
            xe_ref[blk, c, base + HALO:base + SEG_ROWS, :] = of_stream(
                lambda x_ref, xp_ref, xn_ref: x_ref[lo:lo + ROW_TILE, cols])

    def in_sequence(second_tile):
        flag = 0.0
        for n, (_, first_step, tiles_per_seq) in enumerate(streams):
            if tiles_per_seq > 1:
                tile = (pl.program_id(0) - first_step) * CONV_TILES_PER_STEP + second_tile
                inside = jnp.logical_and(stream == n, lax.rem(tile, tiles_per_seq) != 0)
                flag = jnp.where(inside, 1.0, flag)
        return flag

    keep = jnp.ones((BLOCK_ROWS, 1), F32)
    for seg, u in enumerate(tiles):
        base = seg * SEG_ROWS
        w = tiles[seg - 1]
        for slot in range(HALO):
            value = in_sequence(u) if slot == HALO - 1 else in_sequence(w + 1) if slot == 0 else 0.0
            keep = jnp.where(rows == _perm_row(base + slot), value, keep)
    h = (_load_permuted(xe_ref.at[blk]) * (1.0 + scale) + shift) * keep
    h_ref[blk] = h.astype(BF16)


def _conv3_permuted(t, cw):
    first_prev = pltpu.roll(t[BLOCK_ROWS - SUBLANES:BLOCK_ROWS], 1, 0)
    prev = jnp.concatenate([first_prev, t[0:BLOCK_ROWS - SUBLANES]], axis=0)
    last_next = pltpu.roll(t[0:SUBLANES], SUBLANES - 1, 0)
    nxt = jnp.concatenate([t[SUBLANES:BLOCK_ROWS], last_next], axis=0)
    return cw[0:1, :] * prev + cw[1:2, :] * t + cw[2:3, :] * nxt


def _finish_block(blk, stream, out_refs, xe_ref, acc, gate, lng, lnb):
    y = DEEPNORM_ALPHA * _load_permuted(xe_ref.at[blk]) + gate * acc
    _store_unpermuted(_layer_norm(y, lng, lnb), xe_ref.at[blk])

    def copy_out(o_ref):
        for seg in range(SEGS_PER_BLOCK):
            lo = (blk * SEGS_PER_BLOCK + seg) * ROW_TILE
            base = seg * SEG_ROWS
            for c in range(N_SLABS):
                o_ref[lo:lo + ROW_TILE, c * LANES:(c + 1) * LANES] = xe_ref[blk, c, base + HALO:base + SEG_ROWS, :]

    copy_out(out_refs[-1])
    for n, o_ref in enumerate(out_refs[:-1]):
        pl.when(stream == n)(functools.partial(copy_out, o_ref))


def _split_conv_refs(refs, stream_steps):
    n_streams = len(stream_steps)
    x_refs = [refs[3 * n:3 * n + 3] for n in range(n_streams)]
    rest = refs[3 * n_streams:]
    stream = jnp.int32(0)
    for n in range(1, n_streams):
        stream = jnp.where(pl.program_id(0) >= stream_steps[n][0], n, stream)
    streams = [(x_refs[n], first, tps) for n, (first, tps) in enumerate(stream_steps)]
    return stream, streams, rest


def _ffn_kernel(*refs, stream_steps):
    stream, streams, rest = _split_conv_refs(refs, stream_steps)
    n = len(streams)
    mod_ref, wup_ref, cw_ref, wdn_ref, lng_ref, lnb_ref = rest[:6]
    out_refs, (xe_ref, h_ref, act_ref) = rest[6:6 + n], rest[6 + n:]
    for blk in range(BLOCKS_PER_STEP):
        _stage_block(blk, stream, streams, mod_ref[3:4, :], mod_ref[4:5, :], xe_ref, h_ref)
        for j in range(N_FF_CHUNKS):
            a_cols = slice(j * FF_CHUNK, (j + 1) * FF_CHUNK)
            g_cols = slice(D_FF + j * FF_CHUNK, D_FF + (j + 1) * FF_CHUNK)
            a = _conv3_permuted(_dot(h_ref[blk], wup_ref[:, a_cols]), cw_ref[:, a_cols])
            g = _conv3_permuted(_dot(h_ref[blk], wup_ref[:, g_cols]), cw_ref[:, g_cols])
            act_ref[blk, :, a_cols] = (g * _sigmoid(g) * a).astype(BF16)
    for blk in range(BLOCKS_PER_STEP):
        acc = _dot(act_ref[blk], wdn_ref[...])
        _finish_block(blk, stream, out_refs, xe_ref, acc, mod_ref[5:6, :], lng_ref[1:2, :], lnb_ref[1:2, :])


def _mixer_kernel(*refs, stream_steps):
    stream, streams, rest = _split_conv_refs(refs, stream_steps)
    n = len(streams)
    mod_ref, win_ref, cw_ref, wout_ref, lng_ref, lnb_ref = rest[:6]
    out_refs, (xe_ref, h_ref, act_ref) = rest[6:6 + n], rest[6 + n:]
    for blk in range(BLOCKS_PER_STEP):
        _stage_block(blk, stream, streams, mod_ref[0:1, :], mod_ref[1:2, :], xe_ref, h_ref)
        for j in range(N_MIX_CHUNKS):
            cols = slice(j * FF_CHUNK, (j + 1) * FF_CHUNK)
            bg, cg, xin = (_dot(h_ref[blk], win_ref[:, k * D_MODEL + j * FF_CHUNK:k * D_MODEL + (j + 1) * FF_CHUNK])
                           for k in range(3))
            y = bg * _conv3_permuted(cg * xin, cw_ref[:, cols])
            act_ref[blk, :, cols] = y.astype(BF16)
    for blk in range(BLOCKS_PER_STEP):
        acc = _dot(act_ref[blk], wout_ref[...])
        _finish_block(blk, stream, out_refs, xe_ref, acc, mod_ref[2:3, :], lng_ref[0:1, :], lnb_ref[0:1, :])


def _gated_conv_block(kernel_fn, name, xs, mod_rows, mod4, w1, cw, w2, ln_g, ln_b, layer, w_layer):
    hidden = w2.shape[1]
    step_rows = CONV_TILES_PER_STEP * ROW_TILE
    halo_blocks_per_step = step_rows // HALO
    stream_steps, in_specs, out_specs, out_shape, args = [], [], [], [], []
    first = 0
    for x in xs:
        b, s, _ = x.shape
        assert s % ROW_TILE == 0 and (s % step_rows == 0 or step_rows % s == 0)
        n_steps = b * s // step_rows
        last_halo_block = b * s // HALO - 1
        own = lambda i, first=first, n_steps=n_steps: jnp.clip(i - first, 0, n_steps - 1)
        in_specs += [
            pl.BlockSpec((step_rows, D_MODEL), lambda i, own=own: (own(i), 0)),
            pl.BlockSpec((HALO, D_MODEL),
                         lambda i, own=own: (jnp.maximum(own(i) * halo_blocks_per_step - 1, 0), 0)),
            pl.BlockSpec((HALO, D_MODEL),
                         lambda i, own=own, last=last_halo_block:
                         (jnp.minimum((own(i) + 1) * halo_blocks_per_step, last), 0)),
        ]
        out_specs.append(pl.BlockSpec((step_rows, D_MODEL), lambda i, own=own: (own(i), 0)))
        out_shape.append(jax.ShapeDtypeStruct((b * s, D_MODEL), F32))
        args += [x.reshape(b * s, D_MODEL)] * 3
        stream_steps.append((first, s // ROW_TILE))
        first += n_steps

    def mod_index(i):
        row = jnp.int32(0)
        for (start, _), x, mod_row in zip(stream_steps, xs, mod_rows):
            row = jnp.where(i >= start, mod_row((i - start) * step_rows // x.shape[1]), row)
        return (layer, row, 0, 0)

    ln_spec = pl.BlockSpec((None, 2, D_MODEL), lambda i: (layer, 0, 0))
    in_specs += [
        pl.BlockSpec((None, None, 6, D_MODEL), mod_index),
        _resident((None,) + w1.shape[1:], lambda i: (w_layer, 0, 0)),
        _resident((None,) + cw.shape[1:], lambda i: (w_layer, 0, 0)),
        _resident((None,) + w2.shape[1:], lambda i: (w_layer, 0, 0)),
        ln_spec, ln_spec,
    ]
    outs = pl.pallas_call(
        functools.partial(kernel_fn, stream_steps=tuple(stream_steps)),
        grid=(first,),
        in_specs=in_specs,
        out_specs=out_specs,
        out_shape=out_shape,
        scratch_shapes=[
            pltpu.VMEM((BLOCKS_PER_STEP, N_SLABS, BLOCK_ROWS, LANES), F32),
            pltpu.VMEM((BLOCKS_PER_STEP, BLOCK_ROWS, D_MODEL), BF16),
            pltpu.VMEM((BLOCKS_PER_STEP, BLOCK_ROWS, hidden), BF16),
        ],
        compiler_params=pltpu.CompilerParams(dimension_semantics=("arbitrary",),
                                             vmem_limit_bytes=CONV_VMEM_LIMIT_BYTES),
        name=name,
    )(*args, mod4, w1, cw, w2, ln_g, ln_b)
    return [o.reshape(x.shape) for o, x in zip(outs, xs)]


def _dft_tables(n, scale):
    jk = np.outer(np.arange(n), np.arange(n)) % n
    ang = 2.0 * np.pi * jk / n
    return (np.cos(ang) * scale).astype(np.float32), (np.sin(ang) * scale).astype(np.float32)


def _channel_dft_blocks():
    c, s = _dft_tables(FOURIER_GROUP, 1.0)
    eye = np.eye(MXU_WIDTH // FOURIER_GROUP, dtype=np.float32)
    return np.kron(eye, c), np.kron(eye, s)


def _rope_tables(n):
    half = HEAD_DIM // 2
    inv = 1.0 / (ROPE_THETA ** (np.arange(0, half, 2, dtype=np.float64) / half))
    pos = np.arange(n)
    d = np.arange(LANES) % HEAD_DIM
    coord = np.where(d < half, (pos // GRID_W)[:, None], (pos % GRID_W)[:, None])
    ang = coord * inv[d % (half // 2)][None, :]
    first = (d % half) < (half // 2)
    cos = np.cos(ang)
    sa = np.where(first[None, :], -np.sin(ang), 0.0)
    sb = np.where(first[None, :], 0.0, np.sin(ang))
    return tuple(jnp.asarray(t.astype(np.float32)) for t in (cos, sa, sb))


def _head_sum_matrix():
    return np.kron(np.eye(QK_WIDTH // HEAD_DIM, dtype=np.float32), np.ones((HEAD_DIM, HEAD_DIM), np.float32))


def kernel(x_prompt, x_sample, cache_k, cache_v, c, c_ctx, w_ada, b_ada, ln_g, ln_b, w_in_a, q_norm_g,
           k_norm_g, w_out_a, w_in_c, conv_c, w_out_c, w_up, conv_f, w_down):
    n_prompt, s_prompt, _ = x_prompt.shape
    n_sample, s_sample, _ = x_sample.shape
    past = cache_k.shape[2]

    cond = jnp.concatenate(
        [c_ctx[None, :], c, jnp.zeros((N_COND_ROWS - 1 - n_sample, D_MODEL), F32)], axis=0)
    mod4 = _modulation(cond, w_ada, b_ada).reshape(DEPTH, N_COND_ROWS, 6, D_MODEL)

    chan_c, chan_s = _channel_dft_blocks()
    w_cat = _fold_in_proj(w_in_a, jnp.asarray(chan_c), jnp.asarray(chan_s))
    head_sum = jnp.asarray(_head_sum_matrix()).astype(BF16)
    gains = jnp.concatenate([jnp.tile(q_norm_g[0], N_HEADS), jnp.tile(k_norm_g[0], N_KV_HEADS)])[None, :]
    w_out_bf = w_out_a[0].astype(BF16)

    def layer0_mixer(x, mod_row, cache):
        s = x.shape[1]
        scale = (FOURIER_GROUP * s) ** -0.5
        dft_c, dft_s = (jnp.asarray(t).astype(BF16) for t in _dft_tables(s, scale))
        rope_tabs = _rope_tables(s) if cache is not None else None
        fcs, q, k, vt, *v = _in_proj(x, mod4, mod_row, w_cat, head_sum, gains, rope_tabs)
        x = _attn_out(x, fcs, q, k, vt, cache, dft_c, dft_s, w_out_bf, mod4, mod_row, ln_g, ln_b, 0)
        return x, k, v

    mod_rows = (lambda bi: 0, lambda bi: bi + 1)
    cache = (cache_k[:, 0].reshape(n_sample, past, KV_WIDTH),
             cache_v[:, 0].reshape(n_sample, past, KV_WIDTH).transpose(0, 2, 1))
    xp, k_new, (v_new,) = layer0_mixer(x_prompt, mod_rows[0], None)
    xs, _, _ = layer0_mixer(x_sample, mod_rows[1], cache)
    xs = [xp, xs]
    xs = _gated_conv_block(_ffn_kernel, "conv_ffn", xs, mod_rows, mod4, w_up, conv_f, w_down, ln_g, ln_b, 0, 0)
    xs = _gated_conv_block(_mixer_kernel, "conv_mixer", xs, mod_rows, mod4, w_in_c, conv_c, w_out_c, ln_g, ln_b, 1, 0)
    y_prompt, y_sample = _gated_conv_block(_ffn_kernel, "conv_ffn", xs, mod_rows, mod4, w_up, conv_f, w_down,
                                           ln_g, ln_b, 1, 1)

    new_shape = (n_prompt, 1, s_prompt, N_KV_HEADS, HEAD_DIM)
    return y_prompt, y_sample, k_new.reshape(new_shape), v_new.reshape(new_shape)
```

```python
import functools

import numpy as np
import jax
import jax.numpy as jnp
from jax import lax
from jax.experimental import pallas as pl
from jax.experimental.pallas import tpu as pltpu

D_MODEL = 1024
DEPTH = 2
GRID_W = 64
HEAD_DIM = 64
N_HEADS = 8
N_KV_HEADS = 2
Q_PER_KV = N_HEADS // N_KV_HEADS
ATTN_WIDTH = N_HEADS * HEAD_DIM
KV_WIDTH = N_KV_HEADS * HEAD_DIM
QK_WIDTH = ATTN_WIDTH + KV_WIDTH
FOURIER_GROUP = 64
FOURIER_WIDTH = 512
D_FF = 2816
ROPE_THETA = 10000.0
EPS = 1e-6
DEEPNORM_ALPHA = (2 * DEPTH) ** 0.25
LOG2_E = 1.4426950408889634

F32 = jnp.float32
BF16 = jnp.bfloat16

SUBLANES = 8
BF16_SUBLANES = 16
LANES = 128
MXU_WIDTH = 256

ROW_TILE = 256
ATTN_SUB_TILE = 128
KEY_CHUNK = MXU_WIDTH
HALO = SUBLANES
SEG_ROWS = HALO + ROW_TILE
SEGS_PER_BLOCK = 2
BLOCK_ROWS = SEGS_PER_BLOCK * SEG_ROWS
PERM_STRIDE = BLOCK_ROWS // SUBLANES
BLOCKS_PER_STEP = 1
N_SLABS = D_MODEL // LANES
TILES_PER_STEP = 2
CONV_TILES_PER_STEP = BLOCKS_PER_STEP * SEGS_PER_BLOCK
FF_CHUNK = MXU_WIDTH
N_FF_CHUNKS = D_FF // FF_CHUNK
N_MIX_CHUNKS = D_MODEL // FF_CHUNK
PROJ_WIDTH = 2 * FOURIER_WIDTH + QK_WIDTH + KV_WIDTH
N_COND_ROWS = 16
VMEM_LIMIT_BYTES = 48 * 1024 * 1024
CONV_VMEM_LIMIT_BYTES = 56 * 1024 * 1024


def _dot(a, b):
    return lax.dot_general(a, b, (((a.ndim - 1,), (0,)), ((), ())), preferred_element_type=F32)


def _split_bf16(a):
    hi = a.astype(BF16)
    lo = (a - hi.astype(F32)).astype(BF16)
    return hi, lo


def _sigmoid(x):
    return 1.0 / (1.0 + jnp.exp(-x))


def _layer_norm(y, g, b):
    mu = jnp.mean(y, axis=-1, keepdims=True)
    yc = y - mu
    var = jnp.mean(yc * yc, axis=-1, keepdims=True)
    return yc * lax.rsqrt(var + EPS) * g + b


def _params(n_axes):
    return pltpu.CompilerParams(dimension_semantics=("arbitrary",) * n_axes,
                                vmem_limit_bytes=VMEM_LIMIT_BYTES)


def _resident(block, index_map):
    return pl.BlockSpec(block, index_map, pipeline_mode=pl.Buffered(1))


def _mod_kernel(cond_ref, w_ref, b_ref, o_ref):
    c = cond_ref[...]
    s = (c * _sigmoid(c)).astype(BF16)
    o_ref[...] = _dot(s, w_ref[...].astype(BF16)) + b_ref[...]


def _modulation(cond, w_ada, b_ada):
    tn = 1536
    n_out = 6 * D_MODEL
    return pl.pallas_call(
        _mod_kernel,
        grid=(DEPTH, n_out // tn),
        in_specs=[
            pl.BlockSpec((N_COND_ROWS, D_MODEL), lambda l, n: (0, 0)),
            pl.BlockSpec((None, D_MODEL, tn), lambda l, n: (l, 0, n)),
            pl.BlockSpec((None, 1, tn), lambda l, n: (l, 0, n)),
        ],
        out_specs=pl.BlockSpec((None, N_COND_ROWS, tn), lambda l, n: (l, 0, n)),
        out_shape=jax.ShapeDtypeStruct((DEPTH, N_COND_ROWS, n_out), F32),
        compiler_params=_params(2),
        name="modulation",
    )(cond, w_ada, b_ada.reshape(DEPTH, 1, n_out))


def _fold_kernel(w_ref, c_ref, s_ref, o_ref):
    m_c, m_s = _split_bf16(c_ref[...]), _split_bf16(s_ref[...])
    for j in range(FOURIER_WIDTH // MXU_WIDTH):
        cols = slice(j * MXU_WIDTH, (j + 1) * MXU_WIDTH)
        w_hi, w_lo = _split_bf16(w_ref[:, cols])
        for half, (m_hi, m_lo) in enumerate((m_c, m_s)):
            out_cols = slice(half * FOURIER_WIDTH + j * MXU_WIDTH, half * FOURIER_WIDTH + (j + 1) * MXU_WIDTH)
            o_ref[:, out_cols] = (_dot(w_hi, m_hi) + _dot(w_hi, m_lo) + _dot(w_lo, m_hi)).astype(BF16)
    o_ref[:, 2 * FOURIER_WIDTH:] = w_ref[:, FOURIER_WIDTH:].astype(BF16)


def _fold_in_proj(w_in_a, dft_c, dft_s):
    return pl.pallas_call(
        _fold_kernel,
        grid=(1,),
        in_specs=[
            pl.BlockSpec((None,) + w_in_a.shape[1:], lambda j: (0, 0, 0)),
            pl.BlockSpec((MXU_WIDTH, MXU_WIDTH), lambda j: (0, 0)),
            pl.BlockSpec((MXU_WIDTH, MXU_WIDTH), lambda j: (0, 0)),
        ],
        out_specs=pl.BlockSpec((D_MODEL, PROJ_WIDTH), lambda j: (0, 0)),
        out_shape=jax.ShapeDtypeStruct((D_MODEL, PROJ_WIDTH), BF16),
        compiler_params=_params(1),
        name="fold_in_proj",
    )(w_in_a, dft_c, dft_s)


def _step_blocking(b, s):
    rows = min(s, TILES_PER_STEP * ROW_TILE)
    batches = TILES_PER_STEP * ROW_TILE // rows
    assert s % rows == 0 and b % batches == 0 and rows % ROW_TILE == 0
    return batches, rows, [(bu, ru) for bu in range(batches) for ru in range(0, rows, ROW_TILE)]


def _in_proj_kernel(*refs, rope, tiles):
    if rope:
        (x_ref, mod_ref, w_ref, hs_ref, g_ref, cos_ref, sa_ref, sb_ref,
         fcs_ref, q_ref, k_ref, vt_ref) = refs
    else:
        x_ref, mod_ref, w_ref, hs_ref, g_ref, fcs_ref, q_ref, k_ref, vt_ref, v_ref = refs
    shift = mod_ref[0:1, :]
    scale = mod_ref[1:2, :]
    for bu, ru in tiles:
        rows = slice(ru, ru + ROW_TILE)
        h = (x_ref[bu, rows, :] * (1.0 + scale) + shift).astype(BF16)
        qkv = _dot(h, w_ref[:, 2 * FOURIER_WIDTH:])
        qk, v = qkv[:, :QK_WIDTH], qkv[:, QK_WIDTH:]
        sq = (qk * qk).astype(BF16)
        ssq = jnp.concatenate(
            [_dot(sq[:, lo:min(lo + MXU_WIDTH, QK_WIDTH)],
                  hs_ref[lo:min(lo + MXU_WIDTH, QK_WIDTH), lo:min(lo + MXU_WIDTH, QK_WIDTH)])
             for lo in range(0, QK_WIDTH, MXU_WIDTH)], axis=1)
        fcs_ref[bu, rows, :] = _dot(h, w_ref[:, :2 * FOURIER_WIDTH]).astype(BF16)
        y = qk * lax.rsqrt(ssq * (1.0 / HEAD_DIM) + EPS) * g_ref[...]
        if rope:
            pos = pl.ds(pl.multiple_of(pl.program_id(1) * x_ref.shape[1] + ru, ROW_TILE), ROW_TILE)
            parts = []
            for t in range(QK_WIDTH // LANES):
                yt = y[:, t * LANES:(t + 1) * LANES]
                parts.append(yt * cos_ref[pos, :]
                             + pltpu.roll(yt, LANES - 16, 1) * sa_ref[pos, :]
                             + pltpu.roll(yt, 16, 1) * sb_ref[pos, :])
            y = jnp.concatenate(parts, axis=1)
        q_ref[bu, rows, :] = (y[:, :ATTN_WIDTH] * (HEAD_DIM ** -0.5 * LOG2_E)).astype(q_ref.dtype)
        k_ref[bu, rows, :] = y[:, ATTN_WIDTH:].astype(k_ref.dtype)
        vt_ref[bu, :, rows] = v.T.astype(vt_ref.dtype)
        if not rope:
            v_ref[bu, rows, :] = v


def _in_proj(x, mod4, mod_row, w_cat, head_sum, gains, rope_tabs):
    b, s, _ = x.shape
    rope = rope_tabs is not None
    bb, tr, tiles = _step_blocking(b, s)
    row_spec = lambda w: pl.BlockSpec((bb, tr, w), lambda bi, i: (bi, i, 0))
    out_specs = [row_spec(2 * FOURIER_WIDTH), row_spec(ATTN_WIDTH), row_spec(KV_WIDTH),
                 pl.BlockSpec((bb, KV_WIDTH, tr), lambda bi, i: (bi, 0, i))]
    out_shape = [
        jax.ShapeDtypeStruct((b, s, 2 * FOURIER_WIDTH), BF16),
        jax.ShapeDtypeStruct((b, s, ATTN_WIDTH), BF16),
        jax.ShapeDtypeStruct((b, s, KV_WIDTH), BF16 if rope else F32),
        jax.ShapeDtypeStruct((b, KV_WIDTH, s), BF16),
    ]
    if not rope:
        out_specs.append(row_spec(KV_WIDTH))
        out_shape.append(jax.ShapeDtypeStruct((b, s, KV_WIDTH), F32))
    in_specs = [
        row_spec(D_MODEL),
        pl.BlockSpec((None, None, 6, D_MODEL), lambda bi, i: (0, mod_row(bi * bb), 0, 0)),
        _resident((D_MODEL, PROJ_WIDTH), lambda bi, i: (0, 0)),
        _resident((QK_WIDTH, QK_WIDTH), lambda bi, i: (0, 0)),
        _resident((1, QK_WIDTH), lambda bi, i: (0, 0)),
    ]
    args = [x, mod4, w_cat, head_sum, gains]
    if rope:
        in_specs += [_resident((s, LANES), lambda bi, i: (0, 0))] * 3
        args += list(rope_tabs)
    return pl.pallas_call(
        functools.partial(_in_proj_kernel, rope=rope, tiles=tiles),
        grid=(b // bb, s // tr),
        in_specs=in_specs,
        out_specs=out_specs,
        out_shape=out_shape,
        compiler_params=_params(2),
        name="in_proj_rope" if rope else "in_proj",
    )(*args)


def _attention_scores(q_ref, bu, row0, key_sets, sub_tile):
    nt = (((1,), (1,)), ((), ()))
    q = q_ref[bu, row0:row0 + sub_tile, :]
    items = []
    for j in range(N_KV_HEADS):
        kv = slice(j * HEAD_DIM, (j + 1) * HEAD_DIM)
        qs = jnp.concatenate(
            [q[:, h * HEAD_DIM:(h + 1) * HEAD_DIM] for h in range(j * Q_PER_KV, (j + 1) * Q_PER_KV)], axis=0)
        items.append([lax.dot_general(keys_ref[bu, :, kv].astype(BF16), qs, nt, preferred_element_type=F32)
                      for keys_ref, _ in key_sets])
    return items


def _attention_finish(items, bu, key_sets, attn_ref, arow0, sub_tile):
    for j, sts in enumerate(items):
        parts = []
        for st_all, (_, vts_ref) in zip(sts, key_sets):
            n_keys = st_all.shape[0]
            chunk = min(KEY_CHUNK, n_keys)
            ones_rows = jnp.ones((BF16_SUBLANES, chunk), BF16)
            for c in range(0, n_keys, chunk):
                st = st_all[c:c + chunk]
                m = jnp.max(st, axis=0, keepdims=True)
                vt_ext = jnp.concatenate(
                    [vts_ref[bu, j * HEAD_DIM:(j + 1) * HEAD_DIM, c:c + chunk].astype(BF16), ones_rows], axis=0)
                parts.append((m, _dot(vt_ext, jnp.exp2(st - m).astype(BF16))))
        m_all = functools.reduce(jnp.maximum, [m for m, _ in parts])
        ot = (functools.reduce(jnp.add, [jnp.exp2(m - m_all) * o for m, o in parts])
              if len(parts) > 1 else parts[0][1])
        o = ot[0:HEAD_DIM] / ot[HEAD_DIM:HEAD_DIM + 1]
        for g in range(0, Q_PER_KV, 2):
            pair = jnp.concatenate([o[:, g * sub_tile:(g + 1) * sub_tile],
                                    o[:, (g + 1) * sub_tile:(g + 2) * sub_tile]], axis=0).T
            lane0 = (j * Q_PER_KV + g) * HEAD_DIM
            attn_ref[arow0:arow0 + sub_tile, lane0:lane0 + LANES] = pair.astype(attn_ref.dtype)


def _attn_out_kernel(*refs, has_cache, tiles, sub_tile):
    if has_cache:
        (q_ref, k_ref, vt_ref, ck_ref, cvt_ref, x_ref, fcs_ref, dc_ref, ds_ref, w_ref, mod_ref, lng_ref, lnb_ref,
         o_ref, attn_ref) = refs
        key_sets = [(k_ref, vt_ref), (ck_ref, cvt_ref)]
    else:
        (q_ref, k_ref, vt_ref, x_ref, fcs_ref, dc_ref, ds_ref, w_ref, mod_ref, lng_ref, lnb_ref,
         o_ref, attn_ref) = refs
        key_sets = [(k_ref, vt_ref)]
    gate = mod_ref[2:3, :]
    subs_per_tile = ROW_TILE // sub_tile
    subs = [(n, bu, ru, t) for n, (bu, ru) in enumerate(tiles) for t in range(subs_per_tile)]
    scores = lambda n, bu, ru, t: _attention_scores(q_ref, bu, ru + t * sub_tile, key_sets, sub_tile)

    four = None
    items_next = scores(*subs[0])
    for i, (n, bu, ru, t) in enumerate(subs):
        items = items_next
        if i + 1 < len(subs):
            items_next = scores(*subs[i + 1])
        if t == 0:
            pos = pl.ds(pl.multiple_of(pl.program_id(1) * x_ref.shape[1] + ru, ROW_TILE), ROW_TILE)
            four = (_dot(dc_ref[pos, :], fcs_ref[bu, :, :FOURIER_WIDTH])
                    - _dot(ds_ref[pos, :], fcs_ref[bu, :, FOURIER_WIDTH:]))
        _attention_finish(items, bu, key_sets, attn_ref, n * ROW_TILE + t * sub_tile, sub_tile)
        if t == subs_per_tile - 1:
            rows = slice(ru, ru + ROW_TILE)
            out = (_dot(four.astype(BF16), w_ref[:FOURIER_WIDTH, :])
                   + _dot(attn_ref[n * ROW_TILE:(n + 1) * ROW_TILE, :], w_ref[FOURIER_WIDTH:, :]))
            y = DEEPNORM_ALPHA * x_ref[bu, rows, :] + gate * out
            o_ref[bu, rows, :] = _layer_norm(y, lng_ref[0:1, :], lnb_ref[0:1, :])


def _attn_out(x, fcs, q, k, vt, cache, dft_c, dft_s, w_out, mod4, mod_row, ln_g, ln_b, layer):
    b, s, _ = x.shape
    has_cache = cache is not None
    bb, tr, tiles = _step_blocking(b, s)
    row_spec = lambda w: pl.BlockSpec((bb, tr, w), lambda bi, i: (bi, i, 0))
    seq_spec = lambda n, w: pl.BlockSpec((bb, n, w), lambda bi, i: (bi, 0, 0))
    seq_t_spec = lambda n: pl.BlockSpec((bb, KV_WIDTH, n), lambda bi, i: (bi, 0, 0))
    ln_spec = pl.BlockSpec((None, 2, D_MODEL), lambda bi, i: (layer, 0, 0))
    in_specs = [row_spec(ATTN_WIDTH), seq_spec(s, KV_WIDTH), seq_t_spec(s)]
    args = [q, k, vt]
    if has_cache:
        past = cache[0].shape[1]
        in_specs += [seq_spec(past, KV_WIDTH), seq_t_spec(past)]
        args += list(cache)
    in_specs += [
        row_spec(D_MODEL),
        seq_spec(s, 2 * FOURIER_WIDTH),
        _resident((s, s), lambda bi, i: (0, 0)),
        _resident((s, s), lambda bi, i: (0, 0)),
        _resident((2 * FOURIER_WIDTH, D_MODEL), lambda bi, i: (0, 0)),
        pl.BlockSpec((None, None, 6, D_MODEL), lambda bi, i: (layer, mod_row(bi * bb), 0, 0)),
        ln_spec, ln_spec,
    ]
    args += [x, fcs, dft_c, dft_s, w_out, mod4, ln_g, ln_b]
    return pl.pallas_call(
        functools.partial(_attn_out_kernel, has_cache=has_cache, tiles=tiles, sub_tile=ATTN_SUB_TILE),
        grid=(b // bb, s // tr),
        in_specs=in_specs,
        out_specs=row_spec(D_MODEL),
        out_shape=jax.ShapeDtypeStruct((b, s, D_MODEL), F32),
        scratch_shapes=[pltpu.VMEM((bb * tr, ATTN_WIDTH), BF16)],
        compiler_params=_params(2),
        name="attn_out_cached" if has_cache else "attn_out",
    )(*args)


def _load_permuted(slab_ref):
    slabs = [jnp.concatenate([slab_ref[c, pl.ds(v, SUBLANES, stride=PERM_STRIDE), :] for v in range(PERM_STRIDE)],
                             axis=0) for c in range(N_SLABS)]
    return jnp.concatenate(slabs, axis=1)


def _store_unpermuted(y, slab_ref):
    for c in range(N_SLABS):
        for v in range(PERM_STRIDE):
            slab_ref[c, pl.ds(v, SUBLANES, stride=PERM_STRIDE), :] = (
                y[v * SUBLANES:(v + 1) * SUBLANES, c * LANES:(c + 1) * LANES])


def _perm_row(q):
    return SUBLANES * (q % PERM_STRIDE) + q // PERM_STRIDE


def _stage_block(blk, stream, streams, shift, scale, xe_ref, h_ref):
    sublane = lax.broadcasted_iota(jnp.int32, (SUBLANES, 1), 0)
    rows = lax.broadcasted_iota(jnp.int32, (BLOCK_ROWS, 1), 0)
    tiles = [blk * SEGS_PER_BLOCK + seg for seg in range(SEGS_PER_BLOCK)]

    def of_stream(pick):
        value = pick(*streams[0][0])
        for n in range(1, len(streams)):
            value = jnp.where(stream == n, pick(*streams[n][0]), value)
        return value

    for seg, u in enumerate(tiles):
        lo = u * ROW_TILE
        base = seg * SEG_ROWS
        w = tiles[seg - 1]
        after_lo = (w + 1) * ROW_TILE
        for c in range(N_SLABS):
            cols = slice(c * LANES, (c + 1) * LANES)
            before = of_stream(lambda x_ref, xp_ref, xn_ref:
                               xp_ref[:, cols] if u == 0 else x_ref[lo - HALO:lo, cols])
            after = of_stream(lambda x_ref, xp_ref, xn_ref:
                              xn_ref[0:1, cols] if w == CONV_TILES_PER_STEP - 1
                              else x_ref[after_lo:after_lo + 1, cols])
            xe_ref[blk, c, base:base + HALO, :] = jnp.where(sublane == 0, after, before)
            xe_ref[blk, c, base + HALO:base + SEG_ROWS, :] = of_stream(
                lambda x_ref, xp_ref, xn_ref: x_ref[lo:lo + ROW_TILE, cols])

    def in_sequence(second_tile):
        flag = 0.0
        for n, (_, first_step, tiles_per_seq) in enumerate(streams):
            if tiles_per_seq > 1:
                tile = (pl.program_id(0) - first_step) * CONV_TILES_PER_STEP + second_tile
                inside = jnp.logical_and(stream == n, lax.rem(tile, tiles_per_seq) != 0)
                flag = jnp.where(inside, 1.0, flag)
        return flag

    keep = jnp.ones((BLOCK_ROWS, 1), F32)
    for seg, u in enumerate(tiles):
        base = seg * SEG_ROWS
        w = tiles[seg - 1]
        for slot in range(HALO):
            value = in_sequence(u) if slot == HALO - 1 else in_sequence(w + 1) if slot == 0 else 0.0
            keep = jnp.where(rows == _perm_row(base + slot), value, keep)
    h = (_load_permuted(xe_ref.at[blk]) * (1.0 + scale) + shift) * keep
    h_ref[blk] = h.astype(BF16)


def _conv3_permuted(t, cw):
    first_prev = pltpu.roll(t[BLOCK_ROWS - SUBLANES:BLOCK_ROWS], 1, 0)
    prev = jnp.concatenate([first_prev, t[0:BLOCK_ROWS - SUBLANES]], axis=0)
    last_next = pltpu.roll(t[0:SUBLANES], SUBLANES - 1, 0)
    nxt = jnp.concatenate([t[SUBLANES:BLOCK_ROWS], last_next], axis=0)
    return cw[0:1, :] * prev + cw[1:2, :] * t + cw[2:3, :] * nxt


def _finish_block(blk, stream, out_refs, xe_ref, acc, gate, lng, lnb):
    y = DEEPNORM_ALPHA * _load_permuted(xe_ref.at[blk]) + gate * acc
    _store_unpermuted(_layer_norm(y, lng, lnb), xe_ref.at[blk])

    def copy_out(o_ref):
        for seg in range(SEGS_PER_BLOCK):
            lo = (blk * SEGS_PER_BLOCK + seg) * ROW_TILE
            base = seg * SEG_ROWS
            for c in range(N_SLABS):
                o_ref[lo:lo + ROW_TILE, c * LANES:(c + 1) * LANES] = xe_ref[blk, c, base + HALO:base + SEG_ROWS, :]

    copy_out(out_refs[-1])
    for n, o_ref in enumerate(out_refs[:-1]):
        pl.when(stream == n)(functools.partial(copy_out, o_ref))


def _split_conv_refs(refs, stream_steps):
    n_streams = len(stream_steps)
    x_refs = [refs[3 * n:3 * n + 3] for n in range(n_streams)]
    rest = refs[3 * n_streams:]
    stream = jnp.int32(0)
    for n in range(1, n_streams):
        stream = jnp.where(pl.program_id(0) >= stream_steps[n][0], n, stream)
    streams = [(x_refs[n], first, tps) for n, (first, tps) in enumerate(stream_steps)]
    return stream, streams, rest


def _ffn_kernel(*refs, stream_steps):
    stream, streams, rest = _split_conv_refs(refs, stream_steps)
    n = len(streams)
    mod_ref, wup_ref, cw_ref, wdn_ref, lng_ref, lnb_ref = rest[:6]
    out_refs, (xe_ref, h_ref, act_ref) = rest[6:6 + n], rest[6 + n:]
    for blk in range(BLOCKS_PER_STEP):
        _stage_block(blk, stream, streams, mod_ref[3:4, :], mod_ref[4:5, :], xe_ref, h_ref)
        for j in range(N_FF_CHUNKS):
            a_cols = slice(j * FF_CHUNK, (j + 1) * FF_CHUNK)
            g_cols = slice(D_FF + j * FF_CHUNK, D_FF + (j + 1) * FF_CHUNK)
            a = _conv3_permuted(_dot(h_ref[blk], wup_ref[:, a_cols]), cw_ref[:, a_cols])
            g = _conv3_permuted(_dot(h_ref[blk], wup_ref[:, g_cols]), cw_ref[:, g_cols])
            act_ref[blk, :, a_cols] = (g * _sigmoid(g) * a).astype(BF16)
    for blk in range(BLOCKS_PER_STEP):
        acc = _dot(act_ref[blk], wdn_ref[...])
        _finish_block(blk, stream, out_refs, xe_ref, acc, mod_ref[5:6, :], lng_ref[1:2, :], lnb_ref[1:2, :])


def _mixer_kernel(*refs, stream_steps):
    stream, streams, rest = _split_conv_refs(refs, stream_steps)
    n = len(streams)
    mod_ref, win_ref, cw_ref, wout_ref, lng_ref, lnb_ref = rest[:6]
    out_refs, (xe_ref, h_ref, act_ref) = rest[6:6 + n], rest[6 + n:]
    for blk in range(BLOCKS_PER_STEP):
        _stage_block(blk, stream, streams, mod_ref[0:1, :], mod_ref[1:2, :], xe_ref, h_ref)
        for j in range(N_MIX_CHUNKS):
            cols = slice(j * FF_CHUNK, (j + 1) * FF_CHUNK)
            bg, cg, xin = (_dot(h_ref[blk], win_ref[:, k * D_MODEL + j * FF_CHUNK:k * D_MODEL + (j + 1) * FF_CHUNK])
                           for k in range(3))
            y = bg * _conv3_permuted(cg * xin, cw_ref[:, cols])
            act_ref[blk, :, cols] = y.astype(BF16)
    for blk in range(BLOCKS_PER_STEP):
        acc = _dot(act_ref[blk], wout_ref[...])
        _finish_block(blk, stream, out_refs, xe_ref, acc, mod_ref[2:3, :], lng_ref[0:1, :], lnb_ref[0:1, :])


def _gated_conv_block(kernel_fn, name, xs, mod_rows, mod4, w1, cw, w2, ln_g, ln_b, layer, w_layer):
    hidden = w2.shape[1]
    step_rows = CONV_TILES_PER_STEP * ROW_TILE
    halo_blocks_per_step = step_rows // HALO
    stream_steps, in_specs, out_specs, out_shape, args = [], [], [], [], []
    first = 0
    for x in xs:
        b, s, _ = x.shape
        assert s % ROW_TILE == 0 and (s % step_rows == 0 or step_rows % s == 0)
        n_steps = b * s // step_rows
        last_halo_block = b * s // HALO - 1
        own = lambda i, first=first, n_steps=n_steps: jnp.clip(i - first, 0, n_steps - 1)
        in_specs += [
            pl.BlockSpec((step_rows, D_MODEL), lambda i, own=own: (own(i), 0)),
            pl.BlockSpec((HALO, D_MODEL),
                         lambda i, own=own: (jnp.maximum(own(i) * halo_blocks_per_step - 1, 0), 0)),
            pl.BlockSpec((HALO, D_MODEL),
                         lambda i, own=own, last=last_halo_block:
                         (jnp.minimum((own(i) + 1) * halo_blocks_per_step, last), 0)),
        ]
        out_specs.append(pl.BlockSpec((step_rows, D_MODEL), lambda i, own=own: (own(i), 0)))
        out_shape.append(jax.ShapeDtypeStruct((b * s, D_MODEL), F32))
        args += [x.reshape(b * s, D_MODEL)] * 3
        stream_steps.append((first, s // ROW_TILE))
        first += n_steps

    def mod_index(i):
        row = jnp.int32(0)
        for (start, _), x, mod_row in zip(stream_steps, xs, mod_rows):
            row = jnp.where(i >= start, mod_row((i - start) * step_rows // x.shape[1]), row)
        return (layer, row, 0, 0)

    ln_spec = pl.BlockSpec((None, 2, D_MODEL), lambda i: (layer, 0, 0))
    in_specs += [
        pl.BlockSpec((None, None, 6, D_MODEL), mod_index),
        _resident((None,) + w1.shape[1:], lambda i: (w_layer, 0, 0)),
        _resident((None,) + cw.shape[1:], lambda i: (w_layer, 0, 0)),
        _resident((None,) + w2.shape[1:], lambda i: (w_layer, 0, 0)),
        ln_spec, ln_spec,
    ]
    outs = pl.pallas_call(
        functools.partial(kernel_fn, stream_steps=tuple(stream_steps)),
        grid=(first,),
        in_specs=in_specs,
        out_specs=out_specs,
        out_shape=out_shape,
        scratch_shapes=[
            pltpu.VMEM((BLOCKS_PER_STEP, N_SLABS, BLOCK_ROWS, LANES), F32),
            pltpu.VMEM((BLOCKS_PER_STEP, BLOCK_ROWS, D_MODEL), BF16),
            pltpu.VMEM((BLOCKS_PER_STEP, BLOCK_ROWS, hidden), BF16),
        ],
        compiler_params=pltpu.CompilerParams(dimension_semantics=("arbitrary",),
                                             vmem_limit_bytes=CONV_VMEM_LIMIT_BYTES),
        name=name,
    )(*args, mod4, w1, cw, w2, ln_g, ln_b)
    return [o.reshape(x.shape) for o, x in zip(outs, xs)]


def _dft_tables(n, scale):
    jk = np.outer(np.arange(n), np.arange(n)) % n
    ang = 2.0 * np.pi * jk / n
    return (np.cos(ang) * scale).astype(np.float32), (np.sin(ang) * scale).astype(np.float32)


def _channel_dft_blocks():
    c, s = _dft_tables(FOURIER_GROUP, 1.0)
    eye = np.eye(MXU_WIDTH // FOURIER_GROUP, dtype=np.float32)
    return np.kron(eye, c), np.kron(eye, s)


def _rope_tables(n):
    half = HEAD_DIM // 2
    inv = 1.0 / (ROPE_THETA ** (np.arange(0, half, 2, dtype=np.float64) / half))
    pos = np.arange(n)
    d = np.arange(LANES) % HEAD_DIM
    coord = np.where(d < half, (pos // GRID_W)[:, None], (pos % GRID_W)[:, None])
    ang = coord * inv[d % (half // 2)][None, :]
    first = (d % half) < (half // 2)
    cos = np.cos(ang)
    sa = np.where(first[None, :], -np.sin(ang), 0.0)
    sb = np.where(first[None, :], 0.0, np.sin(ang))
    return tuple(jnp.asarray(t.astype(np.float32)) for t in (cos, sa, sb))


def _head_sum_matrix():
    return np.kron(np.eye(QK_WIDTH // HEAD_DIM, dtype=np.float32), np.ones((HEAD_DIM, HEAD_DIM), np.float32))


def kernel(x_prompt, x_sample, cache_k, cache_v, c, c_ctx, w_ada, b_ada, ln_g, ln_b, w_in_a, q_norm_g,
           k_norm_g, w_out_a, w_in_c, conv_c, w_out_c, w_up, conv_f, w_down):
    n_prompt, s_prompt, _ = x_prompt.shape
    n_sample, s_sample, _ = x_sample.shape
    past = cache_k.shape[2]

    cond = jnp.concatenate(
        [c_ctx[None, :], c, jnp.zeros((N_COND_ROWS - 1 - n_sample, D_MODEL), F32)], axis=0)
    mod4 = _modulation(cond, w_ada, b_ada).reshape(DEPTH, N_COND_ROWS, 6, D_MODEL)

    chan_c, chan_s = _channel_dft_blocks()
    w_cat = _fold_in_proj(w_in_a, jnp.asarray(chan_c), jnp.asarray(chan_s))
    head_sum = jnp.asarray(_head_sum_matrix()).astype(BF16)
    gains = jnp.concatenate([jnp.tile(q_norm_g[0], N_HEADS), jnp.tile(k_norm_g[0], N_KV_HEADS)])[None, :]
    w_out_bf = w_out_a[0].astype(BF16)

    def layer0_mixer(x, mod_row, cache):
        s = x.shape[1]
        scale = (FOURIER_GROUP * s) ** -0.5
        dft_c, dft_s = (jnp.asarray(t).astype(BF16) for t in _dft_tables(s, scale))
        rope_tabs = _rope_tables(s) if cache is not None else None
        fcs, q, k, vt, *v = _in_proj(x, mod4, mod_row, w_cat, head_sum, gains, rope_tabs)
        x = _attn_out(x, fcs, q, k, vt, cache, dft_c, dft_s, w_out_bf, mod4, mod_row, ln_g, ln_b, 0)
        return x, k, v

    mod_rows = (lambda bi: 0, lambda bi: bi + 1)
    cache = (cache_k[:, 0].reshape(n_sample, past, KV_WIDTH),
             cache_v[:, 0].reshape(n_sample, past, KV_WIDTH).transpose(0, 2, 1))
    xp, k_new, (v_new,) = layer0_mixer(x_prompt, mod_rows[0], None)
    xs, _, _ = layer0_mixer(x_sample, mod_rows[1], cache)
    xs = [xp, xs]
    xs = _gated_conv_block(_ffn_kernel, "conv_ffn", xs, mod_rows, mod4, w_up, conv_f, w_down, ln_g, ln_b, 0, 0)
    xs = _gated_conv_block(_mixer_kernel, "conv_mixer", xs, mod_rows, mod4, w_in_c, conv_c, w_out_c, ln_g, ln_b, 1, 0)
    y_prompt, y_sample = _gated_conv_block(_ffn_kernel, "conv_ffn", xs, mod_rows, mod4, w_up, conv_f, w_down,
                                           ln_g, ln_b, 1, 1)

    new_shape = (n_prompt, 1, s_prompt, N_KV_HEADS, HEAD_DIM)
    return y_prompt, y_sample, k_new.reshape(new_shape), v_new.reshape(new_shape)
```
